```python
import jax, jax.numpy as jnp
from jax import lax
import numpy as np

D_MODEL = 1024
BATCH = 4
SEQ = 8192
DEPTH = 1

PLE_DIM = 256
HEAD_DIM = 64
A_HEADS = 8
A_CONFIGS = ((128, 1), (512, 4), (2048, 16))
B_HEADS = 8
B_KV_GROUPS = 2
B_REP = B_HEADS // B_KV_GROUPS
CMP_LEN = 32
CMP_STRIDE = 16
CMP_HIDDEN = 256
SEL_LEN = 64
SEL_TOPK = 16
WIN = 512
Q_BLK = 128
D_FF = 4 * D_MODEL
N_BRANCHES = 2
A_WIDTH = A_HEADS * HEAD_DIM
B_WIDTH = B_HEADS * HEAD_DIM
KV_WIDTH = B_KV_GROUPS * HEAD_DIM
C_IN = 3 * A_WIDTH + B_WIDTH + 6 * KV_WIDTH + 3 * B_HEADS + N_BRANCHES * D_MODEL
EPS = 1e-6
NEG = -1e30
FORCE = 1e9

kernel_name = 'hybrid_dilated_nsa_gated_block'


def rmsnorm(x, g):
    xf = x.astype(jnp.float32)
    inv = lax.rsqrt(jnp.mean(xf * xf, axis=-1, keepdims=True) + EPS)
    return (xf * inv).astype(x.dtype) * g


def alibi_slopes(n):
    return jnp.asarray([2.0 ** (-8.0 * (h + 1) / n) for h in range(n)], jnp.float32)


def masked_softmax(s, mask):
    s = jnp.where(mask, s, NEG)
    m = jnp.max(s, axis=-1, keepdims=True)
    e = jnp.where(mask, jnp.exp(s - m), 0.0)
    denom = jnp.sum(e, axis=-1, keepdims=True)
    p = e / jnp.maximum(denom, 1e-30)
    lse = m[..., 0] + jnp.log(jnp.maximum(denom[..., 0], 1e-30))
    return p, lse


def split_input_projection(proj):
    sizes = [A_WIDTH] * 3 + [B_WIDTH] + [KV_WIDTH] * 6 + [3 * B_HEADS] + [D_MODEL] * N_BRANCHES
    points = [int(v) for v in np.cumsum(sizes)[:-1]]
    return jnp.split(proj, points, axis=-1)


def dilated_window_attention(q, k, v):
    B, S, H, Dh = q.shape
    scale = HEAD_DIM ** -0.5
    slopes = alibi_slopes(A_HEADS)[:, None, None, None]
    outs, lses = [], []
    for window, dil in A_CONFIGS:
        W = window // dil
        L = S // dil
        blk = min(Q_BLK, L)
        nblk = L // blk

        def strided(t):
            return t.reshape(B, L, dil, H, Dh).transpose(0, 2, 3, 1, 4)

        qs = strided(q).reshape(B, dil, H, nblk, blk, Dh)
        pad = ((0, 0), (0, 0), (0, 0), (W, 0), (0, 0))
        kp = jnp.pad(strided(k), pad)
        vp = jnp.pad(strided(v), pad)
        idx = jnp.arange(nblk)[:, None] * blk + jnp.arange(blk + W)[None, :]
        kb = kp[:, :, :, idx]
        vb = vp[:, :, :, idx]
        i = jnp.arange(blk)[:, None]
        j = jnp.arange(blk + W)[None, :]
        dist = i - j + W
        key_pos = jnp.arange(nblk)[:, None, None] * blk + j[None] - W
        mask = (dist >= 0) & (dist <= W) & (key_pos >= 0)
        s = jnp.einsum('brhnqd,brhnkd->brhnqk', qs, kb).astype(jnp.float32) * scale
        s = s - slopes * (dist * dil).astype(jnp.float32)
        p, lse = masked_softmax(s, mask)
        o = jnp.einsum('brhnqk,brhnkd->brhnqd', p.astype(vb.dtype), vb)
        outs.append(o.reshape(B, dil, H, L, Dh).transpose(0, 3, 1, 2, 4).reshape(B, S, H, Dh))
        lses.append(lse.reshape(B, dil, H, L).transpose(0, 3, 1, 2).reshape(B, S, H))
    w = jax.nn.softmax(jnp.stack(lses, axis=-1), axis=-1)
    return jnp.einsum('bshc,cbshd->bshd', w.astype(q.dtype), jnp.stack(outs, axis=0))


def compress_blocks(t, cidx, pe, w1, w2):
    B, G, _, Dh = t.shape
    blocks = t[:, :, cidx] + pe
    flat = blocks.reshape(B, G, cidx.shape[0], CMP_LEN * Dh)
    return jax.nn.gelu(flat @ w1) @ w2


def selection_overlap(ncmp, nsel):
    ratio = SEL_LEN // CMP_STRIDE
    span = CMP_LEN // CMP_STRIDE
    i = np.arange(ncmp)[:, None]
    j = np.arange(nsel)[None, :]
    ov = np.minimum(i + span, ratio * (j + 1)) - np.maximum(i, ratio * j)
    return np.maximum(ov, 0).astype(np.float32)


def native_sparse_attention(q, kc, vc, ks, vs, kw, vw, gates, pe_k, wk1, wk2, pe_v, wv1, wv2):
    B, S, _ = q.shape
    G, R, Dh = B_KV_GROUPS, B_REP, HEAD_DIM
    scale = HEAD_DIM ** -0.5
    q = q.reshape(B, S, G, R, Dh).transpose(0, 2, 3, 1, 4)

    def kv(t):
        return t.reshape(B, S, G, Dh).transpose(0, 2, 1, 3)

    kc, vc, ks, vs, kw, vw = kv(kc), kv(vc), kv(ks), kv(vs), kv(kw), kv(vw)
    ncmp = (S - CMP_LEN) // CMP_STRIDE + 1
    cidx = jnp.arange(ncmp)[:, None] * CMP_STRIDE + jnp.arange(CMP_LEN)[None, :]
    cmp_end = cidx[:, -1]
    k_cmp = compress_blocks(kc, cidx, pe_k, wk1, wk2)
    v_cmp = compress_blocks(vc, cidx, pe_v, wv1, wv2)
    nsel = S // SEL_LEN
    overlap = jnp.asarray(selection_overlap(ncmp, nsel))
    topk = min(SEL_TOPK, nsel)
    ks_blk = ks.reshape(B, G, nsel, SEL_LEN, Dh)
    vs_blk = vs.reshape(B, G, nsel, SEL_LEN, Dh)
    kw_pad = jnp.pad(kw, ((0, 0), (0, 0), (WIN, 0), (0, 0)))
    vw_pad = jnp.pad(vw, ((0, 0), (0, 0), (WIN, 0), (0, 0)))
    slopes = alibi_slopes(B_HEADS).reshape(G, R)[None, :, :, None, None]
    blk_id = jnp.arange(nsel)
    bi = jnp.arange(B)[:, None, None, None]
    gi = jnp.arange(G)[None, :, None, None]

    def one_block(q0):
        qb = lax.dynamic_slice_in_dim(q, q0, Q_BLK, axis=3)
        t = q0 + jnp.arange(Q_BLK)
        dist_c = t[:, None] - cmp_end[None, :]
        s = jnp.einsum('bgrqd,bgnd->bgrqn', qb, k_cmp).astype(jnp.float32) * scale
        s = s - slopes * dist_c.astype(jnp.float32)
        p_cmp, _ = masked_softmax(s, dist_c >= 0)
        o_cmp = jnp.einsum('bgrqn,bgnd->bgrqd', p_cmp.astype(v_cmp.dtype), v_cmp)
        imp = jnp.einsum('bgrqn,nj->bgqj', p_cmp, overlap)
        cur = t // SEL_LEN
        forced = (blk_id[None, :] == 0) | (blk_id[None, :] == cur[:, None]) | (blk_id[None, :] == cur[:, None] - 1)
        allowed = blk_id[None, :] * SEL_LEN <= t[:, None]
        rank = jnp.where(allowed, imp + jnp.where(forced, FORCE, 0.0), NEG)
        _, sel = lax.top_k(rank, topk)
        kg = ks_blk[bi, gi, sel].reshape(B, G, Q_BLK, topk * SEL_LEN, Dh)
        vg = vs_blk[bi, gi, sel].reshape(B, G, Q_BLK, topk * SEL_LEN, Dh)
        pos = (sel[..., None] * SEL_LEN + jnp.arange(SEL_LEN)).reshape(B, G, Q_BLK, topk * SEL_LEN)
        dist_s = (t[None, None, :, None] - pos)[:, :, None]
        s = jnp.einsum('bgrqd,bgqkd->bgrqk', qb, kg).astype(jnp.float32) * scale
        s = s - slopes * dist_s.astype(jnp.float32)
        p_slc, _ = masked_softmax(s, dist_s >= 0)
        o_slc = jnp.einsum('bgrqk,bgqkd->bgrqd', p_slc.astype(vg.dtype), vg)
        kwb = lax.dynamic_slice_in_dim(kw_pad, q0, Q_BLK + WIN, axis=2)
        vwb = lax.dynamic_slice_in_dim(vw_pad, q0, Q_BLK + WIN, axis=2)
        kpos = q0 - WIN + jnp.arange(Q_BLK + WIN)
        dist_w = t[:, None] - kpos[None, :]
        mask_w = (dist_w >= 0) & (dist_w < WIN) & (kpos[None, :] >= 0)
        s = jnp.einsum('bgrqd,bgkd->bgrqk', qb, kwb).astype(jnp.float32) * scale
        s = s - slopes * dist_w.astype(jnp.float32)
        p_win, _ = masked_softmax(s, mask_w)
        o_win = jnp.einsum('bgrqk,bgkd->bgrqd', p_win.astype(vwb.dtype), vwb)
        return jnp.stack([o_cmp, o_slc, o_win], axis=-2)

    n_qblk = S // Q_BLK
    out = lax.map(one_block, jnp.arange(n_qblk) * Q_BLK)
    out = out.transpose(1, 0, 4, 2, 3, 5, 6).reshape(B, S, B_HEADS, 3, Dh)
    g = jax.nn.sigmoid(gates.reshape(B, S, B_HEADS, 3))
    return jnp.einsum('bshc,bshcd->bshd', g.astype(out.dtype), out)


def setup_inputs(seed: int = 0) -> dict:
    key = jax.random.key(seed)
    ks = jax.random.split(key, 20)
    f32 = jnp.float32

    def nrm(k, shape, fan_in):
        return jax.random.normal(k, shape, f32) * (fan_in ** -0.5)

    def gain(k, shape):
        return 1.0 + 0.02 * jax.random.normal(k, shape, f32)

    return {
        'x': jax.random.normal(ks[0], (BATCH, SEQ, D_MODEL), f32),
        'p': jax.random.normal(ks[1], (DEPTH, BATCH, SEQ, PLE_DIM), f32),
        'norm_mix_g': gain(ks[2], (DEPTH, D_MODEL)),
        'w_in': nrm(ks[3], (DEPTH, D_MODEL, C_IN), D_MODEL),
        'pe_ck': 0.1 * jax.random.normal(ks[4], (DEPTH, CMP_LEN, HEAD_DIM), f32),
        'w_ck1': nrm(ks[5], (DEPTH, CMP_LEN * HEAD_DIM, CMP_HIDDEN), CMP_LEN * HEAD_DIM),
        'w_ck2': nrm(ks[6], (DEPTH, CMP_HIDDEN, HEAD_DIM), CMP_HIDDEN),
        'pe_cv': 0.1 * jax.random.normal(ks[7], (DEPTH, CMP_LEN, HEAD_DIM), f32),
        'w_cv1': nrm(ks[8], (DEPTH, CMP_LEN * HEAD_DIM, CMP_HIDDEN), CMP_LEN * HEAD_DIM),
        'w_cv2': nrm(ks[9], (DEPTH, CMP_HIDDEN, HEAD_DIM), CMP_HIDDEN),
        'w_up_a': nrm(ks[10], (DEPTH, A_WIDTH, D_MODEL), A_WIDTH),
        'w_up_b': nrm(ks[11], (DEPTH, B_WIDTH, D_MODEL), B_WIDTH),
        'w_out': nrm(ks[12], (DEPTH, D_MODEL, D_MODEL), D_MODEL),
        'norm_mlp_g': gain(ks[13], (DEPTH, D_MODEL)),
        'w_mlp1': nrm(ks[14], (DEPTH, D_MODEL, D_FF), D_MODEL),
        'w_mlp2': nrm(ks[15], (DEPTH, D_FF, D_MODEL), D_FF),
        'norm_ple_g': gain(ks[16], (DEPTH, D_MODEL)),
        'w_ple_gate': nrm(ks[17], (DEPTH, D_MODEL, D_MODEL), D_MODEL),
        'w_ple': nrm(ks[18], (DEPTH, PLE_DIM, D_MODEL), PLE_DIM),
        'norm_final_g': gain(ks[19], (D_MODEL,)),
    }


def reference(x, p, norm_mix_g, w_in, pe_ck, w_ck1, w_ck2, pe_cv, w_cv1, w_cv2,
              w_up_a, w_up_b, w_out, norm_mlp_g, w_mlp1, w_mlp2,
              norm_ple_g, w_ple_gate, w_ple, norm_final_g):
    B, S, _ = x.shape
    h = x
    for i in range(DEPTH):
        n = rmsnorm(h, norm_mix_g[i])
        (qa, ka, va, qb, kc, vc, ksl, vsl, kwn, vwn,
         nsa_gates, gate_a, gate_b) = split_input_projection(n @ w_in[i])
        o_a = dilated_window_attention(qa.reshape(B, S, A_HEADS, HEAD_DIM),
                                       ka.reshape(B, S, A_HEADS, HEAD_DIM),
                                       va.reshape(B, S, A_HEADS, HEAD_DIM))
        o_b = native_sparse_attention(qb, kc, vc, ksl, vsl, kwn, vwn, nsa_gates,
                                      pe_ck[i], w_ck1[i], w_ck2[i], pe_cv[i], w_cv1[i], w_cv2[i])
        y_a = o_a.reshape(B, S, A_WIDTH) @ w_up_a[i]
        y_b = o_b.reshape(B, S, B_WIDTH) @ w_up_b[i]
        mixed = jax.nn.sigmoid(gate_a) * y_a + jax.nn.sigmoid(gate_b) * y_b
        h = h + mixed @ w_out[i]
        n2 = rmsnorm(h, norm_mlp_g[i])
        h = h + jnp.square(jax.nn.relu(n2 @ w_mlp1[i])) @ w_mlp2[i]
        gate = jax.nn.sigmoid(rmsnorm(h, norm_ple_g[i]) @ w_ple_gate[i])
        h = h + gate * (p[i] @ w_ple[i])
    return rmsnorm(h, norm_final_g)
```

```python
import functools

import numpy as np
import jax
import jax.numpy as jnp
from jax import lax
from jax.experimental import pallas as pl
from jax.experimental.pallas import tpu as pltpu

HEAD_DIM = 64
A_HEADS = 8
A_CONFIGS = ((128, 1), (512, 4), (2048, 16))
B_HEADS = 8
B_KV_GROUPS = 2
B_REP = B_HEADS // B_KV_GROUPS
CMP_LEN = 32
CMP_STRIDE = 16
CMP_HIDDEN = 256
SEL_LEN = 64
SEL_TOPK = 16
WIN = 512
Q_BLK = 128
EPS = 1e-6
NEG = -1e30
FORCE = 1e9
MASK_BIG = 2.0 ** 100
SCALE = HEAD_DIM ** -0.5

LANES = 128
A_TILE = 2048
SLC_TILE = 512
VMEM_LIMIT = 56 * 1024 * 1024

LANE_POS_BLK = 64
LANE_POS_IN = 65
LANE_CMP_HI = 66
LANE_CMP_LO = 67

F32 = jnp.float32
BF16 = jnp.bfloat16


def _dot(a, b):
    return jnp.dot(a, b, preferred_element_type=F32)


def _dot_nt(a, b):
    return lax.dot_general(a, b, (((1,), (1,)), ((), ())), preferred_element_type=F32)


def _rms(x, g):
    inv = lax.rsqrt(jnp.mean(x * x, axis=-1, keepdims=True) + EPS)
    return (x * inv) * g


def _iota(shape, dim, dtype=jnp.int32):
    return lax.broadcasted_iota(dtype, shape, dim)


def _proj_body(x_ref, g_ref, wa_ref, wc_ref, wq_ref, wkv_ref, wng_ref, wgab_ref,
               pa_ref, kvc_ref, qb_ref, kvb_ref, ng_ref, gab_ref, *, tm, s_len):
    n = _rms(x_ref[...], g_ref[...]).astype(BF16)
    pa_ref[...] = _dot(n, wa_ref[...])
    kvc_ref[...] = _dot(n, wc_ref[...])
    qb_ref[...] = _dot(n, wq_ref[...]).astype(BF16)
    ng_ref[...] = _dot(n, wng_ref[...])
    gab_ref[...] = _dot(n, wgab_ref[...]).astype(BF16)
    pos = (pl.program_id(0) * tm) % s_len + _iota((tm, LANES), 0)
    lane = _iota((tm, LANES), 1)
    posc = jnp.where(lane == LANE_POS_BLK, (pos // SEL_LEN).astype(F32),
                     jnp.where(lane == LANE_POS_IN, (pos % SEL_LEN).astype(F32), 0.0))
    kv = _dot(n, wkv_ref[...])
    for c in range(6):
        blk = kv[:, c * LANES:(c + 1) * LANES]
        if c in (0, 1, 3, 4):
            blk = blk + posc
        kvb_ref[:, c * LANES:(c + 1) * LANES] = blk.astype(BF16)


def _proj(x2, g, wa, wc, wq, wkv, wng, wgab, *, s_len, tm=256):
    t_len, d = x2.shape
    const = lambda i: (0, 0)
    row = lambda i: (i, 0)
    ws = (wa, wc, wq, wkv, wng, wgab)
    out_dtypes = (F32, F32, BF16, BF16, F32, BF16)
    return pl.pallas_call(
        functools.partial(_proj_body, tm=tm, s_len=s_len),
        grid=(t_len // tm,),
        in_specs=[pl.BlockSpec((tm, d), row), pl.BlockSpec((1, d), const)]
                 + [pl.BlockSpec(w.shape, const) for w in ws],
        out_specs=[pl.BlockSpec((tm, w.shape[1]), row) for w in ws],
        out_shape=[jax.ShapeDtypeStruct((t_len, w.shape[1]), dt) for w, dt in zip(ws, out_dtypes)],
        compiler_params=pltpu.CompilerParams(dimension_semantics=("arbitrary",),
                                             vmem_limit_bytes=VMEM_LIMIT),
        name="proj",
    )(x2, g, *ws)


def _gelu_tanh(x):
    return 0.5 * x * (1.0 + jnp.tanh(np.sqrt(2.0 / np.pi).astype(np.float32) * (x + 0.044715 * (x * x * x))))


def _compress_body(ch_ref, pe_ref, w1_ref, w2_ref, out_ref, *, ncp):
    ch = ch_ref[...]
    xa = (ch + pe_ref[0:1, :]).astype(BF16)
    xb = (ch + pe_ref[1:2, :]).astype(BF16)
    a = _dot(xa, w1_ref[0])
    b = _dot(xb, w1_ref[1])
    pre = a + jnp.concatenate([b[1:], b[:1]], axis=0)
    hid = _gelu_tanh(pre).astype(BF16)
    out = _dot(hid, w2_ref[...])
    kv_is_key = pl.program_id(1) < B_KV_GROUPS
    n_idx = _iota((ncp, LANES), 0)
    lane = _iota((ncp, LANES), 1)
    nc = jnp.where(lane == LANE_CMP_HI, (n_idx // 16).astype(F32),
                   jnp.where(lane == LANE_CMP_LO, (n_idx % 16).astype(F32), 0.0))
    out = out + jnp.where(kv_is_key, nc, 0.0)
    out_ref[...] = out.astype(BF16)


def _compress(chunks, pe2, w1s, w2s):
    bsz, nslot, ncp, width = chunks.shape
    return pl.pallas_call(
        functools.partial(_compress_body, ncp=ncp),
        grid=(bsz, nslot),
        in_specs=[pl.BlockSpec((None, None, ncp, width), lambda b, s: (b, s, 0, 0)),
                  pl.BlockSpec((None, 2, width), lambda b, s: (s // B_KV_GROUPS, 0, 0)),
                  pl.BlockSpec((None, 2, width, CMP_HIDDEN), lambda b, s: (s // B_KV_GROUPS, 0, 0, 0)),
                  pl.BlockSpec((None, CMP_HIDDEN, LANES), lambda b, s: (s, 0, 0))],
        out_specs=pl.BlockSpec((None, None, ncp, LANES), lambda b, s: (b, s, 0, 0)),
        out_shape=jax.ShapeDtypeStruct((bsz, nslot, ncp, LANES), BF16),
        compiler_params=pltpu.CompilerParams(dimension_semantics=("arbitrary", "arbitrary"),
                                             vmem_limit_bytes=VMEM_LIMIT),
        name="compress",
    )(chunks, pe2, w1s, w2s)


def _head_slope(h, n_heads):
    out = jnp.float32(2.0 ** (-8.0 * n_heads / n_heads))
    for k in range(n_heads - 1):
        out = jnp.where(h == k, jnp.float32(2.0 ** (-8.0 * (k + 1) / n_heads)), out)
    return out


def _dilated_body(q_ref, kp_ref, kc_ref, vp_ref, vc_ref, out_ref, o_sc, l_sc):
    hp = pl.program_id(1)
    first_tile = pl.program_id(2) == 0
    lane = _iota((1, LANES), 1)
    low = lane < HEAD_DIM
    qi = _iota((Q_BLK, 2 * Q_BLK), 0)
    kj = _iota((Q_BLK, 2 * Q_BLK), 1)
    dist = qi - kj + Q_BLK
    kj_row = _iota((1, 2 * Q_BLK), 1)

    for c, (window, dil) in enumerate(A_CONFIGS):
        w_steps = window // dil
        nsub = A_TILE // (Q_BLK * dil)
        valid = (dist >= 0) & (dist <= w_steps)
        biases = []
        for hh in range(2):
            slope = _head_slope(2 * hp + hh, A_HEADS)
            biases.append(jnp.where(valid, -(slope * dil) * dist.astype(F32), NEG))

        def block(idx, carry, c=c, dil=dil, nsub=nsub, biases=biases):
            r = idx // nsub
            j = idx % nsub
            row0 = j * (Q_BLK * dil) + r
            rowp = jnp.maximum(j - 1, 0) * (Q_BLK * dil) + r
            rowl = (nsub - 1) * (Q_BLK * dil) + r
            sl = lambda start: pl.ds(start, Q_BLK, stride=dil) if dil > 1 else pl.ds(start, Q_BLK)
            q = q_ref[sl(row0), :] * SCALE
            k_cur = kc_ref[sl(row0), :]
            v_cur = vc_ref[sl(row0), :]
            is_first = j == 0
            k_prev = jnp.where(is_first, kp_ref[sl(rowl), :], kc_ref[sl(rowp), :])
            v_prev = jnp.where(is_first, vp_ref[sl(rowl), :], vc_ref[sl(rowp), :])
            k2 = jnp.concatenate([k_prev, k_cur], axis=0).astype(BF16)
            v2 = jnp.concatenate([v_prev, v_cur], axis=0).astype(BF16)
            no_prev = jnp.where(is_first & first_tile, NEG, 0.0)
            edge = jnp.where(kj_row < Q_BLK, no_prev, 0.0)
            outs, lses = [], []
            for hh in range(2):
                qm = jnp.where(low == (hh == 0), q, 0.0).astype(BF16)
                s = _dot_nt(qm, k2) + biases[hh] + edge
                m = jnp.max(s, axis=-1, keepdims=True)
                e = jnp.exp(s - m)
                l = jnp.sum(e, axis=-1, keepdims=True)
                outs.append(_dot(e.astype(BF16), v2) / l)
                lses.append(jnp.broadcast_to(m + jnp.log(l), (Q_BLK, LANES)))
            o_sc[c, sl(row0), :] = jnp.where(low, outs[0], outs[1])
            l_sc[c, sl(row0), :] = jnp.where(low, lses[0], lses[1])
            return carry

        lax.fori_loop(0, dil * nsub, block, 0)

    l0, l1, l2 = l_sc[0], l_sc[1], l_sc[2]
    m = jnp.maximum(jnp.maximum(l0, l1), l2)
    e0, e1, e2 = jnp.exp(l0 - m), jnp.exp(l1 - m), jnp.exp(l2 - m)
    mix = (e0 * o_sc[0] + e1 * o_sc[1] + e2 * o_sc[2]) / (e0 + e1 + e2)
    out_ref[...] = mix.astype(BF16)


def _dilated(pa, *, bsz, s_len):
    nt = s_len // A_TILE
    npair = A_HEADS // 2
    cur = lambda off: (lambda b, h, t: (b * nt + t, off + h))
    prev = lambda off: (lambda b, h, t: (b * nt + jnp.maximum(t - 1, 0), off + h))
    blk = (A_TILE, LANES)
    return pl.pallas_call(
        _dilated_body,
        grid=(bsz, npair, nt),
        in_specs=[pl.BlockSpec(blk, cur(0)),
                  pl.BlockSpec(blk, prev(npair)), pl.BlockSpec(blk, cur(npair)),
                  pl.BlockSpec(blk, prev(2 * npair)), pl.BlockSpec(blk, cur(2 * npair))],
        out_specs=pl.BlockSpec(blk, cur(0)),
        out_shape=jax.ShapeDtypeStruct((bsz * s_len, A_HEADS * HEAD_DIM), BF16),
        scratch_shapes=[pltpu.VMEM((len(A_CONFIGS), A_TILE, LANES), F32),
                        pltpu.VMEM((len(A_CONFIGS), A_TILE, LANES), F32)],
        compiler_params=pltpu.CompilerParams(dimension_semantics=("arbitrary",) * 3,
                                             vmem_limit_bytes=VMEM_LIMIT),
        name="dilated",
    )(pa, pa, pa, pa, pa)


def _nsa_body(q_ref, ks0_ref, ks1_ref, vs_ref, kw0_ref, kw1_ref, vw_ref, cmp_ref, ng_ref,
              oh_ref, ovt_ref, out_ref, *, s_len, ncp, topk):
    q0 = pl.program_id(1) * Q_BLK
    lane = _iota((1, LANES), 1)
    low = lane < HEAD_DIM
    t_col = q0 + _iota((Q_BLK, 1), 0)
    t_row = q0 + _iota((1, Q_BLK), 1)
    nsel_pad = LANES
    win_keys = WIN + Q_BLK

    cmp_end = CMP_STRIDE * _iota((1, ncp), 1) + (CMP_LEN - 1)
    mask_c = cmp_end <= t_col
    bias_c = jnp.where(mask_c, 0.0, NEG)
    kstart = pl.multiple_of(jnp.maximum(q0 - WIN, 0), Q_BLK)
    d_w = t_col - (kstart + _iota((1, win_keys), 1))
    bias_w = jnp.where((d_w >= 0) & (d_w < WIN), 0.0, NEG)
    blk_t = _iota((nsel_pad, 1), 0)
    allowed_t = blk_t * SEL_LEN <= t_row
    cur_t = t_row // SEL_LEN
    forced_t = (blk_t == 0) | (blk_t == cur_t) | (blk_t == cur_t - 1)
    blk_f = blk_t.astype(F32)

    sig = jax.nn.sigmoid(ng_ref[...])
    q_all = q_ref[...].astype(F32)
    ks_refs = (ks0_ref, ks1_ref)
    kw_refs = (kw0_ref, kw1_ref)
    n_full = q0 // SLC_TILE
    head_out = [None] * B_HEADS

    for g in range(B_KV_GROUPS):
        rows = []
        for r in range(B_REP):
            h = g * B_REP + r
            slope = 2.0 ** (-8.0 * (h + 1) / B_HEADS)
            blk = q_all[:, (h // 2) * LANES:(h // 2 + 1) * LANES]
            if h % 2 == 1:
                blk = pltpu.roll(blk, HEAD_DIM, axis=1)
            cols = jnp.where(lane == LANE_POS_BLK, SEL_LEN * slope,
                             jnp.where(lane == LANE_POS_IN, slope,
                                       jnp.where(lane == LANE_CMP_HI, 16 * CMP_STRIDE * slope,
                                                 jnp.where(lane == LANE_CMP_LO, CMP_STRIDE * slope, 0.0))))
            rows.append(jnp.where(low, blk * SCALE, cols))
        qg = jnp.concatenate(rows, axis=0).astype(BF16)

        s = _dot_nt(qg, cmp_ref[g]).reshape(B_REP, Q_BLK, ncp) + bias_c[None]
        m = jnp.max(s, axis=-1, keepdims=True)
        e = jnp.where(mask_c[None], jnp.exp(s - m), 0.0)
        den = jnp.sum(e, axis=-1, keepdims=True)
        p = e / jnp.maximum(den, 1e-30)
        o_cmp = _dot(p.reshape(B_REP * Q_BLK, ncp).astype(BF16), cmp_ref[B_KV_GROUPS + g])
        psum = p[0] + p[1] + p[2] + p[3]
        p_hi = psum.astype(BF16)
        p_lo = (psum - p_hi.astype(F32)).astype(BF16)
        imp_t = _dot_nt(ovt_ref[...], p_hi) + _dot_nt(ovt_ref[...], p_lo)

        rank = jnp.where(allowed_t, imp_t + jnp.where(forced_t, FORCE, 0.0), NEG)

        def pick_one(_, carry):
            rank, sel = carry
            best = jnp.max(rank, axis=0, keepdims=True)
            cand = jnp.where(rank == best, blk_f, float(nsel_pad))
            idx = jnp.min(cand, axis=0, keepdims=True)
            pick = blk_f == idx
            return jnp.where(pick, -3e38, rank), jnp.where(pick, 1.0, sel)

        _, sel_t = lax.fori_loop(0, topk, pick_one, (rank, jnp.zeros((nsel_pad, Q_BLK), F32)))
        sel_bias = jnp.where(sel_t.T > 0.5, 0.0, -MASK_BIG).astype(BF16)
        q_slc = jnp.concatenate([qg, jnp.concatenate([sel_bias] * B_REP, axis=0)], axis=1)

        ks_ref = ks_refs[g]

        def slc_tile(kt, carry, diag, ks_ref=ks_ref, q_slc=q_slc):
            m_i, l_i, acc = carry
            k0 = pl.multiple_of(kt * SLC_TILE, SLC_TILE)
            k_aug = jnp.concatenate([ks_ref[pl.ds(k0, SLC_TILE), :], oh_ref[pl.ds(k0, SLC_TILE), :]], axis=1)
            s = _dot_nt(q_slc, k_aug)
            if diag:
                causal = (k0 + _iota((1, SLC_TILE), 1)) <= t_col
                s = (s.reshape(B_REP, Q_BLK, SLC_TILE) + jnp.where(causal, 0.0, NEG)[None]).reshape(
                    B_REP * Q_BLK, SLC_TILE)
            m_new = jnp.maximum(m_i, jnp.max(s, axis=-1, keepdims=True))
            alpha = jnp.exp(m_i - m_new)
            pe = jnp.exp(s - m_new)
            l_new = alpha * l_i + jnp.sum(pe, axis=-1, keepdims=True)
            acc_new = alpha * acc + _dot(pe.astype(BF16), vs_ref[pl.ds(k0, SLC_TILE), :])
            return m_new, l_new, acc_new

        init = (jnp.full((B_REP * Q_BLK, 1), -3e38, F32), jnp.zeros((B_REP * Q_BLK, 1), F32),
                jnp.zeros((B_REP * Q_BLK, LANES), F32))
        carry = lax.fori_loop(0, n_full, functools.partial(slc_tile, diag=False), init)
        _, l_s, acc_s = slc_tile(n_full, carry, True)
        o_slc = acc_s / l_s

        s = _dot_nt(qg, kw_refs[g][pl.ds(kstart, win_keys), :]).reshape(B_REP, Q_BLK, win_keys) + bias_w[None]
        m = jnp.max(s, axis=-1, keepdims=True)
        e = jnp.exp(s - m)
        den = jnp.sum(e, axis=-1, keepdims=True)
        o_win = _dot(e.reshape(B_REP * Q_BLK, win_keys).astype(BF16), vw_ref[pl.ds(kstart, win_keys), :])
        o_win = o_win / den.reshape(B_REP * Q_BLK, 1)

        for r in range(B_REP):
            h = g * B_REP + r
            rs = slice(r * Q_BLK, (r + 1) * Q_BLK)
            o = (sig[:, 3 * h:3 * h + 1] * o_cmp[rs] + sig[:, 3 * h + 1:3 * h + 2] * o_slc[rs]
                 + sig[:, 3 * h + 2:3 * h + 3] * o_win[rs])
            if h % 2 != g:
                o = pltpu.roll(o, HEAD_DIM, axis=1)
            head_out[h] = o

    for hp in range(B_HEADS // 2):
        out_ref[:, hp * LANES:(hp + 1) * LANES] = jnp.where(low, head_out[2 * hp], head_out[2 * hp + 1]).astype(BF16)


def _nsa(qb, kvb, cmp, ng, onehot, ovt, *, bsz, s_len):
    nq = s_len // Q_BLK
    ncp = cmp.shape[2]
    topk = min(SEL_TOPK, s_len // SEL_LEN)
    rowblk = lambda b, i: (b * nq + i, 0)
    res = lambda col: (lambda b, i: (b, col))
    return pl.pallas_call(
        functools.partial(_nsa_body, s_len=s_len, ncp=ncp, topk=topk),
        grid=(bsz, nq),
        in_specs=[pl.BlockSpec((Q_BLK, B_HEADS * HEAD_DIM), rowblk)]
                 + [pl.BlockSpec((s_len, LANES), res(col)) for col in range(6)]
                 + [pl.BlockSpec((None, 2 * B_KV_GROUPS, ncp, LANES), lambda b, i: (b, 0, 0, 0)),
                    pl.BlockSpec((Q_BLK, LANES), rowblk),
                    pl.BlockSpec(onehot.shape, lambda b, i: (0, 0)),
                    pl.BlockSpec(ovt.shape, lambda b, i: (0, 0))],
        out_specs=pl.BlockSpec((Q_BLK, B_HEADS * HEAD_DIM), rowblk),
        out_shape=jax.ShapeDtypeStruct((bsz * s_len, B_HEADS * HEAD_DIM), BF16),
        compiler_params=pltpu.CompilerParams(dimension_semantics=("arbitrary", "arbitrary"),
                                             vmem_limit_bytes=VMEM_LIMIT),
        name="nsa",
    )(qb, kvb, kvb, kvb, kvb, kvb, kvb, cmp, ng, onehot, ovt)


def _post_body(x_ref, oa_ref, ob_ref, gab_ref, p_ref, wua_ref, wub_ref, wout_ref, g2_ref, w1_ref, w2_ref,
               g3_ref, wpg_ref, wple_ref, gf_ref, out_ref, *, d, ff_chunk):
    ya = _dot(oa_ref[...], wua_ref[...])
    yb = _dot(ob_ref[...], wub_ref[...])
    mixed = (jax.nn.sigmoid(gab_ref[:, :d].astype(F32)) * ya
             + jax.nn.sigmoid(gab_ref[:, d:].astype(F32)) * yb)
    h = x_ref[...] + _dot(mixed.astype(BF16), wout_ref[...])
    n2 = _rms(h, g2_ref[...]).astype(BF16)
    acc = h
    for c in range(w1_ref.shape[1] // ff_chunk):
        cs = slice(c * ff_chunk, (c + 1) * ff_chunk)
        hid = jnp.square(jnp.maximum(_dot(n2, w1_ref[:, cs]), 0.0))
        acc = acc + _dot(hid.astype(BF16), w2_ref[cs, :])
    n3 = _rms(acc, g3_ref[...]).astype(BF16)
    gate = jax.nn.sigmoid(_dot(n3, wpg_ref[...]))
    h3 = acc + gate * _dot(p_ref[...].astype(BF16), wple_ref[...])
    out_ref[...] = _rms(h3, gf_ref[...])


def _post(x2, oa, ob, gab, p2, wua, wub, wout, g2, w1, w2, g3, wpg, wple, gf, *, tm=256, ff_chunk=1024):
    t_len, d = x2.shape
    row = lambda i: (i, 0)
    const = lambda i: (0, 0)
    resident = lambda a: pl.BlockSpec(a.shape, const, pipeline_mode=pl.Buffered(1))
    acts = (x2, oa, ob, gab, p2)
    params = (wua, wub, wout, g2, w1, w2, g3, wpg, wple, gf)
    return pl.pallas_call(
        functools.partial(_post_body, d=d, ff_chunk=ff_chunk),
        grid=(t_len // tm,),
        in_specs=[pl.BlockSpec((tm, a.shape[1]), row) for a in acts] + [resident(w) for w in params],
        out_specs=pl.BlockSpec((tm, d), row),
        out_shape=jax.ShapeDtypeStruct((t_len, d), F32),
        compiler_params=pltpu.CompilerParams(dimension_semantics=("arbitrary",),
                                             vmem_limit_bytes=VMEM_LIMIT),
        name="post",
    )(*acts, *params)


def _selection_overlap_t(ncp, s_len):
    ncmp = (s_len - CMP_LEN) // CMP_STRIDE + 1
    nsel = s_len // SEL_LEN
    ratio = SEL_LEN // CMP_STRIDE
    span = CMP_LEN // CMP_STRIDE
    i = np.arange(ncmp)[:, None]
    j = np.arange(nsel)[None, :]
    ov = np.maximum(np.minimum(i + span, ratio * (j + 1)) - np.maximum(i, ratio * j), 0)
    out = np.zeros((LANES, ncp), np.float32)
    out[:nsel, :ncmp] = ov.T
    return out


def _layer(h, p_i, norm_mix_g, w_in, pe_ck, w_ck1, w_ck2, pe_cv, w_cv1, w_cv2, w_up_a, w_up_b, w_out,
           norm_mlp_g, w_mlp1, w_mlp2, norm_ple_g, w_ple_gate, w_ple, final_g):
    bsz, s_len, d = h.shape
    t_len = bsz * s_len
    aw = A_HEADS * HEAD_DIM
    bw = B_HEADS * HEAD_DIM
    kvw = B_KV_GROUPS * HEAD_DIM
    assert s_len % A_TILE == 0 and s_len // SEL_LEN <= LANES and kvw == LANES

    o_qb = 3 * aw
    o_kv = o_qb + bw
    o_ng = o_kv + 6 * kvw
    o_ga = o_ng + 3 * B_HEADS
    kv = lambda i: w_in[:, o_kv + i * kvw:o_kv + (i + 1) * kvw]
    zeros_h = jnp.zeros((d, HEAD_DIM), w_in.dtype)
    grp = lambda w, g: jnp.concatenate([w[:, g * HEAD_DIM:(g + 1) * HEAD_DIM], zeros_h], axis=1)
    wa = w_in[:, :3 * aw]
    wc = jnp.concatenate([kv(0), kv(1)], axis=1)
    wq = w_in[:, o_qb:o_qb + bw]
    wkv = jnp.concatenate([grp(kv(2), 0), grp(kv(2), 1), kv(3), grp(kv(4), 0), grp(kv(4), 1), kv(5)], axis=1)
    wng = jnp.concatenate([w_in[:, o_ng:o_ga], jnp.zeros((d, LANES - 3 * B_HEADS), w_in.dtype)], axis=1)
    wgab = w_in[:, o_ga:]
    x2 = h.reshape(t_len, d)
    pa, kvc, qb, kvb, ng, gab = _proj(x2, norm_mix_g.reshape(1, d), *(w.astype(BF16) for w in
                                      (wa, wc, wq, wkv, wng, wgab)), s_len=s_len)

    ncp = s_len // CMP_STRIDE
    half = CMP_STRIDE * HEAD_DIM
    chunks = kvc.reshape(bsz, ncp, CMP_STRIDE, 2 * B_KV_GROUPS, HEAD_DIM).transpose(0, 3, 1, 2, 4)
    chunks = chunks.reshape(bsz, 2 * B_KV_GROUPS, ncp, half)
    pe2 = jnp.stack([pe_ck.reshape(2, half), pe_cv.reshape(2, half)])
    w1s = jnp.stack([w_ck1.reshape(2, half, CMP_HIDDEN), w_cv1.reshape(2, half, CMP_HIDDEN)]).astype(BF16)
    zpad = jnp.zeros((CMP_HIDDEN, HEAD_DIM), w_ck2.dtype)
    w2s = jnp.stack([jnp.concatenate([w_ck2, zpad], axis=1), jnp.concatenate([w_ck2, zpad], axis=1),
                     jnp.concatenate([w_cv2, zpad], axis=1), jnp.concatenate([zpad, w_cv2], axis=1)]).astype(BF16)
    cmp = _compress(chunks, pe2, w1s, w2s)

    oa = _dilated(pa, bsz=bsz, s_len=s_len)
    onehot = (np.arange(s_len)[:, None] // SEL_LEN == np.arange(LANES)[None, :])
    ob = _nsa(qb, kvb, cmp, ng, jnp.asarray(onehot, BF16), jnp.asarray(_selection_overlap_t(ncp, s_len), BF16),
              bsz=bsz, s_len=s_len)

    b16 = lambda w: w.astype(BF16)
    row = lambda v: v.reshape(1, d)
    return _post(x2, oa, ob, gab, p_i.reshape(t_len, -1), b16(w_up_a), b16(w_up_b), b16(w_out), row(norm_mlp_g),
                 b16(w_mlp1), b16(w_mlp2), row(norm_ple_g), b16(w_ple_gate), b16(w_ple), row(final_g)
                 ).reshape(bsz, s_len, d)


def kernel(x, p, norm_mix_g, w_in, pe_ck, w_ck1, w_ck2, pe_cv, w_cv1, w_cv2, w_up_a, w_up_b, w_out,
           norm_mlp_g, w_mlp1, w_mlp2, norm_ple_g, w_ple_gate, w_ple, norm_final_g):
    depth = w_in.shape[0]
    assert depth == 1, "the fused tail applies the final norm inside the single layer"
    return _layer(x, p[0], norm_mix_g[0], w_in[0], pe_ck[0], w_ck1[0], w_ck2[0], pe_cv[0], w_cv1[0], w_cv2[0],
                  w_up_a[0], w_up_b[0], w_out[0], norm_mlp_g[0], w_mlp1[0], w_mlp2[0], norm_ple_g[0],
                  w_ple_gate[0], w_ple[0], norm_final_g)
```

```python
import functools

import numpy as np
import jax
import jax.numpy as jnp
from jax import lax
from jax.experimental import pallas as pl
from jax.experimental.pallas import tpu as pltpu

HEAD_DIM = 64
A_HEADS = 8
A_CONFIGS = ((128, 1), (512, 4), (2048, 16))
B_HEADS = 8
B_KV_GROUPS = 2
B_REP = B_HEADS // B_KV_GROUPS
CMP_LEN = 32
CMP_STRIDE = 16
CMP_HIDDEN = 256
SEL_LEN = 64
SEL_TOPK = 16
WIN = 512
Q_BLK = 128
EPS = 1e-6
NEG = -1e30
FORCE = 1e9
MASK_BIG = 2.0 ** 100
SCALE = HEAD_DIM ** -0.5

LANES = 128
A_TILE = 2048
A_GROUP = 4
SLC_TILE = 512
VMEM_LIMIT = 56 * 1024 * 1024

LANE_POS_BLK = 64
LANE_POS_IN = 65
LANE_CMP_HI = 66
LANE_CMP_LO = 67

F32 = jnp.float32
BF16 = jnp.bfloat16


def _dot(a, b):
    return jnp.dot(a, b, preferred_element_type=F32)


def _dot_nt(a, b):
    return lax.dot_general(a, b, (((1,), (1,)), ((), ())), preferred_element_type=F32)


def _rms(x, g):
    inv = lax.rsqrt(jnp.mean(x * x, axis=-1, keepdims=True) + EPS)
    return (x * inv) * g


def _iota(shape, dim, dtype=jnp.int32):
    return lax.broadcasted_iota(dtype, shape, dim)


def _proj_body(x_ref, g_ref, wa_ref, wc_ref, wq_ref, wkv_ref, wng_ref, wgab_ref,
               a0_ref, a1_ref, a2_ref, kvc_ref, qb_ref, kvb_ref, ng_ref, gab_ref, res_sc, *, tm, s_len):
    n = _rms(x_ref[...], g_ref[...]).astype(BF16)
    res = _dot(n, wa_ref[...])
    a_refs = (a0_ref, a1_ref, a2_ref)
    for s in range(res.shape[1] // LANES):
        cols = slice(s * LANES, (s + 1) * LANES)
        res_sc[s] = res[:, cols]
        for (window, dil), a_ref in zip(A_CONFIGS, a_refs):
            for r in range(dil):
                src = pl.ds(r, tm // dil, stride=dil) if dil > 1 else pl.ds(0, tm)
                a_ref[r, :, cols] = res_sc[s, src, :].astype(BF16)
    kvc_ref[...] = _dot(n, wc_ref[...])
    qb_ref[...] = _dot(n, wq_ref[...]).astype(BF16)
    ng_ref[...] = _dot(n, wng_ref[...])
    gab_ref[...] = _dot(n, wgab_ref[...]).astype(BF16)
    pos = (pl.program_id(0) * tm) % s_len + _iota((tm, LANES), 0)
    lane = _iota((tm, LANES), 1)
    posc = jnp.where(lane == LANE_POS_BLK, (pos // SEL_LEN).astype(F32),
                     jnp.where(lane == LANE_POS_IN, (pos % SEL_LEN).astype(F32), 0.0))
    kv = _dot(n, wkv_ref[...])
    for c in range(6):
        blk = kv[:, c * LANES:(c + 1) * LANES]
        if c in (0, 1, 3, 4):
            blk = blk + posc
        kvb_ref[:, c * LANES:(c + 1) * LANES] = blk.astype(BF16)


def _proj(x2, g, wa, wc, wq, wkv, wng, wgab, *, s_len, tm=256):
    t_len, d = x2.shape
    bsz = t_len // s_len
    nrt = s_len // tm
    const = lambda i: (0, 0)
    row = lambda i: (i, 0)
    ws = (wa, wc, wq, wkv, wng, wgab)
    flat = (wc, wq, wkv, wng, wgab)
    flat_dtypes = (F32, BF16, BF16, F32, BF16)
    aw = wa.shape[1]
    a_specs = [pl.BlockSpec((None, dil, tm // dil, aw), lambda i: (i // nrt, 0, i % nrt, 0))
               for _, dil in A_CONFIGS]
    a_shapes = [jax.ShapeDtypeStruct((bsz, dil, s_len // dil, aw), BF16) for _, dil in A_CONFIGS]
    return pl.pallas_call(
        functools.partial(_proj_body, tm=tm, s_len=s_len),
        grid=(t_len // tm,),
        in_specs=[pl.BlockSpec((tm, d), row), pl.BlockSpec((1, d), const)]
                 + [pl.BlockSpec(w.shape, const) for w in ws],
        out_specs=a_specs + [pl.BlockSpec((tm, w.shape[1]), row) for w in flat],
        out_shape=a_shapes + [jax.ShapeDtypeStruct((t_len, w.shape[1]), dt) for w, dt in zip(flat, flat_dtypes)],
        scratch_shapes=[pltpu.VMEM((aw // LANES, tm, LANES), F32)],
        compiler_params=pltpu.CompilerParams(dimension_semantics=("arbitrary",),
                                             vmem_limit_bytes=VMEM_LIMIT),
        name="proj",
    )(x2, g, *ws)


def _gelu_tanh(x):
    return 0.5 * x * (1.0 + jnp.tanh(np.sqrt(2.0 / np.pi).astype(np.float32) * (x + 0.044715 * (x * x * x))))


def _compress_body(ch_ref, pe_ref, w1_ref, w2_ref, out_ref, *, ncp):
    ch = ch_ref[...]
    xa = (ch + pe_ref[0:1, :]).astype(BF16)
    xb = (ch + pe_ref[1:2, :]).astype(BF16)
    a = _dot(xa, w1_ref[0])
    b = _dot(xb, w1_ref[1])
    pre = a + jnp.concatenate([b[1:], b[:1]], axis=0)
    hid = _gelu_tanh(pre).astype(BF16)
    out = _dot(hid, w2_ref[...])
    kv_is_key = pl.program_id(1) < B_KV_GROUPS
    n_idx = _iota((ncp, LANES), 0)
    lane = _iota((ncp, LANES), 1)
    nc = jnp.where(lane == LANE_CMP_HI, (n_idx // 16).astype(F32),
                   jnp.where(lane == LANE_CMP_LO, (n_idx % 16).astype(F32), 0.0))
    out = out + jnp.where(kv_is_key, nc, 0.0)
    out_ref[...] = out.astype(BF16)


def _compress(chunks, pe2, w1s, w2s):
    bsz, nslot, ncp, width = chunks.shape
    return pl.pallas_call(
        functools.partial(_compress_body, ncp=ncp),
        grid=(bsz, nslot),
        in_specs=[pl.BlockSpec((None, None, ncp, width), lambda b, s: (b, s, 0, 0)),
                  pl.BlockSpec((None, 2, width), lambda b, s: (s // B_KV_GROUPS, 0, 0)),
                  pl.BlockSpec((None, 2, width, CMP_HIDDEN), lambda b, s: (s // B_KV_GROUPS, 0, 0, 0)),
                  pl.BlockSpec((None, CMP_HIDDEN, LANES), lambda b, s: (s, 0, 0))],
        out_specs=pl.BlockSpec((None, None, ncp, LANES), lambda b, s: (b, s, 0, 0)),
        out_shape=jax.ShapeDtypeStruct((bsz, nslot, ncp, LANES), BF16),
        compiler_params=pltpu.CompilerParams(dimension_semantics=("arbitrary", "arbitrary"),
                                             vmem_limit_bytes=VMEM_LIMIT),
        name="compress",
    )(chunks, pe2, w1s, w2s)


def _head_slope(h, n_heads):
    out = jnp.float32(2.0 ** (-8.0 * n_heads / n_heads))
    for k in range(n_heads - 1):
        out = jnp.where(h == k, jnp.float32(2.0 ** (-8.0 * (k + 1) / n_heads)), out)
    return out


def _dilated_body(*refs):
    ncfg = len(A_CONFIGS)
    in_refs = refs[:5 * ncfg]
    out_ref = refs[5 * ncfg]
    o_sc, l_sc, bias_sc = refs[5 * ncfg + 1:5 * ncfg + 4]
    kbufs = refs[5 * ncfg + 4:5 * ncfg + 4 + ncfg]
    vbufs = refs[5 * ncfg + 4 + ncfg:]
    hp = pl.program_id(1)
    first_tile = pl.program_id(2) == 0
    low = _iota((1, LANES), 1) < HEAD_DIM

    @pl.when(first_tile)
    def _():
        qi = _iota((Q_BLK, 2 * Q_BLK), 0)
        kj = _iota((Q_BLK, 2 * Q_BLK), 1)
        dist = qi - kj + Q_BLK
        for c, (window, dil) in enumerate(A_CONFIGS):
            valid = (dist >= 0) & (dist <= window // dil)
            for hh in range(2):
                slope = _head_slope(2 * hp + hh, A_HEADS)
                bias = jnp.where(valid, -(slope * dil) * dist.astype(F32), NEG)
                bias_sc[(c * 2 + hh) * 2] = bias
                bias_sc[(c * 2 + hh) * 2 + 1] = jnp.where(kj < Q_BLK, NEG, bias)

    for c, (window, dil) in enumerate(A_CONFIGS):
        q_ref, k_ref, v_ref, kh_ref, vh_ref = in_refs[5 * c:5 * c + 5]
        kbuf, vbuf = kbufs[c], vbufs[c]
        rows = A_TILE // dil
        nsub = rows // Q_BLK
        kbuf[:, :Q_BLK, :] = kh_ref[...]
        kbuf[:, Q_BLK:, :] = k_ref[...]
        vbuf[:, :Q_BLK, :] = vh_ref[...]
        vbuf[:, Q_BLK:, :] = v_ref[...]

        def group(gidx, carry, c=c, dil=dil, nsub=nsub, q_ref=q_ref, kbuf=kbuf, vbuf=vbuf):
            subs = []
            for u in range(A_GROUP):
                idx = gidx * A_GROUP + u
                r = idx // nsub
                j = idx % nsub
                j0 = pl.multiple_of(j * Q_BLK, Q_BLK)
                seq_start = ((j == 0) & first_tile).astype(jnp.int32)
                subs.append((r, j, j0, seq_start))
            scores = []
            for r, j, j0, seq_start in subs:
                q = q_ref[r, pl.ds(j0, Q_BLK), :]
                k2 = kbuf[r, pl.ds(j0, 2 * Q_BLK), :]
                for hh in range(2):
                    qm = jnp.where(low == (hh == 0), q, jnp.zeros_like(q))
                    scores.append(_dot_nt(qm, k2) + bias_sc[(c * 2 + hh) * 2 + seq_start])
            probs = []
            for s in scores:
                m = jnp.max(s, axis=-1, keepdims=True)
                e = jnp.exp(s - m)
                l = jnp.sum(e, axis=-1, keepdims=True)
                probs.append((e.astype(BF16), l, m + jnp.log(l)))
            for u, (r, j, j0, seq_start) in enumerate(subs):
                v2 = vbuf[r, pl.ds(j0, 2 * Q_BLK), :]
                outs = [_dot(probs[2 * u + hh][0], v2) / probs[2 * u + hh][1] for hh in range(2)]
                lses = [jnp.broadcast_to(probs[2 * u + hh][2], (Q_BLK, LANES)) for hh in range(2)]
                row0 = j * (Q_BLK * dil) + r
                dst = pl.ds(row0, Q_BLK, stride=dil) if dil > 1 else pl.ds(row0, Q_BLK)
                o_sc[c, dst, :] = jnp.where(low, outs[0], outs[1])
                l_sc[c, dst, :] = jnp.where(low, lses[0], lses[1])
            return carry

        lax.fori_loop(0, dil * nsub // A_GROUP, group, 0)

    l0, l1, l2 = l_sc[0], l_sc[1], l_sc[2]
    m = jnp.maximum(jnp.maximum(l0, l1), l2)
    e0, e1, e2 = jnp.exp(l0 - m), jnp.exp(l1 - m), jnp.exp(l2 - m)
    mix = (e0 * o_sc[0] + e1 * o_sc[1] + e2 * o_sc[2]) / (e0 + e1 + e2)
    out_ref[...] = mix.astype(BF16)


def _dilated(qkv_by_cfg, *, bsz, s_len):
    nt = s_len // A_TILE
    npair = A_HEADS // 2
    ncfg = len(A_CONFIGS)
    in_specs, operands, kv_scratch = [], [], []
    for (window, dil), arr in zip(A_CONFIGS, qkv_by_cfg):
        rows = A_TILE // dil
        nsub = rows // Q_BLK
        cur = lambda off: (lambda b, h, t: (b, 0, t, off + h))
        halo = lambda off, nsub=nsub: (lambda b, h, t: (b, 0, jnp.maximum(t * nsub - 1, 0), off + h))
        in_specs += [pl.BlockSpec((None, dil, rows, LANES), cur(0)),
                     pl.BlockSpec((None, dil, rows, LANES), cur(npair)),
                     pl.BlockSpec((None, dil, rows, LANES), cur(2 * npair)),
                     pl.BlockSpec((None, dil, Q_BLK, LANES), halo(npair)),
                     pl.BlockSpec((None, dil, Q_BLK, LANES), halo(2 * npair))]
        operands += [arr] * 5
        kv_scratch.append(pltpu.VMEM((dil, Q_BLK + rows, LANES), BF16))
    return pl.pallas_call(
        _dilated_body,
        grid=(bsz, npair, nt),
        in_specs=in_specs,
        out_specs=pl.BlockSpec((A_TILE, LANES), lambda b, h, t: (b * nt + t, h)),
        out_shape=jax.ShapeDtypeStruct((bsz * s_len, A_HEADS * HEAD_DIM), BF16),
        scratch_shapes=[pltpu.VMEM((ncfg, A_TILE, LANES), F32),
                        pltpu.VMEM((ncfg, A_TILE, LANES), F32),
                        pltpu.VMEM((ncfg * 4, Q_BLK, 2 * Q_BLK), F32)] + kv_scratch + kv_scratch,
        compiler_params=pltpu.CompilerParams(dimension_semantics=("arbitrary",) * 3,
                                             vmem_limit_bytes=VMEM_LIMIT),
        name="dilated",
    )(*operands)


def _nsa_body(q_ref, ks0_ref, ks1_ref, vs_ref, kw0_ref, kw1_ref, vw_ref, cmp_ref, ng_ref,
              oh_ref, ovt_ref, out_ref, *, s_len, ncp, topk):
    q0 = pl.program_id(1) * Q_BLK
    lane = _iota((1, LANES), 1)
    low = lane < HEAD_DIM
    t_col = q0 + _iota((Q_BLK, 1), 0)
    t_row = q0 + _iota((1, Q_BLK), 1)
    nsel_pad = LANES
    win_keys = WIN + Q_BLK

    cmp_end = CMP_STRIDE * _iota((1, ncp), 1) + (CMP_LEN - 1)
    mask_c = cmp_end <= t_col
    bias_c = jnp.where(mask_c, 0.0, NEG)
    kstart = pl.multiple_of(jnp.maximum(q0 - WIN, 0), Q_BLK)
    d_w = t_col - (kstart + _iota((1, win_keys), 1))
    bias_w = jnp.where((d_w >= 0) & (d_w < WIN), 0.0, NEG)
    blk_t = _iota((nsel_pad, 1), 0)
    allowed_t = blk_t * SEL_LEN <= t_row
    cur_t = t_row // SEL_LEN
    forced_t = (blk_t == 0) | (blk_t == cur_t) | (blk_t == cur_t - 1)
    blk_f = blk_t.astype(F32)

    sig = jax.nn.sigmoid(ng_ref[...])
    q_all = q_ref[...].astype(F32)
    ks_refs = (ks0_ref, ks1_ref)
    kw_refs = (kw0_ref, kw1_ref)
    n_full = q0 // SLC_TILE
    head_out = [None] * B_HEADS

    for g in range(B_KV_GROUPS):
        rows = []
        for r in range(B_REP):
            h = g * B_REP + r
            slope = 2.0 ** (-8.0 * (h + 1) / B_HEADS)
            blk = q_all[:, (h // 2) * LANES:(h // 2 + 1) * LANES]
            if h % 2 == 1:
                blk = pltpu.roll(blk, HEAD_DIM, axis=1)
            cols = jnp.where(lane == LANE_POS_BLK, SEL_LEN * slope,
                             jnp.where(lane == LANE_POS_IN, slope,
                                       jnp.where(lane == LANE_CMP_HI, 16 * CMP_STRIDE * slope,
                                                 jnp.where(lane == LANE_CMP_LO, CMP_STRIDE * slope, 0.0))))
            rows.append(jnp.where(low, blk * SCALE, cols))
        qg = jnp.concatenate(rows, axis=0).astype(BF16)

        s = _dot_nt(qg, cmp_ref[g]).reshape(B_REP, Q_BLK, ncp) + bias_c[None]
        m = jnp.max(s, axis=-1, keepdims=True)
        e = jnp.where(mask_c[None], jnp.exp(s - m), 0.0)
        den = jnp.sum(e, axis=-1, keepdims=True)
        p = e / jnp.maximum(den, 1e-30)
        o_cmp = _dot(p.reshape(B_REP * Q_BLK, ncp).astype(BF16), cmp_ref[B_KV_GROUPS + g])
        psum = p[0] + p[1] + p[2] + p[3]
        p_hi = psum.astype(BF16)
        p_lo = (psum - p_hi.astype(F32)).astype(BF16)
        imp_t = _dot_nt(ovt_ref[...], p_hi) + _dot_nt(ovt_ref[...], p_lo)

        rank = jnp.where(allowed_t, imp_t + jnp.where(forced_t, FORCE, 0.0), NEG)

        def pick_one(_, carry):
            rank, sel = carry
            best = jnp.max(rank, axis=0, keepdims=True)
            cand = jnp.where(rank == best, blk_f, float(nsel_pad))
            idx = jnp.min(cand, axis=0, keepdims=True)
            pick = blk_f == idx
            return jnp.where(pick, -3e38, rank), jnp.where(pick, 1.0, sel)

        _, sel_t = lax.fori_loop(0, topk, pick_one, (rank, jnp.zeros((nsel_pad, Q_BLK), F32)))
        sel_bias = jnp.where(sel_t.T > 0.5, 0.0, -MASK_BIG).astype(BF16)
        q_slc = jnp.concatenate([qg, jnp.concatenate([sel_bias] * B_REP, axis=0)], axis=1)

        ks_ref = ks_refs[g]

        def slc_tile(kt, carry, diag, ks_ref=ks_ref, q_slc=q_slc):
            m_i, l_i, acc = carry
            k0 = pl.multiple_of(kt * SLC_TILE, SLC_TILE)
            k_aug = jnp.concatenate([ks_ref[pl.ds(k0, SLC_TILE), :], oh_ref[pl.ds(k0, SLC_TILE), :]], axis=1)
            s = _dot_nt(q_slc, k_aug)
            if diag:
                causal = (k0 + _iota((1, SLC_TILE), 1)) <= t_col
                s = (s.reshape(B_REP, Q_BLK, SLC_TILE) + jnp.where(causal, 0.0, NEG)[None]).reshape(
                    B_REP * Q_BLK, SLC_TILE)
            m_new = jnp.maximum(m_i, jnp.max(s, axis=-1, keepdims=True))
            alpha = jnp.exp(m_i - m_new)
            pe = jnp.exp(s - m_new)
            l_new = alpha * l_i + jnp.sum(pe, axis=-1, keepdims=True)
            acc_new = alpha * acc + _dot(pe.astype(BF16), vs_ref[pl.ds(k0, SLC_TILE), :])
            return m_new, l_new, acc_new

        init = (jnp.full((B_REP * Q_BLK, 1), -3e38, F32), jnp.zeros((B_REP * Q_BLK, 1), F32),
                jnp.zeros((B_REP * Q_BLK, LANES), F32))
        carry = lax.fori_loop(0, n_full, functools.partial(slc_tile, diag=False), init)
        _, l_s, acc_s = slc_tile(n_full, carry, True)
        o_slc = acc_s / l_s

        s = _dot_nt(qg, kw_refs[g][pl.ds(kstart, win_keys), :]).reshape(B_REP, Q_BLK, win_keys) + bias_w[None]
        m = jnp.max(s, axis=-1, keepdims=True)
        e = jnp.exp(s - m)
        den = jnp.sum(e, axis=-1, keepdims=True)
        o_win = _dot(e.reshape(B_REP * Q_BLK, win_keys).astype(BF16), vw_ref[pl.ds(kstart, win_keys), :])
        o_win = o_win / den.reshape(B_REP * Q_BLK, 1)

        for r in range(B_REP):
            h = g * B_REP + r
            rs = slice(r * Q_BLK, (r + 1) * Q_BLK)
            o = (sig[:, 3 * h:3 * h + 1] * o_cmp[rs] + sig[:, 3 * h + 1:3 * h + 2] * o_slc[rs]
                 + sig[:, 3 * h + 2:3 * h + 3] * o_win[rs])
            if h % 2 != g:
                o = pltpu.roll(o, HEAD_DIM, axis=1)
            head_out[h] = o

    for hp in range(B_HEADS // 2):
        out_ref[:, hp * LANES:(hp + 1) * LANES] = jnp.where(low, head_out[2 * hp], head_out[2 * hp + 1]).astype(BF16)


def _nsa(qb, kvb, cmp, ng, onehot, ovt, *, bsz, s_len):
    nq = s_len // Q_BLK
    ncp = cmp.shape[2]
    topk = min(SEL_TOPK, s_len // SEL_LEN)
    rowblk = lambda b, i: (b * nq + i, 0)
    res = lambda col: (lambda b, i: (b, col))
    return pl.pallas_call(
        functools.partial(_nsa_body, s_len=s_len, ncp=ncp, topk=topk),
        grid=(bsz, nq),
        in_specs=[pl.BlockSpec((Q_BLK, B_HEADS * HEAD_DIM), rowblk)]
                 + [pl.BlockSpec((s_len, LANES), res(col)) for col in range(6)]
                 + [pl.BlockSpec((None, 2 * B_KV_GROUPS, ncp, LANES), lambda b, i: (b, 0, 0, 0)),
                    pl.BlockSpec((Q_BLK, LANES), rowblk),
                    pl.BlockSpec(onehot.shape, lambda b, i: (0, 0)),
                    pl.BlockSpec(ovt.shape, lambda b, i: (0, 0))],
        out_specs=pl.BlockSpec((Q_BLK, B_HEADS * HEAD_DIM), rowblk),
        out_shape=jax.ShapeDtypeStruct((bsz * s_len, B_HEADS * HEAD_DIM), BF16),
        compiler_params=pltpu.CompilerParams(dimension_semantics=("arbitrary", "arbitrary"),
                                             vmem_limit_bytes=VMEM_LIMIT),
        name="nsa",
    )(qb, kvb, kvb, kvb, kvb, kvb, kvb, cmp, ng, onehot, ovt)


def _post_body(x_ref, oa_ref, ob_ref, gab_ref, p_ref, wua_ref, wub_ref, wout_ref, g2_ref, w1_ref, w2_ref,
               g3_ref, wpg_ref, wple_ref, gf_ref, out_ref, *, d, ff_chunk):
    ya = _dot(oa_ref[...], wua_ref[...])
    yb = _dot(ob_ref[...], wub_ref[...])
    mixed = (jax.nn.sigmoid(gab_ref[:, :d].astype(F32)) * ya
             + jax.nn.sigmoid(gab_ref[:, d:].astype(F32)) * yb)
    h = x_ref[...] + _dot(mixed.astype(BF16), wout_ref[...])
    n2 = _rms(h, g2_ref[...]).astype(BF16)
    acc = h
    for c in range(w1_ref.shape[1] // ff_chunk):
        cs = slice(c * ff_chunk, (c + 1) * ff_chunk)
        hid = jnp.square(jnp.maximum(_dot(n2, w1_ref[:, cs]), 0.0))
        acc = acc + _dot(hid.astype(BF16), w2_ref[cs, :])
    n3 = _rms(acc, g3_ref[...]).astype(BF16)
    gate = jax.nn.sigmoid(_dot(n3, wpg_ref[...]))
    h3 = acc + gate * _dot(p_ref[...].astype(BF16), wple_ref[...])
    out_ref[...] = _rms(h3, gf_ref[...])


def _post(x2, oa, ob, gab, p2, wua, wub, wout, g2, w1, w2, g3, wpg, wple, gf, *, tm=256, ff_chunk=1024):
    t_len, d = x2.shape
    row = lambda i: (i, 0)
    const = lambda i: (0, 0)
    resident = lambda a: pl.BlockSpec(a.shape, const, pipeline_mode=pl.Buffered(1))
    acts = (x2, oa, ob, gab, p2)
    params = (wua, wub, wout, g2, w1, w2, g3, wpg, wple, gf)
    return pl.pallas_call(
        functools.partial(_post_body, d=d, ff_chunk=ff_chunk),
        grid=(t_len // tm,),
        in_specs=[pl.BlockSpec((tm, a.shape[1]), row) for a in acts] + [resident(w) for w in params],
        out_specs=pl.BlockSpec((tm, d), row),
        out_shape=jax.ShapeDtypeStruct((t_len, d), F32),
        compiler_params=pltpu.CompilerParams(dimension_semantics=("arbitrary",),
                                             vmem_limit_bytes=VMEM_LIMIT),
        name="post",
    )(*acts, *params)


def _selection_overlap_t(ncp, s_len):
    ncmp = (s_len - CMP_LEN) // CMP_STRIDE + 1
    nsel = s_len // SEL_LEN
    ratio = SEL_LEN // CMP_STRIDE
    span = CMP_LEN // CMP_STRIDE
    i = np.arange(ncmp)[:, None]
    j = np.arange(nsel)[None, :]
    ov = np.maximum(np.minimum(i + span, ratio * (j + 1)) - np.maximum(i, ratio * j), 0)
    out = np.zeros((LANES, ncp), np.float32)
    out[:nsel, :ncmp] = ov.T
    return out


def _layer(h, p_i, norm_mix_g, w_in, pe_ck, w_ck1, w_ck2, pe_cv, w_cv1, w_cv2, w_up_a, w_up_b, w_out,
           norm_mlp_g, w_mlp1, w_mlp2, norm_ple_g, w_ple_gate, w_ple, final_g):
    bsz, s_len, d = h.shape
    t_len = bsz * s_len
    aw = A_HEADS * HEAD_DIM
    bw = B_HEADS * HEAD_DIM
    kvw = B_KV_GROUPS * HEAD_DIM
    assert s_len % A_TILE == 0 and s_len // SEL_LEN <= LANES and kvw == LANES

    o_qb = 3 * aw
    o_kv = o_qb + bw
    o_ng = o_kv + 6 * kvw
    o_ga = o_ng + 3 * B_HEADS
    kv = lambda i: w_in[:, o_kv + i * kvw:o_kv + (i + 1) * kvw]
    zeros_h = jnp.zeros((d, HEAD_DIM), w_in.dtype)
    grp = lambda w, g: jnp.concatenate([w[:, g * HEAD_DIM:(g + 1) * HEAD_DIM], zeros_h], axis=1)
    wa = jnp.concatenate([w_in[:, :aw] * SCALE, w_in[:, aw:3 * aw]], axis=1)
    wc = jnp.concatenate([kv(0), kv(1)], axis=1)
    wq = w_in[:, o_qb:o_qb + bw]
    wkv = jnp.concatenate([grp(kv(2), 0), grp(kv(2), 1), kv(3), grp(kv(4), 0), grp(kv(4), 1), kv(5)], axis=1)
    wng = jnp.concatenate([w_in[:, o_ng:o_ga], jnp.zeros((d, LANES - 3 * B_HEADS), w_in.dtype)], axis=1)
    wgab = w_in[:, o_ga:]
    x2 = h.reshape(t_len, d)
    a0, a1, a2, kvc, qb, kvb, ng, gab = _proj(x2, norm_mix_g.reshape(1, d), *(w.astype(BF16) for w in
                                              (wa, wc, wq, wkv, wng, wgab)), s_len=s_len)

    ncp = s_len // CMP_STRIDE
    half = CMP_STRIDE * HEAD_DIM
    chunks = kvc.reshape(bsz, ncp, CMP_STRIDE, 2 * B_KV_GROUPS, HEAD_DIM).transpose(0, 3, 1, 2, 4)
    chunks = chunks.reshape(bsz, 2 * B_KV_GROUPS, ncp, half)
    pe2 = jnp.stack([pe_ck.reshape(2, half), pe_cv.reshape(2, half)])
    w1s = jnp.stack([w_ck1.reshape(2, half, CMP_HIDDEN), w_cv1.reshape(2, half, CMP_HIDDEN)]).astype(BF16)
    zpad = jnp.zeros((CMP_HIDDEN, HEAD_DIM), w_ck2.dtype)
    w2s = jnp.stack([jnp.concatenate([w_ck2, zpad], axis=1), jnp.concatenate([w_ck2, zpad], axis=1),
                     jnp.concatenate([w_cv2, zpad], axis=1), jnp.concatenate([zpad, w_cv2], axis=1)]).astype(BF16)
    cmp = _compress(chunks, pe2, w1s, w2s)

    oa = _dilated((a0, a1, a2), bsz=bsz, s_len=s_len)
    onehot = (np.arange(s_len)[:, None] // SEL_LEN == np.arange(LANES)[None, :])
    ob = _nsa(qb, kvb, cmp, ng, jnp.asarray(onehot, BF16), jnp.asarray(_selection_overlap_t(ncp, s_len), BF16),
              bsz=bsz, s_len=s_len)

    b16 = lambda w: w.astype(BF16)
    row = lambda v: v.reshape(1, d)
    return _post(x2, oa, ob, gab, p_i.reshape(t_len, -1), b16(w_up_a), b16(w_up_b), b16(w_out), row(norm_mlp_g),
                 b16(w_mlp1), b16(w_mlp2), row(norm_ple_g), b16(w_ple_gate), b16(w_ple), row(final_g)
                 ).reshape(bsz, s_len, d)


def kernel(x, p, norm_mix_g, w_in, pe_ck, w_ck1, w_ck2, pe_cv, w_cv1, w_cv2, w_up_a, w_up_b, w_out,
           norm_mlp_g, w_mlp1, w_mlp2, norm_ple_g, w_ple_gate, w_ple, norm_final_g):
    depth = w_in.shape[0]
    assert depth == 1, "the fused tail applies the final norm inside the single layer"
    return _layer(x, p[0], norm_mix_g[0], w_in[0], pe_ck[0], w_ck1[0], w_ck2[0], pe_cv[0], w_cv1[0], w_cv2[0],
                  w_up_a[0], w_up_b[0], w_out[0], norm_mlp_g[0], w_mlp1[0], w_mlp2[0], norm_ple_g[0],
                  w_ple_gate[0], w_ple[0], norm_final_g)
```

```python
import functools

import numpy as np
import jax
import jax.numpy as jnp
from jax import lax
from jax.experimental import pallas as pl
from jax.experimental.pallas import tpu as pltpu

HEAD_DIM = 64
A_HEADS = 8
A_CONFIGS = ((128, 1), (512, 4), (2048, 16))
B_HEADS = 8
B_KV_GROUPS = 2
B_REP = B_HEADS // B_KV_GROUPS
CMP_LEN = 32
CMP_STRIDE = 16
CMP_HIDDEN = 256
SEL_LEN = 64
SEL_TOPK = 16
WIN = 512
Q_BLK = 128
EPS = 1e-6
NEG = -1e30
FORCE = 1e9
MASK_BIG = 2.0 ** 100
SCALE = HEAD_DIM ** -0.5

LANES = 128
A_TILE = 2048
A_GROUP = 4
SLC_TILE = 256
FLAG_W = 32
LIST_W = FLAG_W + 2
VMEM_LIMIT = 56 * 1024 * 1024

LANE_POS_BLK = 64
LANE_POS_IN = 65
LANE_CMP_HI = 66
LANE_CMP_LO = 67

F32 = jnp.float32
BF16 = jnp.bfloat16


def _dot(a, b):
    return jnp.dot(a, b, preferred_element_type=F32)


def _dot_nt(a, b):
    return lax.dot_general(a, b, (((1,), (1,)), ((), ())), preferred_element_type=F32)


def _rms(x, g):
    inv = lax.rsqrt(jnp.mean(x * x, axis=-1, keepdims=True) + EPS)
    return (x * inv) * g


def _iota(shape, dim, dtype=jnp.int32):
    return lax.broadcasted_iota(dtype, shape, dim)


def _proj_body(x_ref, g_ref, wa_ref, wc_ref, wq_ref, wkv_ref, wng_ref, wgab_ref, wvt_ref,
               a0_ref, a1_ref, a2_ref, kvc_ref, qb_ref, kvb_ref, ng_ref, gab_ref, vst_ref, res_sc, *, tm, s_len):
    n = _rms(x_ref[...], g_ref[...]).astype(BF16)
    vst_ref[...] = _dot_nt(wvt_ref[...], n).astype(BF16)
    res = _dot(n, wa_ref[...])
    a_refs = (a0_ref, a1_ref, a2_ref)
    for s in range(res.shape[1] // LANES):
        cols = slice(s * LANES, (s + 1) * LANES)
        res_sc[s] = res[:, cols]
        for (window, dil), a_ref in zip(A_CONFIGS, a_refs):
            for r in range(dil):
                src = pl.ds(r, tm // dil, stride=dil) if dil > 1 else pl.ds(0, tm)
                a_ref[r, :, cols] = res_sc[s, src, :].astype(BF16)
    kvc_ref[...] = _dot(n, wc_ref[...])
    qb_ref[...] = _dot(n, wq_ref[...]).astype(BF16)
    ng_ref[...] = _dot(n, wng_ref[...])
    gab_ref[...] = _dot(n, wgab_ref[...]).astype(BF16)
    pos = (pl.program_id(0) * tm) % s_len + _iota((tm, LANES), 0)
    lane = _iota((tm, LANES), 1)
    posc = jnp.where(lane == LANE_POS_BLK, (pos // SEL_LEN).astype(F32),
                     jnp.where(lane == LANE_POS_IN, (pos % SEL_LEN).astype(F32), 0.0))
    kv = _dot(n, wkv_ref[...])
    for c in range(kv.shape[1] // LANES):
        blk = kv[:, c * LANES:(c + 1) * LANES]
        if c < 2 * B_KV_GROUPS:
            blk = blk + posc
        kvb_ref[:, c * LANES:(c + 1) * LANES] = blk.astype(BF16)


def _proj(x2, g, wa, wc, wq, wkv, wng, wgab, wvt, *, s_len, tm=SLC_TILE):
    t_len, d = x2.shape
    bsz = t_len // s_len
    nrt = s_len // tm
    const = lambda i: (0, 0)
    row = lambda i: (i, 0)
    ws = (wa, wc, wq, wkv, wng, wgab, wvt)
    flat = (wc, wq, wkv, wng, wgab)
    flat_dtypes = (F32, BF16, BF16, F32, BF16)
    aw = wa.shape[1]
    a_specs = [pl.BlockSpec((None, dil, tm // dil, aw), lambda i: (i // nrt, 0, i % nrt, 0))
               for _, dil in A_CONFIGS]
    a_shapes = [jax.ShapeDtypeStruct((bsz, dil, s_len // dil, aw), BF16) for _, dil in A_CONFIGS]
    return pl.pallas_call(
        functools.partial(_proj_body, tm=tm, s_len=s_len),
        grid=(t_len // tm,),
        in_specs=[pl.BlockSpec((tm, d), row), pl.BlockSpec((1, d), const)]
                 + [pl.BlockSpec(w.shape, const) for w in ws],
        out_specs=a_specs + [pl.BlockSpec((tm, w.shape[1]), row) for w in flat]
                  + [pl.BlockSpec((None, None, wvt.shape[0], tm), lambda i: (i // nrt, i % nrt, 0, 0))],
        out_shape=a_shapes + [jax.ShapeDtypeStruct((t_len, w.shape[1]), dt) for w, dt in zip(flat, flat_dtypes)]
                  + [jax.ShapeDtypeStruct((bsz, nrt, wvt.shape[0], tm), BF16)],
        scratch_shapes=[pltpu.VMEM((aw // LANES, tm, LANES), F32)],
        compiler_params=pltpu.CompilerParams(dimension_semantics=("arbitrary",),
                                             vmem_limit_bytes=VMEM_LIMIT),
        name="proj",
    )(x2, g, *ws)


def _gelu_tanh(x):
    return 0.5 * x * (1.0 + jnp.tanh(np.sqrt(2.0 / np.pi).astype(np.float32) * (x + 0.044715 * (x * x * x))))


def _compress_body(ch_ref, pe_ref, w1_ref, w2_ref, out_ref, *, ncp):
    ch = ch_ref[...]
    xa = (ch + pe_ref[0:1, :]).astype(BF16)
    xb = (ch + pe_ref[1:2, :]).astype(BF16)
    a = _dot(xa, w1_ref[0])
    b = _dot(xb, w1_ref[1])
    pre = a + jnp.concatenate([b[1:], b[:1]], axis=0)
    hid = _gelu_tanh(pre).astype(BF16)
    out = _dot(hid, w2_ref[...])
    kv_is_key = pl.program_id(1) < B_KV_GROUPS
    n_idx = _iota((ncp, LANES), 0)
    lane = _iota((ncp, LANES), 1)
    nc = jnp.where(lane == LANE_CMP_HI, (n_idx // 16).astype(F32),
                   jnp.where(lane == LANE_CMP_LO, (n_idx % 16).astype(F32), 0.0))
    out = out + jnp.where(kv_is_key, nc, 0.0)
    out_ref[...] = out.astype(BF16)


def _compress(chunks, pe2, w1s, w2s):
    bsz, nslot, ncp, width = chunks.shape
    return pl.pallas_call(
        functools.partial(_compress_body, ncp=ncp),
        grid=(bsz, nslot),
        in_specs=[pl.BlockSpec((None, None, ncp, width), lambda b, s: (b, s, 0, 0)),
                  pl.BlockSpec((None, 2, width), lambda b, s: (s // B_KV_GROUPS, 0, 0)),
                  pl.BlockSpec((None, 2, width, CMP_HIDDEN), lambda b, s: (s // B_KV_GROUPS, 0, 0, 0)),
                  pl.BlockSpec((None, CMP_HIDDEN, LANES), lambda b, s: (s, 0, 0))],
        out_specs=pl.BlockSpec((None, None, ncp, LANES), lambda b, s: (b, s, 0, 0)),
        out_shape=jax.ShapeDtypeStruct((bsz, nslot, ncp, LANES), BF16),
        compiler_params=pltpu.CompilerParams(dimension_semantics=("arbitrary", "arbitrary"),
                                             vmem_limit_bytes=VMEM_LIMIT),
        name="compress",
    )(chunks, pe2, w1s, w2s)


def _head_slope(h, n_heads):
    out = jnp.float32(2.0 ** (-8.0 * n_heads / n_heads))
    for k in range(n_heads - 1):
        out = jnp.where(h == k, jnp.float32(2.0 ** (-8.0 * (k + 1) / n_heads)), out)
    return out


def _dilated_body(*refs):
    ncfg = len(A_CONFIGS)
    in_refs = refs[:5 * ncfg]
    out_ref = refs[5 * ncfg]
    o_sc, l_sc, bias_sc = refs[5 * ncfg + 1:5 * ncfg + 4]
    kbufs = refs[5 * ncfg + 4:5 * ncfg + 4 + ncfg]
    vbufs = refs[5 * ncfg + 4 + ncfg:]
    hp = pl.program_id(1)
    first_tile = pl.program_id(2) == 0
    low = _iota((1, LANES), 1) < HEAD_DIM

    @pl.when(first_tile)
    def _():
        qi = _iota((Q_BLK, 2 * Q_BLK), 0)
        kj = _iota((Q_BLK, 2 * Q_BLK), 1)
        dist = qi - kj + Q_BLK
        for c, (window, dil) in enumerate(A_CONFIGS):
            valid = (dist >= 0) & (dist <= window // dil)
            for hh in range(2):
                slope = _head_slope(2 * hp + hh, A_HEADS)
                bias = jnp.where(valid, -(slope * dil) * dist.astype(F32), NEG)
                bias_sc[(c * 2 + hh) * 2] = bias
                bias_sc[(c * 2 + hh) * 2 + 1] = jnp.where(kj < Q_BLK, NEG, bias)

    for c, (window, dil) in enumerate(A_CONFIGS):
        q_ref, k_ref, v_ref, kh_ref, vh_ref = in_refs[5 * c:5 * c + 5]
        kbuf, vbuf = kbufs[c], vbufs[c]
        rows = A_TILE // dil
        nsub = rows // Q_BLK
        kbuf[:, :Q_BLK, :] = kh_ref[...]
        kbuf[:, Q_BLK:, :] = k_ref[...]
        vbuf[:, :Q_BLK, :] = vh_ref[...]
        vbuf[:, Q_BLK:, :] = v_ref[...]

        def group(gidx, carry, c=c, dil=dil, nsub=nsub, q_ref=q_ref, kbuf=kbuf, vbuf=vbuf):
            subs = []
            for u in range(A_GROUP):
                idx = gidx * A_GROUP + u
                r = idx // nsub
                j = idx % nsub
                j0 = pl.multiple_of(j * Q_BLK, Q_BLK)
                seq_start = ((j == 0) & first_tile).astype(jnp.int32)
                subs.append((r, j, j0, seq_start))
            scores = []
            for r, j, j0, seq_start in subs:
                q = q_ref[r, pl.ds(j0, Q_BLK), :]
                k2 = kbuf[r, pl.ds(j0, 2 * Q_BLK), :]
                for hh in range(2):
                    qm = jnp.where(low == (hh == 0), q, jnp.zeros_like(q))
                    scores.append(_dot_nt(qm, k2) + bias_sc[(c * 2 + hh) * 2 + seq_start])
            probs = []
            for s in scores:
                m = jnp.max(s, axis=-1, keepdims=True)
                e = jnp.exp(s - m)
                l = jnp.sum(e, axis=-1, keepdims=True)
                probs.append((e.astype(BF16), l, m + jnp.log(l)))
            for u, (r, j, j0, seq_start) in enumerate(subs):
                v2 = vbuf[r, pl.ds(j0, 2 * Q_BLK), :]
                outs = [_dot(probs[2 * u + hh][0], v2) / probs[2 * u + hh][1] for hh in range(2)]
                lses = [jnp.broadcast_to(probs[2 * u + hh][2], (Q_BLK, LANES)) for hh in range(2)]
                row0 = j * (Q_BLK * dil) + r
                dst = pl.ds(row0, Q_BLK, stride=dil) if dil > 1 else pl.ds(row0, Q_BLK)
                o_sc[c, dst, :] = jnp.where(low, outs[0], outs[1])
                l_sc[c, dst, :] = jnp.where(low, lses[0], lses[1])
            return carry

        lax.fori_loop(0, dil * nsub // A_GROUP, group, 0)

    l0, l1, l2 = l_sc[0], l_sc[1], l_sc[2]
    m = jnp.maximum(jnp.maximum(l0, l1), l2)
    e0, e1, e2 = jnp.exp(l0 - m), jnp.exp(l1 - m), jnp.exp(l2 - m)
    mix = (e0 * o_sc[0] + e1 * o_sc[1] + e2 * o_sc[2]) / (e0 + e1 + e2)
    out_ref[...] = mix.astype(BF16)


def _dilated(qkv_by_cfg, *, bsz, s_len):
    nt = s_len // A_TILE
    npair = A_HEADS // 2
    ncfg = len(A_CONFIGS)
    in_specs, operands, kv_scratch = [], [], []
    for (window, dil), arr in zip(A_CONFIGS, qkv_by_cfg):
        rows = A_TILE // dil
        nsub = rows // Q_BLK
        cur = lambda off: (lambda b, h, t: (b, 0, t, off + h))
        halo = lambda off, nsub=nsub: (lambda b, h, t: (b, 0, jnp.maximum(t * nsub - 1, 0), off + h))
        in_specs += [pl.BlockSpec((None, dil, rows, LANES), cur(0)),
                     pl.BlockSpec((None, dil, rows, LANES), cur(npair)),
                     pl.BlockSpec((None, dil, rows, LANES), cur(2 * npair)),
                     pl.BlockSpec((None, dil, Q_BLK, LANES), halo(npair)),
                     pl.BlockSpec((None, dil, Q_BLK, LANES), halo(2 * npair))]
        operands += [arr] * 5
        kv_scratch.append(pltpu.VMEM((dil, Q_BLK + rows, LANES), BF16))
    return pl.pallas_call(
        _dilated_body,
        grid=(bsz, npair, nt),
        in_specs=in_specs,
        out_specs=pl.BlockSpec((A_TILE, LANES), lambda b, h, t: (b * nt + t, h)),
        out_shape=jax.ShapeDtypeStruct((bsz * s_len, A_HEADS * HEAD_DIM), BF16),
        scratch_shapes=[pltpu.VMEM((ncfg, A_TILE, LANES), F32),
                        pltpu.VMEM((ncfg, A_TILE, LANES), F32),
                        pltpu.VMEM((ncfg * 4, Q_BLK, 2 * Q_BLK), F32)] + kv_scratch + kv_scratch,
        compiler_params=pltpu.CompilerParams(dimension_semantics=("arbitrary",) * 3,
                                             vmem_limit_bytes=VMEM_LIMIT),
        name="dilated",
    )(*operands)


def _nsa_body(q_ref, ks0_ref, ks1_ref, vs_ref, kw0_ref, kw1_ref, vw_ref, cmp_ref, ng_ref,
              oh_ref, ovt_ref, out_ref, *, s_len, ncp, topk):
    q0 = pl.program_id(1) * Q_BLK
    lane = _iota((1, LANES), 1)
    low = lane < HEAD_DIM
    t_col = q0 + _iota((Q_BLK, 1), 0)
    t_row = q0 + _iota((1, Q_BLK), 1)
    nsel_pad = LANES
    win_keys = WIN + Q_BLK

    cmp_end = CMP_STRIDE * _iota((1, ncp), 1) + (CMP_LEN - 1)
    mask_c = cmp_end <= t_col
    bias_c = jnp.where(mask_c, 0.0, NEG)
    kstart = pl.multiple_of(jnp.maximum(q0 - WIN, 0), Q_BLK)
    d_w = t_col - (kstart + _iota((1, win_keys), 1))
    bias_w = jnp.where((d_w >= 0) & (d_w < WIN), 0.0, NEG)
    blk_t = _iota((nsel_pad, 1), 0)
    allowed_t = blk_t * SEL_LEN <= t_row
    cur_t = t_row // SEL_LEN
    forced_t = (blk_t == 0) | (blk_t == cur_t) | (blk_t == cur_t - 1)
    blk_f = blk_t.astype(F32)

    sig = jax.nn.sigmoid(ng_ref[...])
    q_all = q_ref[...].astype(F32)
    ks_refs = (ks0_ref, ks1_ref)
    kw_refs = (kw0_ref, kw1_ref)
    n_full = q0 // SLC_TILE
    head_out = [None] * B_HEADS

    for g in range(B_KV_GROUPS):
        rows = []
        for r in range(B_REP):
            h = g * B_REP + r
            slope = 2.0 ** (-8.0 * (h + 1) / B_HEADS)
            blk = q_all[:, (h // 2) * LANES:(h // 2 + 1) * LANES]
            if h % 2 == 1:
                blk = pltpu.roll(blk, HEAD_DIM, axis=1)
            cols = jnp.where(lane == LANE_POS_BLK, SEL_LEN * slope,
                             jnp.where(lane == LANE_POS_IN, slope,
                                       jnp.where(lane == LANE_CMP_HI, 16 * CMP_STRIDE * slope,
                                                 jnp.where(lane == LANE_CMP_LO, CMP_STRIDE * slope, 0.0))))
            rows.append(jnp.where(low, blk * SCALE, cols))
        qg = jnp.concatenate(rows, axis=0).astype(BF16)

        s = _dot_nt(qg, cmp_ref[g]).reshape(B_REP, Q_BLK, ncp) + bias_c[None]
        m = jnp.max(s, axis=-1, keepdims=True)
        e = jnp.where(mask_c[None], jnp.exp(s - m), 0.0)
        den = jnp.sum(e, axis=-1, keepdims=True)
        p = e / jnp.maximum(den, 1e-30)
        o_cmp = _dot(p.reshape(B_REP * Q_BLK, ncp).astype(BF16), cmp_ref[B_KV_GROUPS + g])
        psum = p[0] + p[1] + p[2] + p[3]
        p_hi = psum.astype(BF16)
        p_lo = (psum - p_hi.astype(F32)).astype(BF16)
        imp_t = _dot_nt(ovt_ref[...], p_hi) + _dot_nt(ovt_ref[...], p_lo)

        rank = jnp.where(allowed_t, imp_t + jnp.where(forced_t, FORCE, 0.0), NEG)

        def pick_one(_, carry):
            rank, sel = carry
            best = jnp.max(rank, axis=0, keepdims=True)
            cand = jnp.where(rank == best, blk_f, float(nsel_pad))
            idx = jnp.min(cand, axis=0, keepdims=True)
            pick = blk_f == idx
            return jnp.where(pick, -3e38, rank), jnp.where(pick, 1.0, sel)

        _, sel_t = lax.fori_loop(0, topk, pick_one, (rank, jnp.zeros((nsel_pad, Q_BLK), F32)))
        sel_bias = jnp.where(sel_t.T > 0.5, 0.0, -MASK_BIG).astype(BF16)
        q_slc = jnp.concatenate([qg, jnp.concatenate([sel_bias] * B_REP, axis=0)], axis=1)

        ks_ref = ks_refs[g]

        def slc_tile(kt, carry, diag, ks_ref=ks_ref, q_slc=q_slc):
            m_i, l_i, acc = carry
            k0 = pl.multiple_of(kt * SLC_TILE, SLC_TILE)
            k_aug = jnp.concatenate([ks_ref[pl.ds(k0, SLC_TILE), :], oh_ref[pl.ds(k0, SLC_TILE), :]], axis=1)
            s = _dot_nt(q_slc, k_aug)
            if diag:
                causal = (k0 + _iota((1, SLC_TILE), 1)) <= t_col
                s = (s.reshape(B_REP, Q_BLK, SLC_TILE) + jnp.where(causal, 0.0, NEG)[None]).reshape(
                    B_REP * Q_BLK, SLC_TILE)
            m_new = jnp.maximum(m_i, jnp.max(s, axis=-1, keepdims=True))
            alpha = jnp.exp(m_i - m_new)
            pe = jnp.exp(s - m_new)
            l_new = alpha * l_i + jnp.sum(pe, axis=-1, keepdims=True)
            acc_new = alpha * acc + _dot(pe.astype(BF16), vs_ref[pl.ds(k0, SLC_TILE), :])
            return m_new, l_new, acc_new

        init = (jnp.full((B_REP * Q_BLK, 1), -3e38, F32), jnp.zeros((B_REP * Q_BLK, 1), F32),
                jnp.zeros((B_REP * Q_BLK, LANES), F32))
        carry = lax.fori_loop(0, n_full, functools.partial(slc_tile, diag=False), init)
        _, l_s, acc_s = slc_tile(n_full, carry, True)
        o_slc = acc_s / l_s

        s = _dot_nt(qg, kw_refs[g][pl.ds(kstart, win_keys), :]).reshape(B_REP, Q_BLK, win_keys) + bias_w[None]
        m = jnp.max(s, axis=-1, keepdims=True)
        e = jnp.exp(s - m)
        den = jnp.sum(e, axis=-1, keepdims=True)
        o_win = _dot(e.reshape(B_REP * Q_BLK, win_keys).astype(BF16), vw_ref[pl.ds(kstart, win_keys), :])
        o_win = o_win / den.reshape(B_REP * Q_BLK, 1)

        for r in range(B_REP):
            h = g * B_REP + r
            rs = slice(r * Q_BLK, (r + 1) * Q_BLK)
            o = (sig[:, 3 * h:3 * h + 1] * o_cmp[rs] + sig[:, 3 * h + 1:3 * h + 2] * o_slc[rs]
                 + sig[:, 3 * h + 2:3 * h + 3] * o_win[rs])
            if h % 2 != g:
                o = pltpu.roll(o, HEAD_DIM, axis=1)
            head_out[h] = o

    for hp in range(B_HEADS // 2):
        out_ref[:, hp * LANES:(hp + 1) * LANES] = jnp.where(low, head_out[2 * hp], head_out[2 * hp + 1]).astype(BF16)


def _nsa(qb, kvb, cmp, ng, onehot, ovt, *, bsz, s_len):
    nq = s_len // Q_BLK
    ncp = cmp.shape[2]
    topk = min(SEL_TOPK, s_len // SEL_LEN)
    rowblk = lambda b, i: (b * nq + i, 0)
    res = lambda col: (lambda b, i: (b, col))
    return pl.pallas_call(
        functools.partial(_nsa_body, s_len=s_len, ncp=ncp, topk=topk),
        grid=(bsz, nq),
        in_specs=[pl.BlockSpec((Q_BLK, B_HEADS * HEAD_DIM), rowblk)]
                 + [pl.BlockSpec((s_len, LANES), res(col)) for col in range(6)]
                 + [pl.BlockSpec((None, 2 * B_KV_GROUPS, ncp, LANES), lambda b, i: (b, 0, 0, 0)),
                    pl.BlockSpec((Q_BLK, LANES), rowblk),
                    pl.BlockSpec(onehot.shape, lambda b, i: (0, 0)),
                    pl.BlockSpec(ovt.shape, lambda b, i: (0, 0))],
        out_specs=pl.BlockSpec((Q_BLK, B_HEADS * HEAD_DIM), rowblk),
        out_shape=jax.ShapeDtypeStruct((bsz * s_len, B_HEADS * HEAD_DIM), BF16),
        compiler_params=pltpu.CompilerParams(dimension_semantics=("arbitrary", "arbitrary"),
                                             vmem_limit_bytes=VMEM_LIMIT),
        name="nsa",
    )(qb, kvb, kvb, kvb, kvb, kvb, kvb, cmp, ng, onehot, ovt)


def _q_aug(q_all, g, lane, low):
    rows = []
    for r in range(B_REP):
        h = g * B_REP + r
        slope = 2.0 ** (-8.0 * (h + 1) / B_HEADS)
        blk = q_all[:, (h // 2) * LANES:(h // 2 + 1) * LANES]
        if h % 2 == 1:
            blk = pltpu.roll(blk, HEAD_DIM, axis=1)
        cols = jnp.where(lane == LANE_POS_BLK, SEL_LEN * slope,
                         jnp.where(lane == LANE_POS_IN, slope,
                                   jnp.where(lane == LANE_CMP_HI, 16 * CMP_STRIDE * slope,
                                             jnp.where(lane == LANE_CMP_LO, CMP_STRIDE * slope, 0.0))))
        rows.append(jnp.where(low, blk * SCALE, cols))
    return jnp.concatenate(rows, axis=0).astype(BF16)


def _place_heads(per_head, low):
    placed = []
    for h, o in enumerate(per_head):
        placed.append(pltpu.roll(o, HEAD_DIM, axis=1) if h % 2 != h // B_REP else o)
    return [jnp.where(low, placed[2 * hp], placed[2 * hp + 1]) for hp in range(B_HEADS // 2)]


def _nsa_select_body(q_ref, cmp_ref, ng_ref, ovt_ref, gmap_ref, selb_ref, oc_ref, flags_ref, *, ncp, topk):
    q0 = pl.program_id(1) * Q_BLK
    lane = _iota((1, LANES), 1)
    low = lane < HEAD_DIM
    t_col = q0 + _iota((Q_BLK, 1), 0)
    t_row = q0 + _iota((1, Q_BLK), 1)
    nsel_pad = LANES
    groups = range(B_KV_GROUPS)

    cmp_end = CMP_STRIDE * _iota((1, ncp), 1) + (CMP_LEN - 1)
    mask_c = cmp_end <= t_col
    bias_c = jnp.where(mask_c, 0.0, NEG)
    blk_t = _iota((nsel_pad, 1), 0)
    allowed_t = blk_t * SEL_LEN <= t_row
    cur_t = t_row // SEL_LEN
    forced_t = (blk_t == 0) | (blk_t == cur_t) | (blk_t == cur_t - 1)
    blk_f = blk_t.astype(F32)

    q_all = q_ref[...].astype(F32)
    qg = [_q_aug(q_all, g, lane, low) for g in groups]
    scores = [_dot_nt(qg[g], cmp_ref[g]).reshape(B_REP, Q_BLK, ncp) + bias_c[None] for g in groups]
    probs = []
    for s in scores:
        m = jnp.max(s, axis=-1, keepdims=True)
        e = jnp.where(mask_c[None], jnp.exp(s - m), 0.0)
        den = jnp.sum(e, axis=-1, keepdims=True)
        probs.append(e / jnp.maximum(den, 1e-30))
    o_cmp = [_dot(probs[g].reshape(B_REP * Q_BLK, ncp).astype(BF16), cmp_ref[B_KV_GROUPS + g]) for g in groups]
    ranks = []
    for p in probs:
        psum = p[0] + p[1] + p[2] + p[3]
        p_hi = psum.astype(BF16)
        p_lo = (psum - p_hi.astype(F32)).astype(BF16)
        imp_t = _dot_nt(ovt_ref[...], p_hi) + _dot_nt(ovt_ref[...], p_lo)
        ranks.append(jnp.where(allowed_t, imp_t + jnp.where(forced_t, FORCE, 0.0), NEG))

    def pick_one(_, carry):
        out = []
        for rank, sel in carry:
            best = jnp.max(rank, axis=0, keepdims=True)
            cand = jnp.where(rank == best, blk_f, float(nsel_pad))
            idx = jnp.min(cand, axis=0, keepdims=True)
            pick = blk_f == idx
            out.append((jnp.where(pick, -3e38, rank), jnp.where(pick, 1.0, sel)))
        return tuple(out)

    picked = lax.fori_loop(0, topk, pick_one,
                           tuple((rank, jnp.zeros((nsel_pad, Q_BLK), F32)) for rank in ranks))
    ones = jnp.ones((8, Q_BLK), BF16)
    tile_cnt = jnp.zeros((8, LANES), F32)
    for g in groups:
        sel = jnp.where(allowed_t, picked[g][1], 0.0).T
        selb_ref[:, g * LANES:(g + 1) * LANES] = jnp.where(sel > 0.5, 0.0, -MASK_BIG).astype(BF16)
        per_block = _dot(ones, sel.astype(BF16))
        tile_cnt = tile_cnt + _dot(per_block.astype(BF16), gmap_ref[g])
    flags_ref[...] = (tile_cnt > 0.5).astype(jnp.int32)

    sig = jax.nn.sigmoid(ng_ref[...])
    per_head = []
    for h in range(B_HEADS):
        g, r = divmod(h, B_REP)
        per_head.append(sig[:, 3 * h:3 * h + 1] * o_cmp[g][r * Q_BLK:(r + 1) * Q_BLK])
    for hp, tile in enumerate(_place_heads(per_head, low)):
        oc_ref[:, hp * LANES:(hp + 1) * LANES] = tile


def _nsa_attend_body(lists_ref, cnts_ref, q_ref, ks0_ref, ks1_ref, kw0_ref, kw1_ref, vw_ref, vst_ref,
                     selb_ref, oc_ref, ng_ref, oh_ref, out_ref, *, nq, ntile):
    step = pl.program_id(0) * nq + pl.program_id(1)
    q0 = pl.program_id(1) * Q_BLK
    lane = _iota((1, LANES), 1)
    low = lane < HEAD_DIM
    t_col = q0 + _iota((Q_BLK, 1), 0)
    t_row = q0 + _iota((1, Q_BLK), 1)
    win_keys = WIN + Q_BLK
    rows = B_REP * Q_BLK
    groups = range(B_KV_GROUPS)
    ks_refs = (ks0_ref, ks1_ref)
    kw_refs = (kw0_ref, kw1_ref)

    q_all = q_ref[...].astype(F32)
    qg = [_q_aug(q_all, g, lane, low) for g in groups]

    def softmax_step(s, m_i, l_i):
        m_new = jnp.maximum(m_i, jnp.max(s, axis=0, keepdims=True))
        alpha = jnp.exp(m_i - m_new)
        pe = jnp.exp(s - m_new)
        return pe.astype(BF16), alpha, m_new, alpha * l_i + jnp.sum(pe, axis=0, keepdims=True)

    o_slc = []
    for g in groups:
        base = (step * B_KV_GROUPS + g) * LIST_W
        n_list = cnts_ref[step * B_KV_GROUPS + g]
        sel_bias = selb_ref[:, g * LANES:(g + 1) * LANES]
        q_slc = jnp.concatenate([qg[g], jnp.concatenate([sel_bias] * B_REP, axis=0)], axis=1)
        ks_ref = ks_refs[g]

        def tile_id(j, base=base):
            return lists_ref[base + jnp.maximum(j, 0)]

        def qk(j, ks_ref=ks_ref, q_slc=q_slc):
            kt = tile_id(j)
            k0 = pl.multiple_of(jnp.minimum(kt, ntile - 1) * SLC_TILE, SLC_TILE)
            o0 = pl.multiple_of(kt * SLC_TILE, SLC_TILE)
            k_aug = jnp.concatenate([ks_ref[pl.ds(k0, SLC_TILE), :], oh_ref[pl.ds(o0, SLC_TILE), :]], axis=1)
            return _dot_nt(k_aug, q_slc)

        def pv(p, j):
            return _dot(vst_ref[jnp.minimum(tile_id(j), ntile - 1)], p)

        def pair(k, carry):
            s_even, p_prev, a_prev, m_i, l_i, acc = carry
            j = 2 * k
            s_odd = qk(j + 1)
            acc = a_prev * acc + pv(p_prev, j - 1)
            p_even, a_even, m_i, l_i = softmax_step(s_even, m_i, l_i)
            s_next = qk(j + 2)
            acc = a_even * acc + pv(p_even, j)
            p_odd, a_odd, m_i, l_i = softmax_step(s_odd, m_i, l_i)
            return s_next, p_odd, a_odd, m_i, l_i, acc

        init = (qk(0), jnp.zeros((SLC_TILE, rows), BF16), jnp.ones((1, rows), F32),
                jnp.full((1, rows), -3e38, F32), jnp.zeros((1, rows), F32), jnp.zeros((LANES, rows), F32))
        s_diag, p_prev, a_prev, m_i, l_i, acc = lax.fori_loop(0, n_list // 2, pair, init)
        acc = a_prev * acc + pv(p_prev, n_list - 1)
        key_pos = tile_id(n_list) * SLC_TILE + _iota((SLC_TILE, 1), 0)
        causal = jnp.where(key_pos <= t_row, 0.0, NEG)
        s_diag = s_diag + jnp.concatenate([causal] * B_REP, axis=1)
        p, a, m_i, l_i = softmax_step(s_diag, m_i, l_i)
        acc = (a * acc + pv(p, n_list)) / l_i
        o_slc.append(jnp.concatenate([acc[:, r * Q_BLK:(r + 1) * Q_BLK].T for r in range(B_REP)], axis=0))

    kstart = pl.multiple_of(jnp.maximum(q0 - WIN, 0), Q_BLK)
    d_w = t_col - (kstart + _iota((1, win_keys), 1))
    bias_w = jnp.where((d_w >= 0) & (d_w < WIN), 0.0, NEG)
    scores = [_dot_nt(qg[g], kw_refs[g][pl.ds(kstart, win_keys), :]).reshape(B_REP, Q_BLK, win_keys) + bias_w[None]
              for g in groups]
    o_win = []
    for s in scores:
        m = jnp.max(s, axis=-1, keepdims=True)
        e = jnp.exp(s - m)
        den = jnp.sum(e, axis=-1, keepdims=True)
        o = _dot(e.reshape(rows, win_keys).astype(BF16), vw_ref[pl.ds(kstart, win_keys), :])
        o_win.append(o / den.reshape(rows, 1))

    sig = jax.nn.sigmoid(ng_ref[...])
    per_head = []
    for h in range(B_HEADS):
        g, r = divmod(h, B_REP)
        rs = slice(r * Q_BLK, (r + 1) * Q_BLK)
        per_head.append(sig[:, 3 * h + 1:3 * h + 2] * o_slc[g][rs] + sig[:, 3 * h + 2:3 * h + 3] * o_win[g][rs])
    for hp, tile in enumerate(_place_heads(per_head, low)):
        cols = slice(hp * LANES, (hp + 1) * LANES)
        out_ref[:, cols] = (oc_ref[:, cols] + tile).astype(BF16)


def _tile_lists(flags, *, bsz, s_len):
    nq = s_len // Q_BLK
    ntile = s_len // SLC_TILE
    f = flags[:, 0, :B_KV_GROUPS * FLAG_W].reshape(bsz * nq, B_KV_GROUPS, FLAG_W)[:, :, :ntile] > 0
    kt = jnp.arange(ntile, dtype=jnp.int32)
    diag = jnp.tile((jnp.arange(nq, dtype=jnp.int32) * Q_BLK) // SLC_TILE, bsz)[:, None, None]
    touched = f & (kt < diag)
    order = jnp.argsort(jnp.where(touched, kt, ntile + kt), axis=-1).astype(jnp.int32)
    n_touched = jnp.sum(touched, axis=-1).astype(jnp.int32)
    cnt = jnp.maximum(2 * ((n_touched + 1) // 2), 2)
    pos = jnp.arange(LIST_W, dtype=jnp.int32)
    order = jnp.pad(order, ((0, 0), (0, 0), (0, LIST_W - ntile)))
    lists = jnp.where(pos < n_touched[..., None], order, jnp.where(pos == cnt[..., None], diag, ntile))
    return lists.reshape(-1), cnt.reshape(-1)


def _nsa2(qb, kvb, vst, cmp, ng, onehot, ovt, gmap, *, bsz, s_len):
    nq = s_len // Q_BLK
    ntile = s_len // SLC_TILE
    ncp = cmp.shape[2]
    topk = min(SEL_TOPK, s_len // SEL_LEN)
    hw = B_HEADS * HEAD_DIM
    rowblk = lambda b, i: (b * nq + i, 0)
    params = pltpu.CompilerParams(dimension_semantics=("arbitrary", "arbitrary"), vmem_limit_bytes=VMEM_LIMIT)
    selb, oc, flags = pl.pallas_call(
        functools.partial(_nsa_select_body, ncp=ncp, topk=topk),
        grid=(bsz, nq),
        in_specs=[pl.BlockSpec((Q_BLK, hw), rowblk),
                  pl.BlockSpec((None, 2 * B_KV_GROUPS, ncp, LANES), lambda b, i: (b, 0, 0, 0)),
                  pl.BlockSpec((Q_BLK, LANES), rowblk),
                  pl.BlockSpec(ovt.shape, lambda b, i: (0, 0)),
                  pl.BlockSpec(gmap.shape, lambda b, i: (0, 0, 0))],
        out_specs=[pl.BlockSpec((Q_BLK, B_KV_GROUPS * LANES), rowblk),
                   pl.BlockSpec((Q_BLK, hw), rowblk),
                   pl.BlockSpec((None, 8, LANES), lambda b, i: (b * nq + i, 0, 0))],
        out_shape=[jax.ShapeDtypeStruct((bsz * s_len, B_KV_GROUPS * LANES), BF16),
                   jax.ShapeDtypeStruct((bsz * s_len, hw), F32),
                   jax.ShapeDtypeStruct((bsz * nq, 8, LANES), jnp.int32)],
        compiler_params=params,
        name="nsa_select",
    )(qb, cmp, ng, ovt, gmap)

    lists, cnts = _tile_lists(flags, bsz=bsz, s_len=s_len)
    rowblk2 = lambda b, i, lists, cnts: (b * nq + i, 0)
    res = lambda col: (lambda b, i, lists, cnts: (b, col))
    return pl.pallas_call(
        functools.partial(_nsa_attend_body, nq=nq, ntile=ntile),
        grid_spec=pltpu.PrefetchScalarGridSpec(
            num_scalar_prefetch=2,
            grid=(bsz, nq),
            in_specs=[pl.BlockSpec((Q_BLK, hw), rowblk2)]
                     + [pl.BlockSpec((s_len, LANES), res(col)) for col in range(5)]
                     + [pl.BlockSpec((None, ntile, LANES, SLC_TILE), lambda b, i, lists, cnts: (b, 0, 0, 0)),
                        pl.BlockSpec((Q_BLK, B_KV_GROUPS * LANES), rowblk2),
                        pl.BlockSpec((Q_BLK, hw), rowblk2),
                        pl.BlockSpec((Q_BLK, LANES), rowblk2),
                        pl.BlockSpec(onehot.shape, lambda b, i, lists, cnts: (0, 0))],
            out_specs=pl.BlockSpec((Q_BLK, hw), rowblk2)),
        out_shape=jax.ShapeDtypeStruct((bsz * s_len, hw), BF16),
        compiler_params=params,
        name="nsa_attend",
    )(lists, cnts, qb, kvb, kvb, kvb, kvb, kvb, vst, selb, oc, ng, onehot)


def _post_body(x_ref, oa_ref, ob_ref, gab_ref, p_ref, wua_ref, wub_ref, wout_ref, g2_ref, w1_ref, w2_ref,
               g3_ref, wpg_ref, wple_ref, gf_ref, out_ref, *, d, ff_chunk):
    ya = _dot(oa_ref[...], wua_ref[...])
    yb = _dot(ob_ref[...], wub_ref[...])
    mixed = (jax.nn.sigmoid(gab_ref[:, :d].astype(F32)) * ya
             + jax.nn.sigmoid(gab_ref[:, d:].astype(F32)) * yb)
    h = x_ref[...] + _dot(mixed.astype(BF16), wout_ref[...])
    n2 = _rms(h, g2_ref[...]).astype(BF16)
    acc = h
    for c in range(w1_ref.shape[1] // ff_chunk):
        cs = slice(c * ff_chunk, (c + 1) * ff_chunk)
        hid = jnp.square(jnp.maximum(_dot(n2, w1_ref[:, cs]), 0.0))
        acc = acc + _dot(hid.astype(BF16), w2_ref[cs, :])
    n3 = _rms(acc, g3_ref[...]).astype(BF16)
    gate = jax.nn.sigmoid(_dot(n3, wpg_ref[...]))
    h3 = acc + gate * _dot(p_ref[...].astype(BF16), wple_ref[...])
    out_ref[...] = _rms(h3, gf_ref[...])


def _post(x2, oa, ob, gab, p2, wua, wub, wout, g2, w1, w2, g3, wpg, wple, gf, *, tm=256, ff_chunk=1024):
    t_len, d = x2.shape
    row = lambda i: (i, 0)
    const = lambda i: (0, 0)
    resident = lambda a: pl.BlockSpec(a.shape, const, pipeline_mode=pl.Buffered(1))
    acts = (x2, oa, ob, gab, p2)
    params = (wua, wub, wout, g2, w1, w2, g3, wpg, wple, gf)
    return pl.pallas_call(
        functools.partial(_post_body, d=d, ff_chunk=ff_chunk),
        grid=(t_len // tm,),
        in_specs=[pl.BlockSpec((tm, a.shape[1]), row) for a in acts] + [resident(w) for w in params],
        out_specs=pl.BlockSpec((tm, d), row),
        out_shape=jax.ShapeDtypeStruct((t_len, d), F32),
        compiler_params=pltpu.CompilerParams(dimension_semantics=("arbitrary",),
                                             vmem_limit_bytes=VMEM_LIMIT),
        name="post",
    )(*acts, *params)


def _selection_overlap_t(ncp, s_len):
    ncmp = (s_len - CMP_LEN) // CMP_STRIDE + 1
    nsel = s_len // SEL_LEN
    ratio = SEL_LEN // CMP_STRIDE
    span = CMP_LEN // CMP_STRIDE
    i = np.arange(ncmp)[:, None]
    j = np.arange(nsel)[None, :]
    ov = np.maximum(np.minimum(i + span, ratio * (j + 1)) - np.maximum(i, ratio * j), 0)
    out = np.zeros((LANES, ncp), np.float32)
    out[:nsel, :ncmp] = ov.T
    return out


def _layer(h, p_i, norm_mix_g, w_in, pe_ck, w_ck1, w_ck2, pe_cv, w_cv1, w_cv2, w_up_a, w_up_b, w_out,
           norm_mlp_g, w_mlp1, w_mlp2, norm_ple_g, w_ple_gate, w_ple, final_g):
    bsz, s_len, d = h.shape
    t_len = bsz * s_len
    aw = A_HEADS * HEAD_DIM
    bw = B_HEADS * HEAD_DIM
    kvw = B_KV_GROUPS * HEAD_DIM
    assert s_len % A_TILE == 0 and s_len // SEL_LEN <= LANES and kvw == LANES

    o_qb = 3 * aw
    o_kv = o_qb + bw
    o_ng = o_kv + 6 * kvw
    o_ga = o_ng + 3 * B_HEADS
    kv = lambda i: w_in[:, o_kv + i * kvw:o_kv + (i + 1) * kvw]
    zeros_h = jnp.zeros((d, HEAD_DIM), w_in.dtype)
    grp = lambda w, g: jnp.concatenate([w[:, g * HEAD_DIM:(g + 1) * HEAD_DIM], zeros_h], axis=1)
    wa = jnp.concatenate([w_in[:, :aw] * SCALE, w_in[:, aw:3 * aw]], axis=1)
    wc = jnp.concatenate([kv(0), kv(1)], axis=1)
    wq = w_in[:, o_qb:o_qb + bw]
    wkv = jnp.concatenate([grp(kv(2), 0), grp(kv(2), 1), grp(kv(4), 0), grp(kv(4), 1), kv(5)], axis=1)
    wvt = kv(3).T
    wng = jnp.concatenate([w_in[:, o_ng:o_ga], jnp.zeros((d, LANES - 3 * B_HEADS), w_in.dtype)], axis=1)
    wgab = w_in[:, o_ga:]
    x2 = h.reshape(t_len, d)
    a0, a1, a2, kvc, qb, kvb, ng, gab, vst = _proj(x2, norm_mix_g.reshape(1, d), *(w.astype(BF16) for w in
                                                   (wa, wc, wq, wkv, wng, wgab, wvt)), s_len=s_len)

    ncp = s_len // CMP_STRIDE
    half = CMP_STRIDE * HEAD_DIM
    chunks = kvc.reshape(bsz, ncp, CMP_STRIDE, 2 * B_KV_GROUPS, HEAD_DIM).transpose(0, 3, 1, 2, 4)
    chunks = chunks.reshape(bsz, 2 * B_KV_GROUPS, ncp, half)
    pe2 = jnp.stack([pe_ck.reshape(2, half), pe_cv.reshape(2, half)])
    w1s = jnp.stack([w_ck1.reshape(2, half, CMP_HIDDEN), w_cv1.reshape(2, half, CMP_HIDDEN)]).astype(BF16)
    zpad = jnp.zeros((CMP_HIDDEN, HEAD_DIM), w_ck2.dtype)
    w2s = jnp.stack([jnp.concatenate([w_ck2, zpad], axis=1), jnp.concatenate([w_ck2, zpad], axis=1),
                     jnp.concatenate([w_cv2, zpad], axis=1), jnp.concatenate([zpad, w_cv2], axis=1)]).astype(BF16)
    cmp = _compress(chunks, pe2, w1s, w2s)

    oa = _dilated((a0, a1, a2), bsz=bsz, s_len=s_len)
    onehot = (np.arange(s_len + SLC_TILE)[:, None] // SEL_LEN == np.arange(LANES)[None, :])
    onehot[s_len:] = True
    blocks_per_tile = SLC_TILE // SEL_LEN
    gmap = np.zeros((B_KV_GROUPS, LANES, LANES), np.float32)
    for g in range(B_KV_GROUPS):
        gmap[g, np.arange(LANES), FLAG_W * g + np.arange(LANES) // blocks_per_tile] = 1.0
    ob = _nsa2(qb, kvb, vst, cmp, ng, jnp.asarray(onehot, BF16),
               jnp.asarray(_selection_overlap_t(ncp, s_len), BF16), jnp.asarray(gmap, BF16), bsz=bsz, s_len=s_len)

    b16 = lambda w: w.astype(BF16)
    row = lambda v: v.reshape(1, d)
    return _post(x2, oa, ob, gab, p_i.reshape(t_len, -1), b16(w_up_a), b16(w_up_b), b16(w_out), row(norm_mlp_g),
                 b16(w_mlp1), b16(w_mlp2), row(norm_ple_g), b16(w_ple_gate), b16(w_ple), row(final_g)
                 ).reshape(bsz, s_len, d)


def kernel(x, p, norm_mix_g, w_in, pe_ck, w_ck1, w_ck2, pe_cv, w_cv1, w_cv2, w_up_a, w_up_b, w_out,
           norm_mlp_g, w_mlp1, w_mlp2, norm_ple_g, w_ple_gate, w_ple, norm_final_g):
    depth = w_in.shape[0]
    assert depth == 1, "the fused tail applies the final norm inside the single layer"
    return _layer(x, p[0], norm_mix_g[0], w_in[0], pe_ck[0], w_ck1[0], w_ck2[0], pe_cv[0], w_cv1[0], w_cv2[0],
                  w_up_a[0], w_up_b[0], w_out[0], norm_mlp_g[0], w_mlp1[0], w_mlp2[0], norm_ple_g[0],
                  w_ple_gate[0], w_ple[0], norm_final_g)
```

```python
import functools

import numpy as np
import jax
import jax.numpy as jnp
from jax import lax
from jax.experimental import pallas as pl
from jax.experimental.pallas import tpu as pltpu

HEAD_DIM = 64
A_HEADS = 8
A_CONFIGS = ((128, 1), (512, 4), (2048, 16))
B_HEADS = 8
B_KV_GROUPS = 2
B_REP = B_HEADS // B_KV_GROUPS
CMP_LEN = 32
CMP_STRIDE = 16
CMP_HIDDEN = 256
SEL_LEN = 64
SEL_TOPK = 16
WIN = 512
Q_BLK = 128
EPS = 1e-6
NEG = -1e30
FORCE = 1e9
MASK_BIG = 2.0 ** 100
SCALE = HEAD_DIM ** -0.5

LANES = 128
A_TILE = 2048
A_GROUP = 4
SLC_TILE = 256
VT_ROWS = HEAD_DIM + 16
FLAG_W = 32
LIST_W = FLAG_W + 2
VMEM_LIMIT = 56 * 1024 * 1024

LANE_POS_BLK = 64
LANE_POS_IN = 65
LANE_CMP_HI = 66
LANE_CMP_LO = 67

F32 = jnp.float32
BF16 = jnp.bfloat16


def _dot(a, b):
    return jnp.dot(a, b, preferred_element_type=F32)


def _dot_nt(a, b):
    return lax.dot_general(a, b, (((1,), (1,)), ((), ())), preferred_element_type=F32)


def _rms(x, g):
    inv = lax.rsqrt(jnp.mean(x * x, axis=-1, keepdims=True) + EPS)
    return (x * inv) * g


def _iota(shape, dim, dtype=jnp.int32):
    return lax.broadcasted_iota(dtype, shape, dim)


def _proj_body(x_ref, g_ref, wa_ref, wc_ref, wq_ref, wkv_ref, wng_ref, wgab_ref, wvt_ref,
               a0_ref, a1_ref, a2_ref, kvc_ref, qb_ref, kvb_ref, ng_ref, gab_ref, vst_ref, res_sc, *, tm, s_len):
    n = _rms(x_ref[...], g_ref[...]).astype(BF16)
    vt = _dot_nt(wvt_ref[...], n)
    for u in range(tm // SLC_TILE):
        for g in range(B_KV_GROUPS):
            vst_ref[u, g, :HEAD_DIM, :] = vt[g * HEAD_DIM:(g + 1) * HEAD_DIM,
                                             u * SLC_TILE:(u + 1) * SLC_TILE].astype(BF16)
            vst_ref[u, g, HEAD_DIM:, :] = jnp.ones((VT_ROWS - HEAD_DIM, SLC_TILE), BF16)
    res = _dot(n, wa_ref[...])
    a_refs = (a0_ref, a1_ref, a2_ref)
    for s in range(res.shape[1] // LANES):
        cols = slice(s * LANES, (s + 1) * LANES)
        res_sc[s] = res[:, cols]
        for (window, dil), a_ref in zip(A_CONFIGS, a_refs):
            for r in range(dil):
                src = pl.ds(r, tm // dil, stride=dil) if dil > 1 else pl.ds(0, tm)
                a_ref[r, :, cols] = res_sc[s, src, :].astype(BF16)
    kvc_ref[...] = _dot(n, wc_ref[...])
    qb_ref[...] = _dot(n, wq_ref[...]).astype(BF16)
    ng_ref[...] = _dot(n, wng_ref[...])
    gab_ref[...] = _dot(n, wgab_ref[...]).astype(BF16)
    pos = (pl.program_id(0) * tm) % s_len + _iota((tm, LANES), 0)
    lane = _iota((tm, LANES), 1)
    posc = jnp.where(lane == LANE_POS_BLK, (pos // SEL_LEN).astype(F32),
                     jnp.where(lane == LANE_POS_IN, (pos % SEL_LEN).astype(F32), 0.0))
    kv = _dot(n, wkv_ref[...])
    for c in range(kv.shape[1] // LANES):
        blk = kv[:, c * LANES:(c + 1) * LANES]
        if c < 2 * B_KV_GROUPS:
            blk = blk + posc
        kvb_ref[:, c * LANES:(c + 1) * LANES] = blk.astype(BF16)


def _proj(x2, g, wa, wc, wq, wkv, wng, wgab, wvt, *, s_len, tm=2 * SLC_TILE):
    t_len, d = x2.shape
    bsz = t_len // s_len
    nrt = s_len // tm
    const = lambda i: (0, 0)
    row = lambda i: (i, 0)
    ws = (wa, wc, wq, wkv, wng, wgab, wvt)
    flat = (wc, wq, wkv, wng, wgab)
    flat_dtypes = (F32, BF16, BF16, F32, BF16)
    aw = wa.shape[1]
    a_specs = [pl.BlockSpec((None, dil, tm // dil, aw), lambda i: (i // nrt, 0, i % nrt, 0))
               for _, dil in A_CONFIGS]
    a_shapes = [jax.ShapeDtypeStruct((bsz, dil, s_len // dil, aw), BF16) for _, dil in A_CONFIGS]
    return pl.pallas_call(
        functools.partial(_proj_body, tm=tm, s_len=s_len),
        grid=(t_len // tm,),
        in_specs=[pl.BlockSpec((tm, d), row), pl.BlockSpec((1, d), const)]
                 + [pl.BlockSpec(w.shape, const, pipeline_mode=pl.Buffered(1)) for w in ws],
        out_specs=a_specs + [pl.BlockSpec((tm, w.shape[1]), row) for w in flat]
                  + [pl.BlockSpec((None, tm // SLC_TILE, B_KV_GROUPS, VT_ROWS, SLC_TILE),
                                  lambda i: (i // nrt, i % nrt, 0, 0, 0))],
        out_shape=a_shapes + [jax.ShapeDtypeStruct((t_len, w.shape[1]), dt) for w, dt in zip(flat, flat_dtypes)]
                  + [jax.ShapeDtypeStruct((bsz, s_len // SLC_TILE, B_KV_GROUPS, VT_ROWS, SLC_TILE), BF16)],
        scratch_shapes=[pltpu.VMEM((aw // LANES, tm, LANES), F32)],
        compiler_params=pltpu.CompilerParams(dimension_semantics=("arbitrary",),
                                             vmem_limit_bytes=VMEM_LIMIT),
        name="proj",
    )(x2, g, *ws)


def _gelu_tanh(x):
    return 0.5 * x * (1.0 + jnp.tanh(np.sqrt(2.0 / np.pi).astype(np.float32) * (x + 0.044715 * (x * x * x))))


def _compress_body(ch_ref, pe_ref, w1_ref, w2_ref, out_ref, *, ncp):
    ch = ch_ref[...]
    xa = (ch + pe_ref[0:1, :]).astype(BF16)
    xb = (ch + pe_ref[1:2, :]).astype(BF16)
    a = _dot(xa, w1_ref[0])
    b = _dot(xb, w1_ref[1])
    pre = a + jnp.concatenate([b[1:], b[:1]], axis=0)
    hid = _gelu_tanh(pre).astype(BF16)
    out = _dot(hid, w2_ref[...])
    kv_is_key = pl.program_id(1) < B_KV_GROUPS
    n_idx = _iota((ncp, LANES), 0)
    lane = _iota((ncp, LANES), 1)
    nc = jnp.where(lane == LANE_CMP_HI, (n_idx // 16).astype(F32),
                   jnp.where(lane == LANE_CMP_LO, (n_idx % 16).astype(F32), 0.0))
    out = out + jnp.where(kv_is_key, nc, 0.0)
    out_ref[...] = out.astype(BF16)


def _compress(chunks, pe2, w1s, w2s):
    bsz, nslot, ncp, width = chunks.shape
    return pl.pallas_call(
        functools.partial(_compress_body, ncp=ncp),
        grid=(bsz, nslot),
        in_specs=[pl.BlockSpec((None, None, ncp, width), lambda b, s: (b, s, 0, 0)),
                  pl.BlockSpec((None, 2, width), lambda b, s: (s // B_KV_GROUPS, 0, 0)),
                  pl.BlockSpec((None, 2, width, CMP_HIDDEN), lambda b, s: (s // B_KV_GROUPS, 0, 0, 0)),
                  pl.BlockSpec((None, CMP_HIDDEN, LANES), lambda b, s: (s, 0, 0))],
        out_specs=pl.BlockSpec((None, None, ncp, LANES), lambda b, s: (b, s, 0, 0)),
        out_shape=jax.ShapeDtypeStruct((bsz, nslot, ncp, LANES), BF16),
        compiler_params=pltpu.CompilerParams(dimension_semantics=("arbitrary", "arbitrary"),
                                             vmem_limit_bytes=VMEM_LIMIT),
        name="compress",
    )(chunks, pe2, w1s, w2s)


def _head_slope(h, n_heads):
    out = jnp.float32(2.0 ** (-8.0 * n_heads / n_heads))
    for k in range(n_heads - 1):
        out = jnp.where(h == k, jnp.float32(2.0 ** (-8.0 * (k + 1) / n_heads)), out)
    return out


def _dilated_body(*refs):
    ncfg = len(A_CONFIGS)
    in_refs = refs[:5 * ncfg]
    out_ref = refs[5 * ncfg]
    o_sc, l_sc, m_sc, bias_sc = refs[5 * ncfg + 1:5 * ncfg + 5]
    kbufs = refs[5 * ncfg + 5:5 * ncfg + 5 + ncfg]
    vbufs = refs[5 * ncfg + 5 + ncfg:]
    hp = pl.program_id(1)
    first_tile = pl.program_id(2) == 0
    low = _iota((1, LANES), 1) < HEAD_DIM

    @pl.when(first_tile)
    def _():
        qi = _iota((Q_BLK, 2 * Q_BLK), 0)
        kj = _iota((Q_BLK, 2 * Q_BLK), 1)
        dist = qi - kj + Q_BLK
        for c, (window, dil) in enumerate(A_CONFIGS):
            valid = (dist >= 0) & (dist <= window // dil)
            for hh in range(2):
                slope = _head_slope(2 * hp + hh, A_HEADS)
                bias = jnp.where(valid, -(slope * dil) * dist.astype(F32), NEG)
                bias_sc[(c * 2 + hh) * 2] = bias
                bias_sc[(c * 2 + hh) * 2 + 1] = jnp.where(kj < Q_BLK, NEG, bias)

    for c, (window, dil) in enumerate(A_CONFIGS):
        q_ref, k_ref, v_ref, kh_ref, vh_ref = in_refs[5 * c:5 * c + 5]
        kbuf, vbuf = kbufs[c], vbufs[c]
        rows = A_TILE // dil
        nsub = rows // Q_BLK
        kbuf[:, :Q_BLK, :] = kh_ref[...]
        kbuf[:, Q_BLK:, :] = k_ref[...]
        vbuf[:, :Q_BLK, :] = vh_ref[...]
        vbuf[:, Q_BLK:, :] = v_ref[...]

        def group(gidx, carry, c=c, dil=dil, nsub=nsub, q_ref=q_ref, kbuf=kbuf, vbuf=vbuf):
            subs = []
            for u in range(A_GROUP):
                idx = gidx * A_GROUP + u
                r = idx // nsub
                j = idx % nsub
                j0 = pl.multiple_of(j * Q_BLK, Q_BLK)
                seq_start = ((j == 0) & first_tile).astype(jnp.int32)
                subs.append((r, j, j0, seq_start))
            scores = []
            for r, j, j0, seq_start in subs:
                q = q_ref[r, pl.ds(j0, Q_BLK), :]
                k2 = kbuf[r, pl.ds(j0, 2 * Q_BLK), :]
                for hh in range(2):
                    qm = jnp.where(low == (hh == 0), q, jnp.zeros_like(q))
                    scores.append(_dot_nt(qm, k2) + bias_sc[(c * 2 + hh) * 2 + seq_start])
            probs = []
            for s in scores:
                m = jnp.max(s, axis=-1, keepdims=True)
                e = jnp.exp(s - m)
                probs.append((e.astype(BF16), m, jnp.sum(e, axis=-1, keepdims=True)))
            for u, (r, j, j0, seq_start) in enumerate(subs):
                v2 = vbuf[r, pl.ds(j0, 2 * Q_BLK), :]
                outs = [_dot(probs[2 * u + hh][0], v2) for hh in range(2)]
                stats = [[jnp.broadcast_to(probs[2 * u + hh][i], (Q_BLK, LANES)) for hh in range(2)] for i in (1, 2)]
                row0 = j * (Q_BLK * dil) + r
                dst = pl.ds(row0, Q_BLK, stride=dil) if dil > 1 else pl.ds(row0, Q_BLK)
                o_sc[c, dst, :] = jnp.where(low, outs[0], outs[1])
                m_sc[c, dst, :] = jnp.where(low, stats[0][0], stats[0][1])
                l_sc[c, dst, :] = jnp.where(low, stats[1][0], stats[1][1])
            return carry

        lax.fori_loop(0, dil * nsub // A_GROUP, group, 0)

    m = jnp.maximum(jnp.maximum(m_sc[0], m_sc[1]), m_sc[2])
    num = jnp.zeros((A_TILE, LANES), F32)
    den = jnp.zeros((A_TILE, LANES), F32)
    for c in range(ncfg):
        e = jnp.exp(m_sc[c] - m)
        num = num + e * o_sc[c]
        den = den + e * l_sc[c]
    out_ref[...] = (num / den).astype(BF16)


def _dilated(qkv_by_cfg, *, bsz, s_len):
    nt = s_len // A_TILE
    npair = A_HEADS // 2
    ncfg = len(A_CONFIGS)
    in_specs, operands, kv_scratch = [], [], []
    for (window, dil), arr in zip(A_CONFIGS, qkv_by_cfg):
        rows = A_TILE // dil
        nsub = rows // Q_BLK
        cur = lambda off: (lambda b, h, t: (b, 0, t, off + h))
        halo = lambda off, nsub=nsub: (lambda b, h, t: (b, 0, jnp.maximum(t * nsub - 1, 0), off + h))
        in_specs += [pl.BlockSpec((None, dil, rows, LANES), cur(0)),
                     pl.BlockSpec((None, dil, rows, LANES), cur(npair)),
                     pl.BlockSpec((None, dil, rows, LANES), cur(2 * npair)),
                     pl.BlockSpec((None, dil, Q_BLK, LANES), halo(npair)),
                     pl.BlockSpec((None, dil, Q_BLK, LANES), halo(2 * npair))]
        operands += [arr] * 5
        kv_scratch.append(pltpu.VMEM((dil, Q_BLK + rows, LANES), BF16))
    return pl.pallas_call(
        _dilated_body,
        grid=(bsz, npair, nt),
        in_specs=in_specs,
        out_specs=pl.BlockSpec((A_TILE, LANES), lambda b, h, t: (b * nt + t, h)),
        out_shape=jax.ShapeDtypeStruct((bsz * s_len, A_HEADS * HEAD_DIM), BF16),
        scratch_shapes=[pltpu.VMEM((ncfg, A_TILE, LANES), F32)] * 3
                       + [pltpu.VMEM((ncfg * 4, Q_BLK, 2 * Q_BLK), F32)] + kv_scratch + kv_scratch,
        compiler_params=pltpu.CompilerParams(dimension_semantics=("arbitrary",) * 3,
                                             vmem_limit_bytes=VMEM_LIMIT),
        name="dilated",
    )(*operands)


def _nsa_body(q_ref, ks0_ref, ks1_ref, vs_ref, kw0_ref, kw1_ref, vw_ref, cmp_ref, ng_ref,
              oh_ref, ovt_ref, out_ref, *, s_len, ncp, topk):
    q0 = pl.program_id(1) * Q_BLK
    lane = _iota((1, LANES), 1)
    low = lane < HEAD_DIM
    t_col = q0 + _iota((Q_BLK, 1), 0)
    t_row = q0 + _iota((1, Q_BLK), 1)
    nsel_pad = LANES
    win_keys = WIN + Q_BLK

    cmp_end = CMP_STRIDE * _iota((1, ncp), 1) + (CMP_LEN - 1)
    mask_c = cmp_end <= t_col
    bias_c = jnp.where(mask_c, 0.0, NEG)
    kstart = pl.multiple_of(jnp.maximum(q0 - WIN, 0), Q_BLK)
    d_w = t_col - (kstart + _iota((1, win_keys), 1))
    bias_w = jnp.where((d_w >= 0) & (d_w < WIN), 0.0, NEG)
    blk_t = _iota((nsel_pad, 1), 0)
    allowed_t = blk_t * SEL_LEN <= t_row
    cur_t = t_row // SEL_LEN
    forced_t = (blk_t == 0) | (blk_t == cur_t) | (blk_t == cur_t - 1)
    blk_f = blk_t.astype(F32)

    sig = jax.nn.sigmoid(ng_ref[...])
    q_all = q_ref[...].astype(F32)
    ks_refs = (ks0_ref, ks1_ref)
    kw_refs = (kw0_ref, kw1_ref)
    n_full = q0 // SLC_TILE
    head_out = [None] * B_HEADS

    for g in range(B_KV_GROUPS):
        rows = []
        for r in range(B_REP):
            h = g * B_REP + r
            slope = 2.0 ** (-8.0 * (h + 1) / B_HEADS)
            blk = q_all[:, (h // 2) * LANES:(h // 2 + 1) * LANES]
            if h % 2 == 1:
                blk = pltpu.roll(blk, HEAD_DIM, axis=1)
            cols = jnp.where(lane == LANE_POS_BLK, SEL_LEN * slope,
                             jnp.where(lane == LANE_POS_IN, slope,
                                       jnp.where(lane == LANE_CMP_HI, 16 * CMP_STRIDE * slope,
                                                 jnp.where(lane == LANE_CMP_LO, CMP_STRIDE * slope, 0.0))))
            rows.append(jnp.where(low, blk * SCALE, cols))
        qg = jnp.concatenate(rows, axis=0).astype(BF16)

        s = _dot_nt(qg, cmp_ref[g]).reshape(B_REP, Q_BLK, ncp) + bias_c[None]
        m = jnp.max(s, axis=-1, keepdims=True)
        e = jnp.where(mask_c[None], jnp.exp(s - m), 0.0)
        den = jnp.sum(e, axis=-1, keepdims=True)
        p = e / jnp.maximum(den, 1e-30)
        o_cmp = _dot(p.reshape(B_REP * Q_BLK, ncp).astype(BF16), cmp_ref[B_KV_GROUPS + g])
        psum = p[0] + p[1] + p[2] + p[3]
        p_hi = psum.astype(BF16)
        p_lo = (psum - p_hi.astype(F32)).astype(BF16)
        imp_t = _dot_nt(ovt_ref[...], p_hi) + _dot_nt(ovt_ref[...], p_lo)

        rank = jnp.where(allowed_t, imp_t + jnp.where(forced_t, FORCE, 0.0), NEG)

        def pick_one(_, carry):
            rank, sel = carry
            best = jnp.max(rank, axis=0, keepdims=True)
            cand = jnp.where(rank == best, blk_f, float(nsel_pad))
            idx = jnp.min(cand, axis=0, keepdims=True)
            pick = blk_f == idx
            return jnp.where(pick, -3e38, rank), jnp.where(pick, 1.0, sel)

        _, sel_t = lax.fori_loop(0, topk, pick_one, (rank, jnp.zeros((nsel_pad, Q_BLK), F32)))
        sel_bias = jnp.where(sel_t.T > 0.5, 0.0, -MASK_BIG).astype(BF16)
        q_slc = jnp.concatenate([qg, jnp.concatenate([sel_bias] * B_REP, axis=0)], axis=1)

        ks_ref = ks_refs[g]

        def slc_tile(kt, carry, diag, ks_ref=ks_ref, q_slc=q_slc):
            m_i, l_i, acc = carry
            k0 = pl.multiple_of(kt * SLC_TILE, SLC_TILE)
            k_aug = jnp.concatenate([ks_ref[pl.ds(k0, SLC_TILE), :], oh_ref[pl.ds(k0, SLC_TILE), :]], axis=1)
            s = _dot_nt(q_slc, k_aug)
            if diag:
                causal = (k0 + _iota((1, SLC_TILE), 1)) <= t_col
                s = (s.reshape(B_REP, Q_BLK, SLC_TILE) + jnp.where(causal, 0.0, NEG)[None]).reshape(
                    B_REP * Q_BLK, SLC_TILE)
            m_new = jnp.maximum(m_i, jnp.max(s, axis=-1, keepdims=True))
            alpha = jnp.exp(m_i - m_new)
            pe = jnp.exp(s - m_new)
            l_new = alpha * l_i + jnp.sum(pe, axis=-1, keepdims=True)
            acc_new = alpha * acc + _dot(pe.astype(BF16), vs_ref[pl.ds(k0, SLC_TILE), :])
            return m_new, l_new, acc_new

        init = (jnp.full((B_REP * Q_BLK, 1), -3e38, F32), jnp.zeros((B_REP * Q_BLK, 1), F32),
                jnp.zeros((B_REP * Q_BLK, LANES), F32))
        carry = lax.fori_loop(0, n_full, functools.partial(slc_tile, diag=False), init)
        _, l_s, acc_s = slc_tile(n_full, carry, True)
        o_slc = acc_s / l_s

        s = _dot_nt(qg, kw_refs[g][pl.ds(kstart, win_keys), :]).reshape(B_REP, Q_BLK, win_keys) + bias_w[None]
        m = jnp.max(s, axis=-1, keepdims=True)
        e = jnp.exp(s - m)
        den = jnp.sum(e, axis=-1, keepdims=True)
        o_win = _dot(e.reshape(B_REP * Q_BLK, win_keys).astype(BF16), vw_ref[pl.ds(kstart, win_keys), :])
        o_win = o_win / den.reshape(B_REP * Q_BLK, 1)

        for r in range(B_REP):
            h = g * B_REP + r
            rs = slice(r * Q_BLK, (r + 1) * Q_BLK)
            o = (sig[:, 3 * h:3 * h + 1] * o_cmp[rs] + sig[:, 3 * h + 1:3 * h + 2] * o_slc[rs]
                 + sig[:, 3 * h + 2:3 * h + 3] * o_win[rs])
            if h % 2 != g:
                o = pltpu.roll(o, HEAD_DIM, axis=1)
            head_out[h] = o

    for hp in range(B_HEADS // 2):
        out_ref[:, hp * LANES:(hp + 1) * LANES] = jnp.where(low, head_out[2 * hp], head_out[2 * hp + 1]).astype(BF16)


def _nsa(qb, kvb, cmp, ng, onehot, ovt, *, bsz, s_len):
    nq = s_len // Q_BLK
    ncp = cmp.shape[2]
    topk = min(SEL_TOPK, s_len // SEL_LEN)
    rowblk = lambda b, i: (b * nq + i, 0)
    res = lambda col: (lambda b, i: (b, col))
    return pl.pallas_call(
        functools.partial(_nsa_body, s_len=s_len, ncp=ncp, topk=topk),
        grid=(bsz, nq),
        in_specs=[pl.BlockSpec((Q_BLK, B_HEADS * HEAD_DIM), rowblk)]
                 + [pl.BlockSpec((s_len, LANES), res(col)) for col in range(6)]
                 + [pl.BlockSpec((None, 2 * B_KV_GROUPS, ncp, LANES), lambda b, i: (b, 0, 0, 0)),
                    pl.BlockSpec((Q_BLK, LANES), rowblk),
                    pl.BlockSpec(onehot.shape, lambda b, i: (0, 0)),
                    pl.BlockSpec(ovt.shape, lambda b, i: (0, 0))],
        out_specs=pl.BlockSpec((Q_BLK, B_HEADS * HEAD_DIM), rowblk),
        out_shape=jax.ShapeDtypeStruct((bsz * s_len, B_HEADS * HEAD_DIM), BF16),
        compiler_params=pltpu.CompilerParams(dimension_semantics=("arbitrary", "arbitrary"),
                                             vmem_limit_bytes=VMEM_LIMIT),
        name="nsa",
    )(qb, kvb, kvb, kvb, kvb, kvb, kvb, cmp, ng, onehot, ovt)


def _q_aug(q_all, g, lane, low):
    rows = []
    for r in range(B_REP):
        h = g * B_REP + r
        slope = 2.0 ** (-8.0 * (h + 1) / B_HEADS)
        blk = q_all[:, (h // 2) * LANES:(h // 2 + 1) * LANES]
        if h % 2 == 1:
            blk = pltpu.roll(blk, HEAD_DIM, axis=1)
        cols = jnp.where(lane == LANE_POS_BLK, SEL_LEN * slope,
                         jnp.where(lane == LANE_POS_IN, slope,
                                   jnp.where(lane == LANE_CMP_HI, 16 * CMP_STRIDE * slope,
                                             jnp.where(lane == LANE_CMP_LO, CMP_STRIDE * slope, 0.0))))
        rows.append(jnp.where(low, blk * SCALE, cols))
    return jnp.concatenate(rows, axis=0).astype(BF16)


def _place_heads(per_head, low):
    placed = []
    for h, o in enumerate(per_head):
        placed.append(pltpu.roll(o, HEAD_DIM, axis=1) if h % 2 != h // B_REP else o)
    return [jnp.where(low, placed[2 * hp], placed[2 * hp + 1]) for hp in range(B_HEADS // 2)]


def _nsa_select_body(q_ref, cmp_ref, ng_ref, ovt_ref, gmap_ref, selb_ref, oc_ref, flags_ref, *, ncp, topk):
    q0 = pl.program_id(1) * Q_BLK
    lane = _iota((1, LANES), 1)
    low = lane < HEAD_DIM
    t_col = q0 + _iota((Q_BLK, 1), 0)
    t_row = q0 + _iota((1, Q_BLK), 1)
    nsel_pad = LANES
    groups = range(B_KV_GROUPS)

    cmp_end = CMP_STRIDE * _iota((1, ncp), 1) + (CMP_LEN - 1)
    mask_c = cmp_end <= t_col
    bias_c = jnp.where(mask_c, 0.0, NEG)
    blk_t = _iota((nsel_pad, 1), 0)
    allowed_t = blk_t * SEL_LEN <= t_row
    cur_t = t_row // SEL_LEN
    forced_t = (blk_t == 0) | (blk_t == cur_t) | (blk_t == cur_t - 1)
    blk_f = blk_t.astype(F32)

    q_all = q_ref[...].astype(F32)
    qg = [_q_aug(q_all, g, lane, low) for g in groups]
    scores = [_dot_nt(qg[g], cmp_ref[g]).reshape(B_REP, Q_BLK, ncp) + bias_c[None] for g in groups]
    probs = []
    for s in scores:
        m = jnp.max(s, axis=-1, keepdims=True)
        e = jnp.where(mask_c[None], jnp.exp(s - m), 0.0)
        den = jnp.sum(e, axis=-1, keepdims=True)
        probs.append(e / jnp.maximum(den, 1e-30))
    o_cmp = [_dot(probs[g].reshape(B_REP * Q_BLK, ncp).astype(BF16), cmp_ref[B_KV_GROUPS + g]) for g in groups]
    ranks = []
    for p in probs:
        psum = p[0] + p[1] + p[2] + p[3]
        p_hi = psum.astype(BF16)
        p_lo = (psum - p_hi.astype(F32)).astype(BF16)
        imp_t = _dot_nt(ovt_ref[...], p_hi) + _dot_nt(ovt_ref[...], p_lo)
        ranks.append(jnp.where(allowed_t, imp_t + jnp.where(forced_t, FORCE, 0.0), NEG))

    def pick_one(_, carry):
        out = []
        for rank, sel in carry:
            best = jnp.max(rank, axis=0, keepdims=True)
            cand = jnp.where(rank == best, blk_f, float(nsel_pad))
            idx = jnp.min(cand, axis=0, keepdims=True)
            pick = blk_f == idx
            out.append((jnp.where(pick, -3e38, rank), jnp.where(pick, 1.0, sel)))
        return tuple(out)

    picked = lax.fori_loop(0, topk, pick_one,
                           tuple((rank, jnp.zeros((nsel_pad, Q_BLK), F32)) for rank in ranks))
    ones = jnp.ones((8, Q_BLK), BF16)
    tile_cnt = jnp.zeros((8, LANES), F32)
    for g in groups:
        sel = jnp.where(allowed_t, picked[g][1], 0.0).T
        selb_ref[:, g * LANES:(g + 1) * LANES] = jnp.where(sel > 0.5, 0.0, -MASK_BIG).astype(BF16)
        per_block = _dot(ones, sel.astype(BF16))
        tile_cnt = tile_cnt + _dot(per_block.astype(BF16), gmap_ref[g])
    flags_ref[...] = (tile_cnt > 0.5).astype(jnp.int32)

    sig = jax.nn.sigmoid(ng_ref[...])
    per_head = []
    for h in range(B_HEADS):
        g, r = divmod(h, B_REP)
        per_head.append(sig[:, 3 * h:3 * h + 1] * o_cmp[g][r * Q_BLK:(r + 1) * Q_BLK])
    for hp, tile in enumerate(_place_heads(per_head, low)):
        oc_ref[:, hp * LANES:(hp + 1) * LANES] = tile


def _nsa_attend_body(lists_ref, cnts_ref, q_ref, ks0_ref, ks1_ref, kw0_ref, kw1_ref, vw_ref, vst_ref,
                     selb_ref, oc_ref, ng_ref, oh_ref, out_ref, *, nq, ntile):
    step = pl.program_id(0) * nq + pl.program_id(1)
    q0 = pl.program_id(1) * Q_BLK
    lane = _iota((1, LANES), 1)
    low = lane < HEAD_DIM
    t_col = q0 + _iota((Q_BLK, 1), 0)
    t_row = q0 + _iota((1, Q_BLK), 1)
    win_keys = WIN + Q_BLK
    rows = B_REP * Q_BLK
    groups = range(B_KV_GROUPS)
    ks_refs = (ks0_ref, ks1_ref)
    kw_refs = (kw0_ref, kw1_ref)

    q_all = q_ref[...].astype(F32)
    qg = [_q_aug(q_all, g, lane, low) for g in groups]

    def softmax_step(s, m_i):
        m_new = jnp.maximum(m_i, jnp.max(s, axis=0, keepdims=True))
        return jnp.exp(s - m_new).astype(BF16), jnp.exp(m_i - m_new), m_new

    o_slc = []
    for g in groups:
        base = (step * B_KV_GROUPS + g) * LIST_W
        n_list = cnts_ref[step * B_KV_GROUPS + g]
        sel_bias = selb_ref[:, g * LANES:(g + 1) * LANES]
        q_slc = jnp.concatenate([qg[g], jnp.concatenate([sel_bias] * B_REP, axis=0)], axis=1)
        ks_ref = ks_refs[g]

        def tile_id(j, base=base):
            return lists_ref[base + jnp.maximum(j, 0)]

        def qk(j, ks_ref=ks_ref, q_slc=q_slc):
            kt = tile_id(j)
            k0 = pl.multiple_of(jnp.minimum(kt, ntile - 1) * SLC_TILE, SLC_TILE)
            o0 = pl.multiple_of(kt * SLC_TILE, SLC_TILE)
            k_aug = jnp.concatenate([ks_ref[pl.ds(k0, SLC_TILE), :], oh_ref[pl.ds(o0, SLC_TILE), :]], axis=1)
            return _dot_nt(k_aug, q_slc)

        def pv(p, j, g=g):
            return _dot(vst_ref[jnp.minimum(tile_id(j), ntile - 1), g], p)

        def pair(k, carry):
            s_even, p_prev, a_prev, m_i, acc = carry
            j = 2 * k
            s_odd = qk(j + 1)
            acc = a_prev * acc + pv(p_prev, j - 1)
            p_even, a_even, m_i = softmax_step(s_even, m_i)
            s_next = qk(j + 2)
            acc = a_even * acc + pv(p_even, j)
            p_odd, a_odd, m_i = softmax_step(s_odd, m_i)
            return s_next, p_odd, a_odd, m_i, acc

        init = (qk(0), jnp.zeros((SLC_TILE, rows), BF16), jnp.ones((1, rows), F32),
                jnp.full((1, rows), -3e38, F32), jnp.zeros((VT_ROWS, rows), F32))
        s_diag, p_prev, a_prev, m_i, acc = lax.fori_loop(0, n_list // 2, pair, init)
        acc = a_prev * acc + pv(p_prev, n_list - 1)
        key_pos = tile_id(n_list) * SLC_TILE + _iota((SLC_TILE, 1), 0)
        causal = jnp.where(key_pos <= t_row, 0.0, NEG)
        s_diag = s_diag + jnp.concatenate([causal] * B_REP, axis=1)
        p, a, m_i = softmax_step(s_diag, m_i)
        acc = a * acc + pv(p, n_list)
        o_t = acc[:HEAD_DIM] / acc[HEAD_DIM:HEAD_DIM + 1]
        o_t = jnp.concatenate([o_t, o_t], axis=0)
        o_slc.append(jnp.concatenate([o_t[:, r * Q_BLK:(r + 1) * Q_BLK].T for r in range(B_REP)], axis=0))

    kstart = pl.multiple_of(jnp.maximum(q0 - WIN, 0), Q_BLK)
    d_w = t_col - (kstart + _iota((1, win_keys), 1))
    bias_w = jnp.where((d_w >= 0) & (d_w < WIN), 0.0, NEG)
    scores = [_dot_nt(qg[g], kw_refs[g][pl.ds(kstart, win_keys), :]).reshape(B_REP, Q_BLK, win_keys) + bias_w[None]
              for g in groups]
    o_win = []
    for s in scores:
        m = jnp.max(s, axis=-1, keepdims=True)
        e = jnp.exp(s - m)
        den = jnp.sum(e, axis=-1, keepdims=True)
        o = _dot(e.reshape(rows, win_keys).astype(BF16), vw_ref[pl.ds(kstart, win_keys), :])
        o_win.append(o / den.reshape(rows, 1))

    sig = jax.nn.sigmoid(ng_ref[...])
    per_head = []
    for h in range(B_HEADS):
        g, r = divmod(h, B_REP)
        rs = slice(r * Q_BLK, (r + 1) * Q_BLK)
        per_head.append(sig[:, 3 * h + 1:3 * h + 2] * o_slc[g][rs] + sig[:, 3 * h + 2:3 * h + 3] * o_win[g][rs])
    for hp, tile in enumerate(_place_heads(per_head, low)):
        cols = slice(hp * LANES, (hp + 1) * LANES)
        out_ref[:, cols] = (oc_ref[:, cols] + tile).astype(BF16)


def _tile_lists(flags, *, bsz, s_len):
    nq = s_len // Q_BLK
    ntile = s_len // SLC_TILE
    f = flags[:, 0, :B_KV_GROUPS * FLAG_W].reshape(bsz * nq, B_KV_GROUPS, FLAG_W)[:, :, :ntile] > 0
    kt = jnp.arange(ntile, dtype=jnp.int32)
    diag = jnp.tile((jnp.arange(nq, dtype=jnp.int32) * Q_BLK) // SLC_TILE, bsz)[:, None, None]
    touched = f & (kt < diag)
    rank = jnp.cumsum(touched, axis=-1, dtype=jnp.int32) - 1
    n_touched = rank[..., -1] + 1
    cnt = jnp.maximum(2 * ((n_touched + 1) // 2), 2)
    pos = jnp.arange(LIST_W, dtype=jnp.int32)
    hit = touched[..., None, :] & (rank[..., None, :] == pos[:, None])
    order = jnp.sum(jnp.where(hit, kt, 0), axis=-1)
    lists = jnp.where(pos < n_touched[..., None], order, jnp.where(pos == cnt[..., None], diag, ntile))
    return lists.reshape(-1), cnt.reshape(-1)


def _nsa2(qb, kvb, vst, cmp, ng, onehot, ovt, gmap, *, bsz, s_len):
    nq = s_len // Q_BLK
    ntile = s_len // SLC_TILE
    ncp = cmp.shape[2]
    topk = min(SEL_TOPK, s_len // SEL_LEN)
    hw = B_HEADS * HEAD_DIM
    rowblk = lambda b, i: (b * nq + i, 0)
    params = pltpu.CompilerParams(dimension_semantics=("arbitrary", "arbitrary"), vmem_limit_bytes=VMEM_LIMIT)
    selb, oc, flags = pl.pallas_call(
        functools.partial(_nsa_select_body, ncp=ncp, topk=topk),
        grid=(bsz, nq),
        in_specs=[pl.BlockSpec((Q_BLK, hw), rowblk),
                  pl.BlockSpec((None, 2 * B_KV_GROUPS, ncp, LANES), lambda b, i: (b, 0, 0, 0)),
                  pl.BlockSpec((Q_BLK, LANES), rowblk),
                  pl.BlockSpec(ovt.shape, lambda b, i: (0, 0)),
                  pl.BlockSpec(gmap.shape, lambda b, i: (0, 0, 0))],
        out_specs=[pl.BlockSpec((Q_BLK, B_KV_GROUPS * LANES), rowblk),
                   pl.BlockSpec((Q_BLK, hw), rowblk),
                   pl.BlockSpec((None, 8, LANES), lambda b, i: (b * nq + i, 0, 0))],
        out_shape=[jax.ShapeDtypeStruct((bsz * s_len, B_KV_GROUPS * LANES), BF16),
                   jax.ShapeDtypeStruct((bsz * s_len, hw), F32),
                   jax.ShapeDtypeStruct((bsz * nq, 8, LANES), jnp.int32)],
        compiler_params=params,
        name="nsa_select",
    )(qb, cmp, ng, ovt, gmap)

    lists, cnts = _tile_lists(flags, bsz=bsz, s_len=s_len)
    rowblk2 = lambda b, i, lists, cnts: (b * nq + i, 0)
    res = lambda col: (lambda b, i, lists, cnts: (b, col))
    return pl.pallas_call(
        functools.partial(_nsa_attend_body, nq=nq, ntile=ntile),
        grid_spec=pltpu.PrefetchScalarGridSpec(
            num_scalar_prefetch=2,
            grid=(bsz, nq),
            in_specs=[pl.BlockSpec((Q_BLK, hw), rowblk2)]
                     + [pl.BlockSpec((s_len, LANES), res(col)) for col in range(5)]
                     + [pl.BlockSpec((None, ntile, B_KV_GROUPS, VT_ROWS, SLC_TILE),
                                     lambda b, i, lists, cnts: (b, 0, 0, 0, 0)),
                        pl.BlockSpec((Q_BLK, B_KV_GROUPS * LANES), rowblk2),
                        pl.BlockSpec((Q_BLK, hw), rowblk2),
                        pl.BlockSpec((Q_BLK, LANES), rowblk2),
                        pl.BlockSpec(onehot.shape, lambda b, i, lists, cnts: (0, 0))],
            out_specs=pl.BlockSpec((Q_BLK, hw), rowblk2)),
        out_shape=jax.ShapeDtypeStruct((bsz * s_len, hw), BF16),
        compiler_params=params,
        name="nsa_attend",
    )(lists, cnts, qb, kvb, kvb, kvb, kvb, kvb, vst, selb, oc, ng, onehot)


def _post_body(x_ref, oa_ref, ob_ref, gab_ref, p_ref, wua_ref, wub_ref, wout_ref, g2_ref, w1_ref, w2_ref,
               g3_ref, wpg_ref, wple_ref, gf_ref, out_ref, *, d, ff_chunk):
    ya = _dot(oa_ref[...], wua_ref[...])
    yb = _dot(ob_ref[...], wub_ref[...])
    mixed = (jax.nn.sigmoid(gab_ref[:, :d].astype(F32)) * ya
             + jax.nn.sigmoid(gab_ref[:, d:].astype(F32)) * yb)
    h = x_ref[...] + _dot(mixed.astype(BF16), wout_ref[...])
    n2 = _rms(h, g2_ref[...]).astype(BF16)
    acc = h
    for c in range(w1_ref.shape[1] // ff_chunk):
        cs = slice(c * ff_chunk, (c + 1) * ff_chunk)
        hid = jnp.square(jnp.maximum(_dot(n2, w1_ref[:, cs]), 0.0))
        acc = acc + _dot(hid.astype(BF16), w2_ref[cs, :])
    n3 = _rms(acc, g3_ref[...]).astype(BF16)
    gate = jax.nn.sigmoid(_dot(n3, wpg_ref[...]))
    h3 = acc + gate * _dot(p_ref[...].astype(BF16), wple_ref[...])
    out_ref[...] = _rms(h3, gf_ref[...])


def _post(x2, oa, ob, gab, p2, wua, wub, wout, g2, w1, w2, g3, wpg, wple, gf, *, tm=512, ff_chunk=1024):
    t_len, d = x2.shape
    row = lambda i: (i, 0)
    const = lambda i: (0, 0)
    resident = lambda a: pl.BlockSpec(a.shape, const, pipeline_mode=pl.Buffered(1))
    acts = (x2, oa, ob, gab, p2)
    params = (wua, wub, wout, g2, w1, w2, g3, wpg, wple, gf)
    return pl.pallas_call(
        functools.partial(_post_body, d=d, ff_chunk=ff_chunk),
        grid=(t_len // tm,),
        in_specs=[pl.BlockSpec((tm, a.shape[1]), row) for a in acts] + [resident(w) for w in params],
        out_specs=pl.BlockSpec((tm, d), row),
        out_shape=jax.ShapeDtypeStruct((t_len, d), F32),
        compiler_params=pltpu.CompilerParams(dimension_semantics=("arbitrary",),
                                             vmem_limit_bytes=VMEM_LIMIT),
        name="post",
    )(*acts, *params)


def _selection_overlap_t(ncp, s_len):
    ncmp = (s_len - CMP_LEN) // CMP_STRIDE + 1
    nsel = s_len // SEL_LEN
    ratio = SEL_LEN // CMP_STRIDE
    span = CMP_LEN // CMP_STRIDE
    i = np.arange(ncmp)[:, None]
    j = np.arange(nsel)[None, :]
    ov = np.maximum(np.minimum(i + span, ratio * (j + 1)) - np.maximum(i, ratio * j), 0)
    out = np.zeros((LANES, ncp), np.float32)
    out[:nsel, :ncmp] = ov.T
    return out


def _layer(h, p_i, norm_mix_g, w_in, pe_ck, w_ck1, w_ck2, pe_cv, w_cv1, w_cv2, w_up_a, w_up_b, w_out,
           norm_mlp_g, w_mlp1, w_mlp2, norm_ple_g, w_ple_gate, w_ple, final_g):
    bsz, s_len, d = h.shape
    t_len = bsz * s_len
    aw = A_HEADS * HEAD_DIM
    bw = B_HEADS * HEAD_DIM
    kvw = B_KV_GROUPS * HEAD_DIM
    assert s_len % A_TILE == 0 and s_len // SEL_LEN <= LANES and kvw == LANES

    o_qb = 3 * aw
    o_kv = o_qb + bw
    o_ng = o_kv + 6 * kvw
    o_ga = o_ng + 3 * B_HEADS
    kv = lambda i: w_in[:, o_kv + i * kvw:o_kv + (i + 1) * kvw]
    zeros_h = jnp.zeros((d, HEAD_DIM), w_in.dtype)
    grp = lambda w, g: jnp.concatenate([w[:, g * HEAD_DIM:(g + 1) * HEAD_DIM], zeros_h], axis=1)
    wa = jnp.concatenate([w_in[:, :aw] * SCALE, w_in[:, aw:3 * aw]], axis=1)
    wc = jnp.concatenate([kv(0), kv(1)], axis=1)
    wq = w_in[:, o_qb:o_qb + bw]
    wkv = jnp.concatenate([grp(kv(2), 0), grp(kv(2), 1), grp(kv(4), 0), grp(kv(4), 1), kv(5)], axis=1)
    wvt = kv(3).T
    wng = jnp.concatenate([w_in[:, o_ng:o_ga], jnp.zeros((d, LANES - 3 * B_HEADS), w_in.dtype)], axis=1)
    wgab = w_in[:, o_ga:]
    x2 = h.reshape(t_len, d)
    a0, a1, a2, kvc, qb, kvb, ng, gab, vst = _proj(x2, norm_mix_g.reshape(1, d), *(w.astype(BF16) for w in
                                                   (wa, wc, wq, wkv, wng, wgab, wvt)), s_len=s_len)

    ncp = s_len // CMP_STRIDE
    half = CMP_STRIDE * HEAD_DIM
    chunks = kvc.reshape(bsz, ncp, CMP_STRIDE, 2 * B_KV_GROUPS, HEAD_DIM).transpose(0, 3, 1, 2, 4)
    chunks = chunks.reshape(bsz, 2 * B_KV_GROUPS, ncp, half)
    pe2 = jnp.stack([pe_ck.reshape(2, half), pe_cv.reshape(2, half)])
    w1s = jnp.stack([w_ck1.reshape(2, half, CMP_HIDDEN), w_cv1.reshape(2, half, CMP_HIDDEN)]).astype(BF16)
    zpad = jnp.zeros((CMP_HIDDEN, HEAD_DIM), w_ck2.dtype)
    w2s = jnp.stack([jnp.concatenate([w_ck2, zpad], axis=1), jnp.concatenate([w_ck2, zpad], axis=1),
                     jnp.concatenate([w_cv2, zpad], axis=1), jnp.concatenate([zpad, w_cv2], axis=1)]).astype(BF16)
    cmp = _compress(chunks, pe2, w1s, w2s)

    oa = _dilated((a0, a1, a2), bsz=bsz, s_len=s_len)
    onehot = (np.arange(s_len + SLC_TILE)[:, None] // SEL_LEN == np.arange(LANES)[None, :])
    onehot[s_len:] = True
    blocks_per_tile = SLC_TILE // SEL_LEN
    gmap = np.zeros((B_KV_GROUPS, LANES, LANES), np.float32)
    for g in range(B_KV_GROUPS):
        gmap[g, np.arange(LANES), FLAG_W * g + np.arange(LANES) // blocks_per_tile] = 1.0
    ob = _nsa2(qb, kvb, vst, cmp, ng, jnp.asarray(onehot, BF16),
               jnp.asarray(_selection_overlap_t(ncp, s_len), BF16), jnp.asarray(gmap, BF16), bsz=bsz, s_len=s_len)

    b16 = lambda w: w.astype(BF16)
    row = lambda v: v.reshape(1, d)
    return _post(x2, oa, ob, gab, p_i.reshape(t_len, -1), b16(w_up_a), b16(w_up_b), b16(w_out), row(norm_mlp_g),
                 b16(w_mlp1), b16(w_mlp2), row(norm_ple_g), b16(w_ple_gate), b16(w_ple), row(final_g)
                 ).reshape(bsz, s_len, d)


def kernel(x, p, norm_mix_g, w_in, pe_ck, w_ck1, w_ck2, pe_cv, w_cv1, w_cv2, w_up_a, w_up_b, w_out,
           norm_mlp_g, w_mlp1, w_mlp2, norm_ple_g, w_ple_gate, w_ple, norm_final_g):
    depth = w_in.shape[0]
    assert depth == 1, "the fused tail applies the final norm inside the single layer"
    return _layer(x, p[0], norm_mix_g[0], w_in[0], pe_ck[0], w_ck1[0], w_ck2[0], pe_cv[0], w_cv1[0], w_cv2[0],
                  w_up_a[0], w_up_b[0], w_out[0], norm_mlp_g[0], w_mlp1[0], w_mlp2[0], norm_ple_g[0],
                  w_ple_gate[0], w_ple[0], norm_final_g)
```

```python
import functools

import numpy as np
import jax
import jax.numpy as jnp
from jax import lax
from jax.experimental import pallas as pl
from jax.experimental.pallas import tpu as pltpu

HEAD_DIM = 64
A_HEADS = 8
A_CONFIGS = ((128, 1), (512, 4), (2048, 16))
B_HEADS = 8
B_KV_GROUPS = 2
B_REP = B_HEADS // B_KV_GROUPS
CMP_LEN = 32
CMP_STRIDE = 16
CMP_HIDDEN = 256
SEL_LEN = 64
SEL_TOPK = 16
WIN = 512
Q_BLK = 128
EPS = 1e-6
NEG = -1e30
FORCE = 1e9
MASK_BIG = 2.0 ** 100
SCALE = HEAD_DIM ** -0.5

LANES = 128
A_TILE = 2048
A_GROUP = 4
SLC_TILE = 256
VT_ROWS = HEAD_DIM + 16
FLAG_W = 32
LIST_W = FLAG_W
SLC_TRIP = 2
VMEM_LIMIT = 56 * 1024 * 1024

ALIBI_TERMS = 3
LANE_POS = 64
LANE_CMP = LANE_POS + 2 * ALIBI_TERMS
LOG2E = 1.4426950408889634

F32 = jnp.float32
BF16 = jnp.bfloat16


def _dot(a, b):
    return jnp.dot(a, b, preferred_element_type=F32)


def _dot_nt(a, b):
    return lax.dot_general(a, b, (((1,), (1,)), ((), ())), preferred_element_type=F32)


def _rms(x, g):
    inv = lax.rsqrt(jnp.mean(x * x, axis=-1, keepdims=True) + EPS)
    return (x * inv) * g


def _iota(shape, dim, dtype=jnp.int32):
    return lax.broadcasted_iota(dtype, shape, dim)


def _pair_columns(rel_lane, even_val, odd_val):
    inside = (rel_lane >= 0) & (rel_lane < 2 * ALIBI_TERMS)
    return jnp.where(inside, jnp.where(rel_lane % 2 == 0, even_val, odd_val), 0.0)


def _bf16_pieces(x):
    pieces, rest = [], np.float64(x)
    for _ in range(ALIBI_TERMS):
        piece = np.float64(np.asarray(rest, np.float32).astype(jnp.bfloat16).astype(np.float32))
        pieces.append(float(piece))
        rest = rest - piece
    return pieces


def _query_alibi_row(slope, lane):
    row = jnp.zeros(lane.shape, F32)
    coeffs = ((LANE_POS, SEL_LEN * slope), (LANE_POS + 1, slope),
              (LANE_CMP, 16 * CMP_STRIDE * slope), (LANE_CMP + 1, CMP_STRIDE * slope))
    for lane0, coeff in coeffs:
        for t, piece in enumerate(_bf16_pieces(coeff * LOG2E)):
            row = jnp.where(lane == lane0 + 2 * t, piece, row)
    return row


def _proj_body(x_ref, g_ref, wa_ref, wc_ref, wq_ref, wkv_ref, wng_ref, wgab_ref, wvt_ref,
               a0_ref, a1_ref, a2_ref, kvc_ref, qb_ref, kvb_ref, ng_ref, gab_ref, vst_ref, res_sc, *, tm, s_len):
    n = _rms(x_ref[...], g_ref[...]).astype(BF16)
    vt = _dot_nt(wvt_ref[...], n)
    for u in range(tm // SLC_TILE):
        for g in range(B_KV_GROUPS):
            vst_ref[u, g, :HEAD_DIM, :] = vt[g * HEAD_DIM:(g + 1) * HEAD_DIM,
                                             u * SLC_TILE:(u + 1) * SLC_TILE].astype(BF16)
            vst_ref[u, g, HEAD_DIM:, :] = jnp.ones((VT_ROWS - HEAD_DIM, SLC_TILE), BF16)
    res = _dot(n, wa_ref[...])
    a_refs = (a0_ref, a1_ref, a2_ref)
    for s in range(res.shape[1] // LANES):
        cols = slice(s * LANES, (s + 1) * LANES)
        res_sc[s] = res[:, cols]
        for (window, dil), a_ref in zip(A_CONFIGS, a_refs):
            for r in range(dil):
                src = pl.ds(r, tm // dil, stride=dil) if dil > 1 else pl.ds(0, tm)
                a_ref[r, :, cols] = res_sc[s, src, :].astype(BF16)
    kvc_ref[...] = _dot(n, wc_ref[...])
    qb_ref[...] = _dot(n, wq_ref[...]).astype(BF16)
    ng_ref[...] = _dot(n, wng_ref[...])
    gab_ref[...] = _dot(n, wgab_ref[...]).astype(BF16)
    pos = (pl.program_id(0) * tm) % s_len + _iota((tm, LANES), 0)
    lane = _iota((tm, LANES), 1)
    posc = _pair_columns(lane - LANE_POS, (pos // SEL_LEN).astype(F32), (pos % SEL_LEN).astype(F32))
    kv = _dot(n, wkv_ref[...])
    for c in range(kv.shape[1] // LANES):
        blk = kv[:, c * LANES:(c + 1) * LANES]
        if c < 2 * B_KV_GROUPS:
            blk = blk + posc
        kvb_ref[:, c * LANES:(c + 1) * LANES] = blk.astype(BF16)


def _proj(x2, g, wa, wc, wq, wkv, wng, wgab, wvt, *, s_len, tm=2 * SLC_TILE):
    t_len, d = x2.shape
    bsz = t_len // s_len
    nrt = s_len // tm
    const = lambda i: (0, 0)
    row = lambda i: (i, 0)
    ws = (wa, wc, wq, wkv, wng, wgab, wvt)
    flat = (wc, wq, wkv, wng, wgab)
    flat_dtypes = (F32, BF16, BF16, F32, BF16)
    aw = wa.shape[1]
    a_specs = [pl.BlockSpec((None, dil, tm // dil, aw), lambda i: (i // nrt, 0, i % nrt, 0))
               for _, dil in A_CONFIGS]
    a_shapes = [jax.ShapeDtypeStruct((bsz, dil, s_len // dil, aw), BF16) for _, dil in A_CONFIGS]
    return pl.pallas_call(
        functools.partial(_proj_body, tm=tm, s_len=s_len),
        grid=(t_len // tm,),
        in_specs=[pl.BlockSpec((tm, d), row), pl.BlockSpec((1, d), const)]
                 + [pl.BlockSpec(w.shape, const, pipeline_mode=pl.Buffered(1)) for w in ws],
        out_specs=a_specs + [pl.BlockSpec((tm, w.shape[1]), row) for w in flat]
                  + [pl.BlockSpec((None, tm // SLC_TILE, B_KV_GROUPS, VT_ROWS, SLC_TILE),
                                  lambda i: (i // nrt, i % nrt, 0, 0, 0))],
        out_shape=a_shapes + [jax.ShapeDtypeStruct((t_len, w.shape[1]), dt) for w, dt in zip(flat, flat_dtypes)]
                  + [jax.ShapeDtypeStruct((bsz, s_len // SLC_TILE, B_KV_GROUPS, VT_ROWS, SLC_TILE), BF16)],
        scratch_shapes=[pltpu.VMEM((aw // LANES, tm, LANES), F32)],
        compiler_params=pltpu.CompilerParams(dimension_semantics=("arbitrary",),
                                             vmem_limit_bytes=VMEM_LIMIT),
        name="proj",
    )(x2, g, *ws)


def _gelu_tanh(x):
    return 0.5 * x * (1.0 + jnp.tanh(np.sqrt(2.0 / np.pi).astype(np.float32) * (x + 0.044715 * (x * x * x))))


def _compress_body(ch_ref, pe_ref, w1_ref, w2_ref, out_ref, *, ncp):
    ch = ch_ref[...]
    xa = (ch + pe_ref[0:1, :]).astype(BF16)
    xb = (ch + pe_ref[1:2, :]).astype(BF16)
    a = _dot(xa, w1_ref[0])
    b = _dot(xb, w1_ref[1])
    pre = a + jnp.concatenate([b[1:], b[:1]], axis=0)
    hid = _gelu_tanh(pre).astype(BF16)
    out = _dot(hid, w2_ref[...])
    kv_is_key = pl.program_id(1) < B_KV_GROUPS
    n_idx = _iota((ncp, LANES), 0)
    lane = _iota((ncp, LANES), 1)
    nc = _pair_columns(lane - LANE_CMP, (n_idx // 16).astype(F32), (n_idx % 16).astype(F32))
    out = out + jnp.where(kv_is_key, nc, 0.0)
    out_ref[...] = out.astype(BF16)


def _compress(chunks, pe2, w1s, w2s):
    bsz, nslot, ncp, width = chunks.shape
    return pl.pallas_call(
        functools.partial(_compress_body, ncp=ncp),
        grid=(bsz, nslot),
        in_specs=[pl.BlockSpec((None, None, ncp, width), lambda b, s: (b, s, 0, 0)),
                  pl.BlockSpec((None, 2, width), lambda b, s: (s // B_KV_GROUPS, 0, 0)),
                  pl.BlockSpec((None, 2, width, CMP_HIDDEN), lambda b, s: (s // B_KV_GROUPS, 0, 0, 0)),
                  pl.BlockSpec((None, CMP_HIDDEN, LANES), lambda b, s: (s, 0, 0))],
        out_specs=pl.BlockSpec((None, None, ncp, LANES), lambda b, s: (b, s, 0, 0)),
        out_shape=jax.ShapeDtypeStruct((bsz, nslot, ncp, LANES), BF16),
        compiler_params=pltpu.CompilerParams(dimension_semantics=("arbitrary", "arbitrary"),
                                             vmem_limit_bytes=VMEM_LIMIT),
        name="compress",
    )(chunks, pe2, w1s, w2s)


def _head_slope(h, n_heads):
    out = jnp.float32(2.0 ** (-8.0 * n_heads / n_heads))
    for k in range(n_heads - 1):
        out = jnp.where(h == k, jnp.float32(2.0 ** (-8.0 * (k + 1) / n_heads)), out)
    return out


def _dilated_body(*refs):
    ncfg = len(A_CONFIGS)
    in_refs = refs[:5 * ncfg]
    out_ref = refs[5 * ncfg]
    o_sc, l_sc, m_sc, bias_sc = refs[5 * ncfg + 1:5 * ncfg + 5]
    kbufs = refs[5 * ncfg + 5:5 * ncfg + 5 + ncfg]
    vbufs = refs[5 * ncfg + 5 + ncfg:]
    hp = pl.program_id(1)
    first_tile = pl.program_id(2) == 0
    low = _iota((1, LANES), 1) < HEAD_DIM

    @pl.when(first_tile)
    def _():
        qi = _iota((Q_BLK, 2 * Q_BLK), 0)
        kj = _iota((Q_BLK, 2 * Q_BLK), 1)
        dist = qi - kj + Q_BLK
        for c, (window, dil) in enumerate(A_CONFIGS):
            valid = (dist >= 0) & (dist <= window // dil)
            for hh in range(2):
                slope = _head_slope(2 * hp + hh, A_HEADS)
                bias = jnp.where(valid, -(slope * dil * LOG2E) * dist.astype(F32), NEG)
                bias_sc[(c * 2 + hh) * 2] = bias
                bias_sc[(c * 2 + hh) * 2 + 1] = jnp.where(kj < Q_BLK, NEG, bias)

    for c, (window, dil) in enumerate(A_CONFIGS):
        q_ref, k_ref, v_ref, kh_ref, vh_ref = in_refs[5 * c:5 * c + 5]
        kbuf, vbuf = kbufs[c], vbufs[c]
        rows = A_TILE // dil
        nsub = rows // Q_BLK
        kbuf[:, :Q_BLK, :] = kh_ref[...]
        kbuf[:, Q_BLK:, :] = k_ref[...]
        vbuf[:, :Q_BLK, :] = vh_ref[...]
        vbuf[:, Q_BLK:, :] = v_ref[...]

        def group(gidx, carry, c=c, dil=dil, nsub=nsub, q_ref=q_ref, kbuf=kbuf, vbuf=vbuf):
            subs = []
            for u in range(A_GROUP):
                idx = gidx * A_GROUP + u
                r = idx // nsub
                j = idx % nsub
                j0 = pl.multiple_of(j * Q_BLK, Q_BLK)
                seq_start = ((j == 0) & first_tile).astype(jnp.int32)
                subs.append((r, j, j0, seq_start))
            scores = []
            for r, j, j0, seq_start in subs:
                q = q_ref[r, pl.ds(j0, Q_BLK), :]
                k2 = kbuf[r, pl.ds(j0, 2 * Q_BLK), :]
                for hh in range(2):
                    qm = jnp.where(low == (hh == 0), q, jnp.zeros_like(q))
                    scores.append(_dot_nt(qm, k2) + bias_sc[(c * 2 + hh) * 2 + seq_start])
            probs = []
            for s in scores:
                m = jnp.max(s, axis=-1, keepdims=True)
                e = jnp.exp2(s - m)
                probs.append((e.astype(BF16), m, jnp.sum(e, axis=-1, keepdims=True)))
            for u, (r, j, j0, seq_start) in enumerate(subs):
                v2 = vbuf[r, pl.ds(j0, 2 * Q_BLK), :]
                outs = [_dot(probs[2 * u + hh][0], v2) for hh in range(2)]
                stats = [[jnp.broadcast_to(probs[2 * u + hh][i], (Q_BLK, LANES)) for hh in range(2)] for i in (1, 2)]
                row0 = j * (Q_BLK * dil) + r
                dst = pl.ds(row0, Q_BLK, stride=dil) if dil > 1 else pl.ds(row0, Q_BLK)
                o_sc[c, dst, :] = jnp.where(low, outs[0], outs[1])
                m_sc[c, dst, :] = jnp.where(low, stats[0][0], stats[0][1])
                l_sc[c, dst, :] = jnp.where(low, stats[1][0], stats[1][1])
            return carry

        lax.fori_loop(0, dil * nsub // A_GROUP, group, 0)

    m = jnp.maximum(jnp.maximum(m_sc[0], m_sc[1]), m_sc[2])
    num = jnp.zeros((A_TILE, LANES), F32)
    den = jnp.zeros((A_TILE, LANES), F32)
    for c in range(ncfg):
        e = jnp.exp2(m_sc[c] - m)
        num = num + e * o_sc[c]
        den = den + e * l_sc[c]
    out_ref[...] = (num / den).astype(BF16)


def _dilated(qkv_by_cfg, *, bsz, s_len):
    nt = s_len // A_TILE
    npair = A_HEADS // 2
    ncfg = len(A_CONFIGS)
    in_specs, operands, kv_scratch = [], [], []
    for (window, dil), arr in zip(A_CONFIGS, qkv_by_cfg):
        rows = A_TILE // dil
        nsub = rows // Q_BLK
        cur = lambda off: (lambda b, h, t: (b, 0, t, off + h))
        halo = lambda off, nsub=nsub: (lambda b, h, t: (b, 0, jnp.maximum(t * nsub - 1, 0), off + h))
        in_specs += [pl.BlockSpec((None, dil, rows, LANES), cur(0)),
                     pl.BlockSpec((None, dil, rows, LANES), cur(npair)),
                     pl.BlockSpec((None, dil, rows, LANES), cur(2 * npair)),
                     pl.BlockSpec((None, dil, Q_BLK, LANES), halo(npair)),
                     pl.BlockSpec((None, dil, Q_BLK, LANES), halo(2 * npair))]
        operands += [arr] * 5
        kv_scratch.append(pltpu.VMEM((dil, Q_BLK + rows, LANES), BF16))
    return pl.pallas_call(
        _dilated_body,
        grid=(bsz, npair, nt),
        in_specs=in_specs,
        out_specs=pl.BlockSpec((A_TILE, LANES), lambda b, h, t: (b * nt + t, h)),
        out_shape=jax.ShapeDtypeStruct((bsz * s_len, A_HEADS * HEAD_DIM), BF16),
        scratch_shapes=[pltpu.VMEM((ncfg, A_TILE, LANES), F32)] * 3
                       + [pltpu.VMEM((ncfg * 4, Q_BLK, 2 * Q_BLK), F32)] + kv_scratch + kv_scratch,
        compiler_params=pltpu.CompilerParams(dimension_semantics=("arbitrary",) * 3,
                                             vmem_limit_bytes=VMEM_LIMIT),
        name="dilated",
    )(*operands)


def _q_aug(q_all, g, lane, low):
    rows = []
    for r in range(B_REP):
        h = g * B_REP + r
        blk = q_all[:, (h // 2) * LANES:(h // 2 + 1) * LANES]
        if h % 2 == 1:
            blk = pltpu.roll(blk, HEAD_DIM, axis=1)
        rows.append(jnp.where(low, blk, _query_alibi_row(2.0 ** (-8.0 * (h + 1) / B_HEADS), lane)))
    return jnp.concatenate(rows, axis=0).astype(BF16)


def _place_heads(per_head, low):
    placed = []
    for h, o in enumerate(per_head):
        placed.append(pltpu.roll(o, HEAD_DIM, axis=1) if h % 2 != h // B_REP else o)
    return [jnp.where(low, placed[2 * hp], placed[2 * hp + 1]) for hp in range(B_HEADS // 2)]


def _nsa_select_body(q_ref, cmp_ref, ng_ref, ovt_ref, gmap_ref, selb_ref, oc_ref, flags_ref, *, ncp, topk):
    q0 = pl.program_id(1) * Q_BLK
    lane = _iota((1, LANES), 1)
    low = lane < HEAD_DIM
    t_col = q0 + _iota((Q_BLK, 1), 0)
    t_row = q0 + _iota((1, Q_BLK), 1)
    nsel_pad = LANES
    groups = range(B_KV_GROUPS)

    cmp_end = CMP_STRIDE * _iota((1, ncp), 1) + (CMP_LEN - 1)
    mask_c = cmp_end <= t_col
    bias_c = jnp.where(mask_c, 0.0, NEG)
    blk_t = _iota((nsel_pad, 1), 0)
    allowed_t = blk_t * SEL_LEN <= t_row
    cur_t = t_row // SEL_LEN
    forced_t = (blk_t == 0) | (blk_t == cur_t) | (blk_t == cur_t - 1)
    blk_f = blk_t.astype(F32)

    q_all = q_ref[...].astype(F32)
    qg = [_q_aug(q_all, g, lane, low) for g in groups]
    scores = [_dot_nt(qg[g], cmp_ref[g]).reshape(B_REP, Q_BLK, ncp) + bias_c[None] for g in groups]
    probs = []
    for s in scores:
        m = jnp.max(s, axis=-1, keepdims=True)
        e = jnp.where(mask_c[None], jnp.exp2(s - m), 0.0)
        den = jnp.sum(e, axis=-1, keepdims=True)
        probs.append(e / jnp.maximum(den, 1e-30))
    o_cmp = [_dot(probs[g].reshape(B_REP * Q_BLK, ncp).astype(BF16), cmp_ref[B_KV_GROUPS + g]) for g in groups]
    ranks = []
    for p in probs:
        psum = p[0] + p[1] + p[2] + p[3]
        p_hi = psum.astype(BF16)
        p_lo = (psum - p_hi.astype(F32)).astype(BF16)
        imp_t = _dot_nt(ovt_ref[...], p_hi) + _dot_nt(ovt_ref[...], p_lo)
        ranks.append(jnp.where(allowed_t, imp_t + jnp.where(forced_t, FORCE, 0.0), NEG))

    def pick_one(_, carry):
        out = []
        for rank, sel in carry:
            best = jnp.max(rank, axis=0, keepdims=True)
            cand = jnp.where(rank == best, blk_f, float(nsel_pad))
            idx = jnp.min(cand, axis=0, keepdims=True)
            pick = blk_f == idx
            out.append((jnp.where(pick, -3e38, rank), jnp.where(pick, 1.0, sel)))
        return tuple(out)

    picked = lax.fori_loop(0, topk, pick_one,
                           tuple((rank, jnp.zeros((nsel_pad, Q_BLK), F32)) for rank in ranks))
    ones = jnp.ones((8, Q_BLK), BF16)
    tile_cnt = jnp.zeros((8, LANES), F32)
    for g in groups:
        sel = jnp.where(allowed_t, picked[g][1], 0.0).T
        selb_ref[:, g * LANES:(g + 1) * LANES] = jnp.where(sel > 0.5, 0.0, -MASK_BIG).astype(BF16)
        per_block = _dot(ones, sel.astype(BF16))
        tile_cnt = tile_cnt + _dot(per_block.astype(BF16), gmap_ref[g])
    flags_ref[...] = (tile_cnt > 0.5).astype(jnp.int32)

    sig = jax.nn.sigmoid(ng_ref[...])
    per_head = []
    for h in range(B_HEADS):
        g, r = divmod(h, B_REP)
        per_head.append(sig[:, 3 * h:3 * h + 1] * o_cmp[g][r * Q_BLK:(r + 1) * Q_BLK])
    for hp, tile in enumerate(_place_heads(per_head, low)):
        oc_ref[:, hp * LANES:(hp + 1) * LANES] = tile


def _nsa_attend_body(lists_ref, cnts_ref, q_ref, ks0_ref, ks1_ref, kw0_ref, kw1_ref, vw_ref, vst_ref,
                     selb_ref, oc_ref, ng_ref, oh_ref, out_ref, *, nq, ntile):
    step = pl.program_id(0) * nq + pl.program_id(1)
    q0 = pl.program_id(1) * Q_BLK
    lane = _iota((1, LANES), 1)
    low = lane < HEAD_DIM
    t_col = q0 + _iota((Q_BLK, 1), 0)
    t_row = q0 + _iota((1, Q_BLK), 1)
    win_keys = WIN + Q_BLK
    rows = B_REP * Q_BLK
    groups = range(B_KV_GROUPS)
    ks_refs = (ks0_ref, ks1_ref)
    kw_refs = (kw0_ref, kw1_ref)

    q_all = q_ref[...].astype(F32)
    qg = [_q_aug(q_all, g, lane, low) for g in groups]

    def softmax_step(s, m_i):
        m_new = jnp.maximum(m_i, jnp.max(s, axis=0, keepdims=True))
        return jnp.exp2(s - m_new).astype(BF16), jnp.exp2(m_i - m_new), m_new

    q_slc = []
    for g in groups:
        sel_bias = selb_ref[:, g * LANES:(g + 1) * LANES]
        q_slc.append(jnp.concatenate([qg[g], jnp.concatenate([sel_bias] * B_REP, axis=0)], axis=1))

    def qk(g, kt):
        k0 = pl.multiple_of(jnp.minimum(kt, ntile - 1) * SLC_TILE, SLC_TILE)
        o0 = pl.multiple_of(kt * SLC_TILE, SLC_TILE)
        k_aug = jnp.concatenate([ks_refs[g][pl.ds(k0, SLC_TILE), :], oh_ref[pl.ds(o0, SLC_TILE), :]], axis=1)
        return _dot_nt(k_aug, q_slc[g])

    def pv(g, p, kt):
        return _dot(vst_ref[jnp.minimum(kt, ntile - 1), g], p)

    def trip(k, carry):
        work = [(g, lists_ref[(step * B_KV_GROUPS + g) * LIST_W + k * SLC_TRIP + u])
                for u in range(SLC_TRIP) for g in groups]
        scores = [qk(g, kt) for g, kt in work]
        state = list(carry)
        for (g, kt), s in zip(work, scores):
            m_i, acc = state[g]
            p, a, m_i = softmax_step(s, m_i)
            state[g] = (m_i, a * acc + pv(g, p, kt))
        return tuple(state)

    n_trips = jnp.maximum(cnts_ref[step * B_KV_GROUPS], cnts_ref[step * B_KV_GROUPS + 1]) // SLC_TRIP
    init = tuple((jnp.full((1, rows), -3e38, F32), jnp.zeros((VT_ROWS, rows), F32)) for g in groups)
    state = lax.fori_loop(0, n_trips, trip, init)

    kt_diag = q0 // SLC_TILE
    key_pos = kt_diag * SLC_TILE + _iota((SLC_TILE, 1), 0)
    causal = jnp.where(key_pos <= t_row, 0.0, NEG)
    causal = jnp.concatenate([causal] * B_REP, axis=1)
    scores = [qk(g, kt_diag) + causal for g in groups]
    o_slc = []
    for g in groups:
        m_i, acc = state[g]
        p, a, m_i = softmax_step(scores[g], m_i)
        acc = a * acc + pv(g, p, kt_diag)
        o_t = acc[:HEAD_DIM] / acc[HEAD_DIM:HEAD_DIM + 1]
        o_t = jnp.concatenate([o_t, o_t], axis=0)
        o_slc.append(jnp.concatenate([o_t[:, r * Q_BLK:(r + 1) * Q_BLK].T for r in range(B_REP)], axis=0))

    kstart = pl.multiple_of(jnp.maximum(q0 - WIN, 0), Q_BLK)
    d_w = t_col - (kstart + _iota((1, win_keys), 1))
    bias_w = jnp.where((d_w >= 0) & (d_w < WIN), 0.0, NEG)
    scores = [_dot_nt(qg[g], kw_refs[g][pl.ds(kstart, win_keys), :]).reshape(B_REP, Q_BLK, win_keys) + bias_w[None]
              for g in groups]
    o_win = []
    for s in scores:
        m = jnp.max(s, axis=-1, keepdims=True)
        e = jnp.exp2(s - m)
        den = jnp.sum(e, axis=-1, keepdims=True)
        o = _dot(e.reshape(rows, win_keys).astype(BF16), vw_ref[pl.ds(kstart, win_keys), :])
        o_win.append(o / den.reshape(rows, 1))

    sig = jax.nn.sigmoid(ng_ref[...])
    per_head = []
    for h in range(B_HEADS):
        g, r = divmod(h, B_REP)
        rs = slice(r * Q_BLK, (r + 1) * Q_BLK)
        per_head.append(sig[:, 3 * h + 1:3 * h + 2] * o_slc[g][rs] + sig[:, 3 * h + 2:3 * h + 3] * o_win[g][rs])
    for hp, tile in enumerate(_place_heads(per_head, low)):
        cols = slice(hp * LANES, (hp + 1) * LANES)
        out_ref[:, cols] = (oc_ref[:, cols] + tile).astype(BF16)


def _tile_lists(flags, *, bsz, s_len):
    nq = s_len // Q_BLK
    ntile = s_len // SLC_TILE
    f = flags[:, 0, :B_KV_GROUPS * FLAG_W].reshape(bsz * nq, B_KV_GROUPS, FLAG_W)[:, :, :ntile] > 0
    kt = jnp.arange(ntile, dtype=jnp.int32)
    diag = jnp.tile((jnp.arange(nq, dtype=jnp.int32) * Q_BLK) // SLC_TILE, bsz)[:, None, None]
    touched = f & (kt < diag)
    rank = jnp.cumsum(touched, axis=-1, dtype=jnp.int32) - 1
    n_touched = rank[..., -1] + 1
    cnt = SLC_TRIP * ((n_touched + SLC_TRIP - 1) // SLC_TRIP)
    pos = jnp.arange(LIST_W, dtype=jnp.int32)
    hit = touched[..., None, :] & (rank[..., None, :] == pos[:, None])
    order = jnp.sum(jnp.where(hit, kt, 0), axis=-1)
    lists = jnp.where(pos < n_touched[..., None], order, ntile)
    return lists.reshape(-1), cnt.reshape(-1)


def _nsa2(qb, kvb, vst, cmp, ng, onehot, ovt, gmap, *, bsz, s_len):
    nq = s_len // Q_BLK
    ntile = s_len // SLC_TILE
    ncp = cmp.shape[2]
    topk = min(SEL_TOPK, s_len // SEL_LEN)
    hw = B_HEADS * HEAD_DIM
    rowblk = lambda b, i: (b * nq + i, 0)
    params = pltpu.CompilerParams(dimension_semantics=("arbitrary", "arbitrary"), vmem_limit_bytes=VMEM_LIMIT)
    selb, oc, flags = pl.pallas_call(
        functools.partial(_nsa_select_body, ncp=ncp, topk=topk),
        grid=(bsz, nq),
        in_specs=[pl.BlockSpec((Q_BLK, hw), rowblk),
                  pl.BlockSpec((None, 2 * B_KV_GROUPS, ncp, LANES), lambda b, i: (b, 0, 0, 0)),
                  pl.BlockSpec((Q_BLK, LANES), rowblk),
                  pl.BlockSpec(ovt.shape, lambda b, i: (0, 0)),
                  pl.BlockSpec(gmap.shape, lambda b, i: (0, 0, 0))],
        out_specs=[pl.BlockSpec((Q_BLK, B_KV_GROUPS * LANES), rowblk),
                   pl.BlockSpec((Q_BLK, hw), rowblk),
                   pl.BlockSpec((None, 8, LANES), lambda b, i: (b * nq + i, 0, 0))],
        out_shape=[jax.ShapeDtypeStruct((bsz * s_len, B_KV_GROUPS * LANES), BF16),
                   jax.ShapeDtypeStruct((bsz * s_len, hw), F32),
                   jax.ShapeDtypeStruct((bsz * nq, 8, LANES), jnp.int32)],
        compiler_params=params,
        name="nsa_select",
    )(qb, cmp, ng, ovt, gmap)

    lists, cnts = _tile_lists(flags, bsz=bsz, s_len=s_len)
    rowblk2 = lambda b, i, lists, cnts: (b * nq + i, 0)
    res = lambda col: (lambda b, i, lists, cnts: (b, col))
    return pl.pallas_call(
        functools.partial(_nsa_attend_body, nq=nq, ntile=ntile),
        grid_spec=pltpu.PrefetchScalarGridSpec(
            num_scalar_prefetch=2,
            grid=(bsz, nq),
            in_specs=[pl.BlockSpec((Q_BLK, hw), rowblk2)]
                     + [pl.BlockSpec((s_len, LANES), res(col)) for col in range(5)]
                     + [pl.BlockSpec((None, ntile, B_KV_GROUPS, VT_ROWS, SLC_TILE),
                                     lambda b, i, lists, cnts: (b, 0, 0, 0, 0)),
                        pl.BlockSpec((Q_BLK, B_KV_GROUPS * LANES), rowblk2),
                        pl.BlockSpec((Q_BLK, hw), rowblk2),
                        pl.BlockSpec((Q_BLK, LANES), rowblk2),
                        pl.BlockSpec(onehot.shape, lambda b, i, lists, cnts: (0, 0))],
            out_specs=pl.BlockSpec((Q_BLK, hw), rowblk2)),
        out_shape=jax.ShapeDtypeStruct((bsz * s_len, hw), BF16),
        compiler_params=params,
        name="nsa_attend",
    )(lists, cnts, qb, kvb, kvb, kvb, kvb, kvb, vst, selb, oc, ng, onehot)


def _post_body(x_ref, oa_ref, ob_ref, gab_ref, p_ref, wua_ref, wub_ref, wout_ref, g2_ref, w1_ref, w2_ref,
               g3_ref, wpg_ref, wple_ref, gf_ref, out_ref, *, d, ff_chunk):
    ya = _dot(oa_ref[...], wua_ref[...])
    yb = _dot(ob_ref[...], wub_ref[...])
    mixed = (jax.nn.sigmoid(gab_ref[:, :d].astype(F32)) * ya
             + jax.nn.sigmoid(gab_ref[:, d:].astype(F32)) * yb)
    h = x_ref[...] + _dot(mixed.astype(BF16), wout_ref[...])
    n2 = _rms(h, g2_ref[...]).astype(BF16)
    acc = h
    for c in range(w1_ref.shape[1] // ff_chunk):
        cs = slice(c * ff_chunk, (c + 1) * ff_chunk)
        hid = jnp.square(jnp.maximum(_dot(n2, w1_ref[:, cs]), 0.0))
        acc = acc + _dot(hid.astype(BF16), w2_ref[cs, :])
    n3 = _rms(acc, g3_ref[...]).astype(BF16)
    gate = jax.nn.sigmoid(_dot(n3, wpg_ref[...]))
    h3 = acc + gate * _dot(p_ref[...].astype(BF16), wple_ref[...])
    out_ref[...] = _rms(h3, gf_ref[...])


def _post(x2, oa, ob, gab, p2, wua, wub, wout, g2, w1, w2, g3, wpg, wple, gf, *, tm=512, ff_chunk=1024):
    t_len, d = x2.shape
    row = lambda i: (i, 0)
    const = lambda i: (0, 0)
    resident = lambda a: pl.BlockSpec(a.shape, const, pipeline_mode=pl.Buffered(1))
    acts = (x2, oa, ob, gab, p2)
    params = (wua, wub, wout, g2, w1, w2, g3, wpg, wple, gf)
    return pl.pallas_call(
        functools.partial(_post_body, d=d, ff_chunk=ff_chunk),
        grid=(t_len // tm,),
        in_specs=[pl.BlockSpec((tm, a.shape[1]), row) for a in acts] + [resident(w) for w in params],
        out_specs=pl.BlockSpec((tm, d), row),
        out_shape=jax.ShapeDtypeStruct((t_len, d), F32),
        compiler_params=pltpu.CompilerParams(dimension_semantics=("arbitrary",),
                                             vmem_limit_bytes=VMEM_LIMIT),
        name="post",
    )(*acts, *params)


def _selection_overlap_t(ncp, s_len):
    ncmp = (s_len - CMP_LEN) // CMP_STRIDE + 1
    nsel = s_len // SEL_LEN
    ratio = SEL_LEN // CMP_STRIDE
    span = CMP_LEN // CMP_STRIDE
    i = np.arange(ncmp)[:, None]
    j = np.arange(nsel)[None, :]
    ov = np.maximum(np.minimum(i + span, ratio * (j + 1)) - np.maximum(i, ratio * j), 0)
    out = np.zeros((LANES, ncp), np.float32)
    out[:nsel, :ncmp] = ov.T
    return out


def _layer(h, p_i, norm_mix_g, w_in, pe_ck, w_ck1, w_ck2, pe_cv, w_cv1, w_cv2, w_up_a, w_up_b, w_out,
           norm_mlp_g, w_mlp1, w_mlp2, norm_ple_g, w_ple_gate, w_ple, final_g):
    bsz, s_len, d = h.shape
    t_len = bsz * s_len
    aw = A_HEADS * HEAD_DIM
    bw = B_HEADS * HEAD_DIM
    kvw = B_KV_GROUPS * HEAD_DIM
    assert s_len % A_TILE == 0 and s_len // SEL_LEN <= LANES and kvw == LANES

    o_qb = 3 * aw
    o_kv = o_qb + bw
    o_ng = o_kv + 6 * kvw
    o_ga = o_ng + 3 * B_HEADS
    kv = lambda i: w_in[:, o_kv + i * kvw:o_kv + (i + 1) * kvw]
    zeros_h = jnp.zeros((d, HEAD_DIM), w_in.dtype)
    grp = lambda w, g: jnp.concatenate([w[:, g * HEAD_DIM:(g + 1) * HEAD_DIM], zeros_h], axis=1)
    wa = jnp.concatenate([w_in[:, :aw] * (SCALE * LOG2E), w_in[:, aw:3 * aw]], axis=1)
    wc = jnp.concatenate([kv(0), kv(1)], axis=1)
    wq = w_in[:, o_qb:o_qb + bw] * (SCALE * LOG2E)
    wkv = jnp.concatenate([grp(kv(2), 0), grp(kv(2), 1), grp(kv(4), 0), grp(kv(4), 1), kv(5)], axis=1)
    wvt = kv(3).T
    wng = jnp.concatenate([w_in[:, o_ng:o_ga], jnp.zeros((d, LANES - 3 * B_HEADS), w_in.dtype)], axis=1)
    wgab = w_in[:, o_ga:]
    x2 = h.reshape(t_len, d)
    a0, a1, a2, kvc, qb, kvb, ng, gab, vst = _proj(x2, norm_mix_g.reshape(1, d), *(w.astype(BF16) for w in
                                                   (wa, wc, wq, wkv, wng, wgab, wvt)), s_len=s_len)

    ncp = s_len // CMP_STRIDE
    half = CMP_STRIDE * HEAD_DIM
    chunks = kvc.reshape(bsz, ncp, CMP_STRIDE, 2 * B_KV_GROUPS, HEAD_DIM).transpose(0, 3, 1, 2, 4)
    chunks = chunks.reshape(bsz, 2 * B_KV_GROUPS, ncp, half)
    pe2 = jnp.stack([pe_ck.reshape(2, half), pe_cv.reshape(2, half)])
    w1s = jnp.stack([w_ck1.reshape(2, half, CMP_HIDDEN), w_cv1.reshape(2, half, CMP_HIDDEN)]).astype(BF16)
    zpad = jnp.zeros((CMP_HIDDEN, HEAD_DIM), w_ck2.dtype)
    w2s = jnp.stack([jnp.concatenate([w_ck2, zpad], axis=1), jnp.concatenate([w_ck2, zpad], axis=1),
                     jnp.concatenate([w_cv2, zpad], axis=1), jnp.concatenate([zpad, w_cv2], axis=1)]).astype(BF16)
    cmp = _compress(chunks, pe2, w1s, w2s)

    oa = _dilated((a0, a1, a2), bsz=bsz, s_len=s_len)
    onehot = (np.arange(s_len + SLC_TILE)[:, None] // SEL_LEN == np.arange(LANES)[None, :])
    onehot[s_len:] = True
    blocks_per_tile = SLC_TILE // SEL_LEN
    gmap = np.zeros((B_KV_GROUPS, LANES, LANES), np.float32)
    for g in range(B_KV_GROUPS):
        gmap[g, np.arange(LANES), FLAG_W * g + np.arange(LANES) // blocks_per_tile] = 1.0
    ob = _nsa2(qb, kvb, vst, cmp, ng, jnp.asarray(onehot, BF16),
               jnp.asarray(_selection_overlap_t(ncp, s_len), BF16), jnp.asarray(gmap, BF16), bsz=bsz, s_len=s_len)

    b16 = lambda w: w.astype(BF16)
    row = lambda v: v.reshape(1, d)
    return _post(x2, oa, ob, gab, p_i.reshape(t_len, -1), b16(w_up_a), b16(w_up_b), b16(w_out), row(norm_mlp_g),
                 b16(w_mlp1), b16(w_mlp2), row(norm_ple_g), b16(w_ple_gate), b16(w_ple), row(final_g)
                 ).reshape(bsz, s_len, d)


def kernel(x, p, norm_mix_g, w_in, pe_ck, w_ck1, w_ck2, pe_cv, w_cv1, w_cv2, w_up_a, w_up_b, w_out,
           norm_mlp_g, w_mlp1, w_mlp2, norm_ple_g, w_ple_gate, w_ple, norm_final_g):
    depth = w_in.shape[0]
    assert depth == 1, "the fused tail applies the final norm inside the single layer"
    return _layer(x, p[0], norm_mix_g[0], w_in[0], pe_ck[0], w_ck1[0], w_ck2[0], pe_cv[0], w_cv1[0], w_cv2[0],
                  w_up_a[0], w_up_b[0], w_out[0], norm_mlp_g[0], w_mlp1[0], w_mlp2[0], norm_ple_g[0],
                  w_ple_gate[0], w_ple[0], norm_final_g)
```

```python
import functools

import numpy as np
import jax
import jax.numpy as jnp
from jax import lax
from jax.experimental import pallas as pl
from jax.experimental.pallas import tpu as pltpu

HEAD_DIM = 64
A_HEADS = 8
A_CONFIGS = ((128, 1), (512, 4), (2048, 16))
B_HEADS = 8
B_KV_GROUPS = 2
B_REP = B_HEADS // B_KV_GROUPS
CMP_LEN = 32
CMP_STRIDE = 16
CMP_HIDDEN = 256
SEL_LEN = 64
SEL_TOPK = 16
WIN = 512
Q_BLK = 128
EPS = 1e-6
NEG = -1e30
FORCE = 1e9
MASK_BIG = 2.0 ** 100
SCALE = HEAD_DIM ** -0.5

LANES = 128
A_TILE = 2048
A_GROUP = 8
SLC_TILE = 256
VT_ROWS = HEAD_DIM + 16
FLAG_W = 32
LIST_W = FLAG_W
SLC_TRIP = 2
VMEM_LIMIT = 56 * 1024 * 1024

ALIBI_TERMS = 3
LANE_POS = 64
LANE_CMP = LANE_POS + 2 * ALIBI_TERMS
LOG2E = 1.4426950408889634

F32 = jnp.float32
BF16 = jnp.bfloat16


def _dot(a, b):
    return jnp.dot(a, b, preferred_element_type=F32)


def _dot_nt(a, b):
    return lax.dot_general(a, b, (((1,), (1,)), ((), ())), preferred_element_type=F32)


def _rms(x, g):
    inv = lax.rsqrt(jnp.mean(x * x, axis=-1, keepdims=True) + EPS)
    return (x * inv) * g


def _iota(shape, dim, dtype=jnp.int32):
    return lax.broadcasted_iota(dtype, shape, dim)


def _pair_columns(rel_lane, even_val, odd_val):
    inside = (rel_lane >= 0) & (rel_lane < 2 * ALIBI_TERMS)
    return jnp.where(inside, jnp.where(rel_lane % 2 == 0, even_val, odd_val), 0.0)


def _bf16_pieces(x):
    pieces, rest = [], np.float64(x)
    for _ in range(ALIBI_TERMS):
        piece = np.float64(np.asarray(rest, np.float32).astype(jnp.bfloat16).astype(np.float32))
        pieces.append(float(piece))
        rest = rest - piece
    return pieces


def _query_alibi_row(slope, lane):
    row = jnp.zeros(lane.shape, F32)
    coeffs = ((LANE_POS, SEL_LEN * slope), (LANE_POS + 1, slope),
              (LANE_CMP, 16 * CMP_STRIDE * slope), (LANE_CMP + 1, CMP_STRIDE * slope))
    for lane0, coeff in coeffs:
        for t, piece in enumerate(_bf16_pieces(coeff * LOG2E)):
            row = jnp.where(lane == lane0 + 2 * t, piece, row)
    return row


def _proj_body(x_ref, g_ref, wa_ref, wc_ref, wq_ref, wkv_ref, wng_ref, wgab_ref, wvt_ref,
               a0_ref, a1_ref, a2_ref, kvc_ref, qb_ref, kvb_ref, ng_ref, gab_ref, vst_ref, vwt_ref, res_sc,
               *, tm, s_len):
    n = _rms(x_ref[...], g_ref[...]).astype(BF16)
    vt = _dot_nt(wvt_ref[...], n)
    for branch, (out_ref, width) in enumerate(((vst_ref, SLC_TILE), (vwt_ref, Q_BLK))):
        for u in range(tm // width):
            for g in range(B_KV_GROUPS):
                r0 = (branch * B_KV_GROUPS + g) * HEAD_DIM
                out_ref[u, g, :HEAD_DIM, :] = vt[r0:r0 + HEAD_DIM, u * width:(u + 1) * width].astype(BF16)
                out_ref[u, g, HEAD_DIM:, :] = jnp.ones((VT_ROWS - HEAD_DIM, width), BF16)
    res = _dot(n, wa_ref[...])
    a_refs = (a0_ref, a1_ref, a2_ref)
    for s in range(res.shape[1] // LANES):
        cols = slice(s * LANES, (s + 1) * LANES)
        res_sc[s] = res[:, cols]
        for (window, dil), a_ref in zip(A_CONFIGS, a_refs):
            for r in range(dil):
                src = pl.ds(r, tm // dil, stride=dil) if dil > 1 else pl.ds(0, tm)
                a_ref[r, :, cols] = res_sc[s, src, :].astype(BF16)
    kvc_ref[...] = _dot(n, wc_ref[...])
    qb_ref[...] = _dot(n, wq_ref[...]).astype(BF16)
    ng_ref[...] = _dot(n, wng_ref[...])
    gab_ref[...] = _dot(n, wgab_ref[...]).astype(BF16)
    pos = (pl.program_id(0) * tm) % s_len + _iota((tm, LANES), 0)
    lane = _iota((tm, LANES), 1)
    posc = _pair_columns(lane - LANE_POS, (pos // SEL_LEN).astype(F32), (pos % SEL_LEN).astype(F32))
    kv = _dot(n, wkv_ref[...])
    for c in range(kv.shape[1] // LANES):
        kvb_ref[:, c * LANES:(c + 1) * LANES] = (kv[:, c * LANES:(c + 1) * LANES] + posc).astype(BF16)


def _proj(x2, g, wa, wc, wq, wkv, wng, wgab, wvt, *, s_len, tm=2 * SLC_TILE):
    t_len, d = x2.shape
    bsz = t_len // s_len
    nrt = s_len // tm
    const = lambda i: (0, 0)
    row = lambda i: (i, 0)
    ws = (wa, wc, wq, wkv, wng, wgab, wvt)
    flat = (wc, wq, wkv, wng, wgab)
    flat_dtypes = (F32, BF16, BF16, F32, BF16)
    aw = wa.shape[1]
    a_specs = [pl.BlockSpec((None, dil, tm // dil, aw), lambda i: (i // nrt, 0, i % nrt, 0))
               for _, dil in A_CONFIGS]
    a_shapes = [jax.ShapeDtypeStruct((bsz, dil, s_len // dil, aw), BF16) for _, dil in A_CONFIGS]
    return pl.pallas_call(
        functools.partial(_proj_body, tm=tm, s_len=s_len),
        grid=(t_len // tm,),
        in_specs=[pl.BlockSpec((tm, d), row), pl.BlockSpec((1, d), const)]
                 + [pl.BlockSpec(w.shape, const, pipeline_mode=pl.Buffered(1)) for w in ws],
        out_specs=a_specs + [pl.BlockSpec((tm, w.shape[1]), row) for w in flat]
                  + [pl.BlockSpec((None, tm // width, B_KV_GROUPS, VT_ROWS, width),
                                  lambda i: (i // nrt, i % nrt, 0, 0, 0)) for width in (SLC_TILE, Q_BLK)],
        out_shape=a_shapes + [jax.ShapeDtypeStruct((t_len, w.shape[1]), dt) for w, dt in zip(flat, flat_dtypes)]
                  + [jax.ShapeDtypeStruct((bsz, s_len // width, B_KV_GROUPS, VT_ROWS, width), BF16)
                     for width in (SLC_TILE, Q_BLK)],
        scratch_shapes=[pltpu.VMEM((aw // LANES, tm, LANES), F32)],
        compiler_params=pltpu.CompilerParams(dimension_semantics=("arbitrary",),
                                             vmem_limit_bytes=VMEM_LIMIT),
        name="proj",
    )(x2, g, *ws)


def _gelu_tanh(x):
    return 0.5 * x * (1.0 + jnp.tanh(np.sqrt(2.0 / np.pi).astype(np.float32) * (x + 0.044715 * (x * x * x))))


def _compress_body(ch_ref, pe_ref, w1_ref, w2_ref, out_ref, *, ncp):
    ch = ch_ref[...]
    xa = (ch + pe_ref[0:1, :]).astype(BF16)
    xb = (ch + pe_ref[1:2, :]).astype(BF16)
    a = _dot(xa, w1_ref[0])
    b = _dot(xb, w1_ref[1])
    pre = a + jnp.concatenate([b[1:], b[:1]], axis=0)
    hid = _gelu_tanh(pre).astype(BF16)
    out = _dot(hid, w2_ref[...])
    kv_is_key = pl.program_id(1) < B_KV_GROUPS
    n_idx = _iota((ncp, LANES), 0)
    lane = _iota((ncp, LANES), 1)
    nc = _pair_columns(lane - LANE_CMP, (n_idx // 16).astype(F32), (n_idx % 16).astype(F32))
    out = out + jnp.where(kv_is_key, nc, 0.0)
    out_ref[...] = out.astype(BF16)


def _compress(chunks, pe2, w1s, w2s):
    bsz, nslot, ncp, width = chunks.shape
    return pl.pallas_call(
        functools.partial(_compress_body, ncp=ncp),
        grid=(bsz, nslot),
        in_specs=[pl.BlockSpec((None, None, ncp, width), lambda b, s: (b, s, 0, 0)),
                  pl.BlockSpec((None, 2, width), lambda b, s: (s // B_KV_GROUPS, 0, 0)),
                  pl.BlockSpec((None, 2, width, CMP_HIDDEN), lambda b, s: (s // B_KV_GROUPS, 0, 0, 0)),
                  pl.BlockSpec((None, CMP_HIDDEN, LANES), lambda b, s: (s, 0, 0))],
        out_specs=pl.BlockSpec((None, None, ncp, LANES), lambda b, s: (b, s, 0, 0)),
        out_shape=jax.ShapeDtypeStruct((bsz, nslot, ncp, LANES), BF16),
        compiler_params=pltpu.CompilerParams(dimension_semantics=("arbitrary", "arbitrary"),
                                             vmem_limit_bytes=VMEM_LIMIT),
        name="compress",
    )(chunks, pe2, w1s, w2s)


def _head_slope(h, n_heads):
    out = jnp.float32(2.0 ** (-8.0 * n_heads / n_heads))
    for k in range(n_heads - 1):
        out = jnp.where(h == k, jnp.float32(2.0 ** (-8.0 * (k + 1) / n_heads)), out)
    return out


def _dilated_body(*refs):
    ncfg = len(A_CONFIGS)
    in_refs = refs[:5 * ncfg]
    out_ref = refs[5 * ncfg]
    o_sc, l_sc, m_sc, bias_sc = refs[5 * ncfg + 1:5 * ncfg + 5]
    kbufs = refs[5 * ncfg + 5:5 * ncfg + 5 + ncfg]
    vbufs = refs[5 * ncfg + 5 + ncfg:]
    hp = pl.program_id(1)
    first_tile = pl.program_id(2) == 0
    low = _iota((1, LANES), 1) < HEAD_DIM

    @pl.when(first_tile)
    def _():
        qi = _iota((Q_BLK, 2 * Q_BLK), 0)
        kj = _iota((Q_BLK, 2 * Q_BLK), 1)
        dist = qi - kj + Q_BLK
        for c, (window, dil) in enumerate(A_CONFIGS):
            valid = (dist >= 0) & (dist <= window // dil)
            for hh in range(2):
                slope = _head_slope(2 * hp + hh, A_HEADS)
                bias = jnp.where(valid, -(slope * dil * LOG2E) * dist.astype(F32), NEG)
                bias_sc[(c * 2 + hh) * 2] = bias
                bias_sc[(c * 2 + hh) * 2 + 1] = jnp.where(kj < Q_BLK, NEG, bias)

    for c, (window, dil) in enumerate(A_CONFIGS):
        q_ref, k_ref, v_ref, kh_ref, vh_ref = in_refs[5 * c:5 * c + 5]
        kbuf, vbuf = kbufs[c], vbufs[c]
        rows = A_TILE // dil
        nsub = rows // Q_BLK
        kbuf[:, :Q_BLK, :] = kh_ref[...]
        kbuf[:, Q_BLK:, :] = k_ref[...]
        vbuf[:, :Q_BLK, :] = vh_ref[...]
        vbuf[:, Q_BLK:, :] = v_ref[...]

        def group(gidx, carry, c=c, dil=dil, nsub=nsub, q_ref=q_ref, kbuf=kbuf, vbuf=vbuf):
            subs = []
            for u in range(A_GROUP):
                idx = gidx * A_GROUP + u
                r = idx // nsub
                j = idx % nsub
                j0 = pl.multiple_of(j * Q_BLK, Q_BLK)
                seq_start = ((j == 0) & first_tile).astype(jnp.int32)
                subs.append((r, j, j0, seq_start))
            scores = []
            for r, j, j0, seq_start in subs:
                q = q_ref[r, pl.ds(j0, Q_BLK), :]
                k2 = kbuf[r, pl.ds(j0, 2 * Q_BLK), :]
                for hh in range(2):
                    qm = jnp.where(low == (hh == 0), q, jnp.zeros_like(q))
                    scores.append(_dot_nt(qm, k2) + bias_sc[(c * 2 + hh) * 2 + seq_start])
            probs = []
            for s in scores:
                m = jnp.max(s, axis=-1, keepdims=True)
                e = jnp.exp2(s - m)
                probs.append((e.astype(BF16), m, jnp.sum(e, axis=-1, keepdims=True)))
            for u, (r, j, j0, seq_start) in enumerate(subs):
                v2 = vbuf[r, pl.ds(j0, 2 * Q_BLK), :]
                outs = [_dot(probs[2 * u + hh][0], v2) for hh in range(2)]
                stats = [[jnp.broadcast_to(probs[2 * u + hh][i], (Q_BLK, LANES)) for hh in range(2)] for i in (1, 2)]
                row0 = j * (Q_BLK * dil) + r
                dst = pl.ds(row0, Q_BLK, stride=dil) if dil > 1 else pl.ds(row0, Q_BLK)
                o_sc[c, dst, :] = jnp.where(low, outs[0], outs[1])
                m_sc[c, dst, :] = jnp.where(low, stats[0][0], stats[0][1])
                l_sc[c, dst, :] = jnp.where(low, stats[1][0], stats[1][1])
            return carry

        lax.fori_loop(0, dil * nsub // A_GROUP, group, 0)

    m = jnp.maximum(jnp.maximum(m_sc[0], m_sc[1]), m_sc[2])
    num = jnp.zeros((A_TILE, LANES), F32)
    den = jnp.zeros((A_TILE, LANES), F32)
    for c in range(ncfg):
        e = jnp.exp2(m_sc[c] - m)
        num = num + e * o_sc[c]
        den = den + e * l_sc[c]
    out_ref[...] = (num / den).astype(BF16)


def _dilated(qkv_by_cfg, *, bsz, s_len):
    nt = s_len // A_TILE
    npair = A_HEADS // 2
    ncfg = len(A_CONFIGS)
    in_specs, operands, kv_scratch = [], [], []
    for (window, dil), arr in zip(A_CONFIGS, qkv_by_cfg):
        rows = A_TILE // dil
        nsub = rows // Q_BLK
        cur = lambda off: (lambda b, h, t: (b, 0, t, off + h))
        halo = lambda off, nsub=nsub: (lambda b, h, t: (b, 0, jnp.maximum(t * nsub - 1, 0), off + h))
        in_specs += [pl.BlockSpec((None, dil, rows, LANES), cur(0)),
                     pl.BlockSpec((None, dil, rows, LANES), cur(npair)),
                     pl.BlockSpec((None, dil, rows, LANES), cur(2 * npair)),
                     pl.BlockSpec((None, dil, Q_BLK, LANES), halo(npair)),
                     pl.BlockSpec((None, dil, Q_BLK, LANES), halo(2 * npair))]
        operands += [arr] * 5
        kv_scratch.append(pltpu.VMEM((dil, Q_BLK + rows, LANES), BF16))
    return pl.pallas_call(
        _dilated_body,
        grid=(bsz, npair, nt),
        in_specs=in_specs,
        out_specs=pl.BlockSpec((A_TILE, LANES), lambda b, h, t: (b * nt + t, h)),
        out_shape=jax.ShapeDtypeStruct((bsz * s_len, A_HEADS * HEAD_DIM), BF16),
        scratch_shapes=[pltpu.VMEM((ncfg, A_TILE, LANES), F32)] * 3
                       + [pltpu.VMEM((ncfg * 4, Q_BLK, 2 * Q_BLK), F32)] + kv_scratch + kv_scratch,
        compiler_params=pltpu.CompilerParams(dimension_semantics=("arbitrary",) * 3,
                                             vmem_limit_bytes=VMEM_LIMIT),
        name="dilated",
    )(*operands)


def _q_aug(q_all, g, lane, low):
    rows = []
    for r in range(B_REP):
        h = g * B_REP + r
        blk = q_all[:, (h // 2) * LANES:(h // 2 + 1) * LANES]
        if h % 2 == 1:
            blk = pltpu.roll(blk, HEAD_DIM, axis=1)
        rows.append(jnp.where(low, blk, _query_alibi_row(2.0 ** (-8.0 * (h + 1) / B_HEADS), lane)))
    return jnp.concatenate(rows, axis=0).astype(BF16)


def _place_heads(per_head, low):
    placed = []
    for h, o in enumerate(per_head):
        placed.append(pltpu.roll(o, HEAD_DIM, axis=1) if h % 2 != h // B_REP else o)
    return [jnp.where(low, placed[2 * hp], placed[2 * hp + 1]) for hp in range(B_HEADS // 2)]


def _nsa_select_body(q_ref, cmp_ref, ng_ref, ovt_ref, gmap_ref, selb_ref, oc_ref, flags_ref, *, ncp, topk):
    q0 = pl.program_id(1) * Q_BLK
    lane = _iota((1, LANES), 1)
    low = lane < HEAD_DIM
    t_col = q0 + _iota((Q_BLK, 1), 0)
    t_row = q0 + _iota((1, Q_BLK), 1)
    nsel_pad = LANES
    groups = range(B_KV_GROUPS)

    cmp_end = CMP_STRIDE * _iota((1, ncp), 1) + (CMP_LEN - 1)
    mask_c = cmp_end <= t_col
    bias_c = jnp.where(mask_c, 0.0, NEG)
    blk_t = _iota((nsel_pad, 1), 0)
    allowed_t = blk_t * SEL_LEN <= t_row
    cur_t = t_row // SEL_LEN
    forced_t = (blk_t == 0) | (blk_t == cur_t) | (blk_t == cur_t - 1)
    blk_f = blk_t.astype(F32)

    q_all = q_ref[...].astype(F32)
    qg = [_q_aug(q_all, g, lane, low) for g in groups]
    scores = [_dot_nt(qg[g], cmp_ref[g]).reshape(B_REP, Q_BLK, ncp) + bias_c[None] for g in groups]
    has_key = (t_col >= CMP_LEN - 1)[None]
    probs = []
    for s in scores:
        e = jnp.exp2(s - jnp.max(s, axis=-1, keepdims=True))
        den = jnp.sum(e, axis=-1, keepdims=True)
        probs.append(e * jnp.where(has_key, 1.0 / den, 0.0))
    o_cmp = [_dot(probs[g].reshape(B_REP * Q_BLK, ncp).astype(BF16), cmp_ref[B_KV_GROUPS + g]) for g in groups]
    ranks = []
    for p in probs:
        psum = p[0] + p[1] + p[2] + p[3]
        p_hi = psum.astype(BF16)
        p_lo = (psum - p_hi.astype(F32)).astype(BF16)
        imp_t = _dot_nt(ovt_ref[...], p_hi) + _dot_nt(ovt_ref[...], p_lo)
        ranks.append(jnp.where(allowed_t, imp_t + jnp.where(forced_t, FORCE, 0.0), NEG))

    def pick_one(_, carry):
        out = []
        for rank, sel in carry:
            best = jnp.max(rank, axis=0, keepdims=True)
            cand = jnp.where(rank == best, blk_f, float(nsel_pad))
            idx = jnp.min(cand, axis=0, keepdims=True)
            pick = blk_f == idx
            out.append((jnp.where(pick, -3e38, rank), jnp.where(pick, 1.0, sel)))
        return tuple(out)

    picked = lax.fori_loop(0, topk, pick_one,
                           tuple((rank, jnp.zeros((nsel_pad, Q_BLK), F32)) for rank in ranks))
    ones = jnp.ones((8, Q_BLK), BF16)
    tile_cnt = jnp.zeros((8, LANES), F32)
    for g in groups:
        sel = jnp.where(allowed_t, picked[g][1], 0.0).T
        selb_ref[:, g * LANES:(g + 1) * LANES] = jnp.where(sel > 0.5, 0.0, -MASK_BIG).astype(BF16)
        per_block = _dot(ones, sel.astype(BF16))
        tile_cnt = tile_cnt + _dot(per_block.astype(BF16), gmap_ref[g])
    flags_ref[...] = (tile_cnt > 0.5).astype(jnp.int32)

    sig = jax.nn.sigmoid(ng_ref[...])
    per_head = []
    for h in range(B_HEADS):
        g, r = divmod(h, B_REP)
        per_head.append(sig[:, 3 * h:3 * h + 1] * o_cmp[g][r * Q_BLK:(r + 1) * Q_BLK])
    for hp, tile in enumerate(_place_heads(per_head, low)):
        oc_ref[:, hp * LANES:(hp + 1) * LANES] = tile


def _nsa_attend_body(lists_ref, cnts_ref, q_ref, ks0_ref, ks1_ref, kw0_ref, kw1_ref, vst_ref, vwt_ref,
                     selb_ref, oc_ref, ng_ref, oh_ref, out_ref, *, nq, ntile):
    step = pl.program_id(0) * nq + pl.program_id(1)
    q0 = pl.program_id(1) * Q_BLK
    lane = _iota((1, LANES), 1)
    low = lane < HEAD_DIM
    t_col = q0 + _iota((Q_BLK, 1), 0)
    t_row = q0 + _iota((1, Q_BLK), 1)
    win_keys = WIN + Q_BLK
    rows = B_REP * Q_BLK
    groups = range(B_KV_GROUPS)
    ks_refs = (ks0_ref, ks1_ref)
    kw_refs = (kw0_ref, kw1_ref)

    q_all = q_ref[...].astype(F32)
    qg = [_q_aug(q_all, g, lane, low) for g in groups]

    def heads_on_rows(o_t):
        o_t = jnp.concatenate([o_t, o_t], axis=0)
        return jnp.concatenate([o_t[:, r * Q_BLK:(r + 1) * Q_BLK].T for r in range(B_REP)], axis=0)

    def softmax_step(s, m_i):
        m_new = jnp.maximum(m_i, jnp.max(s, axis=0, keepdims=True))
        return jnp.exp2(s - m_new).astype(BF16), jnp.exp2(m_i - m_new), m_new

    q_slc = []
    for g in groups:
        sel_bias = selb_ref[:, g * LANES:(g + 1) * LANES]
        q_slc.append(jnp.concatenate([qg[g], jnp.concatenate([sel_bias] * B_REP, axis=0)], axis=1))

    def qk(g, kt):
        k0 = pl.multiple_of(jnp.minimum(kt, ntile - 1) * SLC_TILE, SLC_TILE)
        o0 = pl.multiple_of(kt * SLC_TILE, SLC_TILE)
        k_aug = jnp.concatenate([ks_refs[g][pl.ds(k0, SLC_TILE), :], oh_ref[pl.ds(o0, SLC_TILE), :]], axis=1)
        return _dot_nt(k_aug, q_slc[g])

    def pv(g, p, kt):
        return _dot(vst_ref[jnp.minimum(kt, ntile - 1), g], p)

    def trip(k, carry):
        work = [(g, lists_ref[(step * B_KV_GROUPS + g) * LIST_W + k * SLC_TRIP + u])
                for u in range(SLC_TRIP) for g in groups]
        scores = [qk(g, kt) for g, kt in work]
        state = list(carry)
        for (g, kt), s in zip(work, scores):
            m_i, acc = state[g]
            p, a, m_i = softmax_step(s, m_i)
            state[g] = (m_i, a * acc + pv(g, p, kt))
        return tuple(state)

    n_trips = jnp.maximum(cnts_ref[step * B_KV_GROUPS], cnts_ref[step * B_KV_GROUPS + 1]) // SLC_TRIP
    init = tuple((jnp.full((1, rows), -3e38, F32), jnp.zeros((VT_ROWS, rows), F32)) for g in groups)
    state = lax.fori_loop(0, n_trips, trip, init)

    kt_diag = q0 // SLC_TILE
    key_pos = kt_diag * SLC_TILE + _iota((SLC_TILE, 1), 0)
    causal = jnp.where(key_pos <= t_row, 0.0, NEG)
    causal = jnp.concatenate([causal] * B_REP, axis=1)
    scores = [qk(g, kt_diag) + causal for g in groups]
    o_slc = []
    for g in groups:
        m_i, acc = state[g]
        p, a, m_i = softmax_step(scores[g], m_i)
        acc = a * acc + pv(g, p, kt_diag)
        o_slc.append(heads_on_rows(acc[:HEAD_DIM] / acc[HEAD_DIM:HEAD_DIM + 1]))

    kstart = pl.multiple_of(jnp.maximum(q0 - WIN, 0), Q_BLK)
    d_w = t_row - (kstart + _iota((win_keys, 1), 0))
    bias_w = jnp.where((d_w >= 0) & (d_w < WIN), 0.0, NEG)
    bias_w = jnp.concatenate([bias_w] * B_REP, axis=1)
    scores = [_dot_nt(kw_refs[g][pl.ds(kstart, win_keys), :], qg[g]) + bias_w for g in groups]
    o_win = []
    for g in groups:
        s = scores[g]
        p = jnp.exp2(s - jnp.max(s, axis=0, keepdims=True)).astype(BF16)
        v_t = jnp.concatenate([vwt_ref[kstart // Q_BLK + u, g] for u in range(win_keys // Q_BLK)], axis=1)
        acc = _dot(v_t, p)
        o_win.append(heads_on_rows(acc[:HEAD_DIM] / acc[HEAD_DIM:HEAD_DIM + 1]))

    sig = jax.nn.sigmoid(ng_ref[...])
    per_head = []
    for h in range(B_HEADS):
        g, r = divmod(h, B_REP)
        rs = slice(r * Q_BLK, (r + 1) * Q_BLK)
        per_head.append(sig[:, 3 * h + 1:3 * h + 2] * o_slc[g][rs] + sig[:, 3 * h + 2:3 * h + 3] * o_win[g][rs])
    for hp, tile in enumerate(_place_heads(per_head, low)):
        cols = slice(hp * LANES, (hp + 1) * LANES)
        out_ref[:, cols] = (oc_ref[:, cols] + tile).astype(BF16)


def _tile_lists(flags, *, bsz, s_len):
    nq = s_len // Q_BLK
    ntile = s_len // SLC_TILE
    f = flags[:, 0, :B_KV_GROUPS * FLAG_W].reshape(bsz * nq, B_KV_GROUPS, FLAG_W)[:, :, :ntile] > 0
    kt = jnp.arange(ntile, dtype=jnp.int32)
    diag = jnp.tile((jnp.arange(nq, dtype=jnp.int32) * Q_BLK) // SLC_TILE, bsz)[:, None, None]
    touched = f & (kt < diag)
    rank = jnp.cumsum(touched, axis=-1, dtype=jnp.int32) - 1
    n_touched = rank[..., -1] + 1
    cnt = SLC_TRIP * ((n_touched + SLC_TRIP - 1) // SLC_TRIP)
    pos = jnp.arange(LIST_W, dtype=jnp.int32)
    hit = touched[..., None, :] & (rank[..., None, :] == pos[:, None])
    order = jnp.sum(jnp.where(hit, kt, 0), axis=-1)
    lists = jnp.where(pos < n_touched[..., None], order, ntile)
    return lists.reshape(-1), cnt.reshape(-1)


def _nsa2(qb, kvb, vst, vwt, cmp, ng, onehot, ovt, gmap, *, bsz, s_len):
    nq = s_len // Q_BLK
    ntile = s_len // SLC_TILE
    ncp = cmp.shape[2]
    topk = min(SEL_TOPK, s_len // SEL_LEN)
    hw = B_HEADS * HEAD_DIM
    rowblk = lambda b, i: (b * nq + i, 0)
    params = pltpu.CompilerParams(dimension_semantics=("arbitrary", "arbitrary"), vmem_limit_bytes=VMEM_LIMIT)
    selb, oc, flags = pl.pallas_call(
        functools.partial(_nsa_select_body, ncp=ncp, topk=topk),
        grid=(bsz, nq),
        in_specs=[pl.BlockSpec((Q_BLK, hw), rowblk),
                  pl.BlockSpec((None, 2 * B_KV_GROUPS, ncp, LANES), lambda b, i: (b, 0, 0, 0)),
                  pl.BlockSpec((Q_BLK, LANES), rowblk),
                  pl.BlockSpec(ovt.shape, lambda b, i: (0, 0)),
                  pl.BlockSpec(gmap.shape, lambda b, i: (0, 0, 0))],
        out_specs=[pl.BlockSpec((Q_BLK, B_KV_GROUPS * LANES), rowblk),
                   pl.BlockSpec((Q_BLK, hw), rowblk),
                   pl.BlockSpec((None, 8, LANES), lambda b, i: (b * nq + i, 0, 0))],
        out_shape=[jax.ShapeDtypeStruct((bsz * s_len, B_KV_GROUPS * LANES), BF16),
                   jax.ShapeDtypeStruct((bsz * s_len, hw), F32),
                   jax.ShapeDtypeStruct((bsz * nq, 8, LANES), jnp.int32)],
        compiler_params=params,
        name="nsa_select",
    )(qb, cmp, ng, ovt, gmap)

    lists, cnts = _tile_lists(flags, bsz=bsz, s_len=s_len)
    rowblk2 = lambda b, i, lists, cnts: (b * nq + i, 0)
    res = lambda col: (lambda b, i, lists, cnts: (b, col))
    return pl.pallas_call(
        functools.partial(_nsa_attend_body, nq=nq, ntile=ntile),
        grid_spec=pltpu.PrefetchScalarGridSpec(
            num_scalar_prefetch=2,
            grid=(bsz, nq),
            in_specs=[pl.BlockSpec((Q_BLK, hw), rowblk2)]
                     + [pl.BlockSpec((s_len, LANES), res(col)) for col in range(2 * B_KV_GROUPS)]
                     + [pl.BlockSpec((None,) + v.shape[1:], lambda b, i, lists, cnts: (b, 0, 0, 0, 0))
                        for v in (vst, vwt)]
                     + [pl.BlockSpec((Q_BLK, B_KV_GROUPS * LANES), rowblk2),
                        pl.BlockSpec((Q_BLK, hw), rowblk2),
                        pl.BlockSpec((Q_BLK, LANES), rowblk2),
                        pl.BlockSpec(onehot.shape, lambda b, i, lists, cnts: (0, 0))],
            out_specs=pl.BlockSpec((Q_BLK, hw), rowblk2)),
        out_shape=jax.ShapeDtypeStruct((bsz * s_len, hw), BF16),
        compiler_params=params,
        name="nsa_attend",
    )(lists, cnts, qb, kvb, kvb, kvb, kvb, vst, vwt, selb, oc, ng, onehot)


def _post_body(x_ref, oa_ref, ob_ref, gab_ref, p_ref, wua_ref, wub_ref, wout_ref, g2_ref, w1_ref, w2_ref,
               g3_ref, wpg_ref, wple_ref, gf_ref, out_ref, *, d, ff_chunk):
    ya = _dot(oa_ref[...], wua_ref[...])
    yb = _dot(ob_ref[...], wub_ref[...])
    mixed = (jax.nn.sigmoid(gab_ref[:, :d].astype(F32)) * ya
             + jax.nn.sigmoid(gab_ref[:, d:].astype(F32)) * yb)
    h = x_ref[...] + _dot(mixed.astype(BF16), wout_ref[...])
    n2 = _rms(h, g2_ref[...]).astype(BF16)
    acc = h
    for c in range(w1_ref.shape[1] // ff_chunk):
        cs = slice(c * ff_chunk, (c + 1) * ff_chunk)
        hid = jnp.square(jnp.maximum(_dot(n2, w1_ref[:, cs]), 0.0))
        acc = acc + _dot(hid.astype(BF16), w2_ref[cs, :])
    n3 = _rms(acc, g3_ref[...]).astype(BF16)
    gate = jax.nn.sigmoid(_dot(n3, wpg_ref[...]))
    h3 = acc + gate * _dot(p_ref[...].astype(BF16), wple_ref[...])
    out_ref[...] = _rms(h3, gf_ref[...])


def _post(x2, oa, ob, gab, p2, wua, wub, wout, g2, w1, w2, g3, wpg, wple, gf, *, tm=512, ff_chunk=1024):
    t_len, d = x2.shape
    row = lambda i: (i, 0)
    const = lambda i: (0, 0)
    resident = lambda a: pl.BlockSpec(a.shape, const, pipeline_mode=pl.Buffered(1))
    acts = (x2, oa, ob, gab, p2)
    params = (wua, wub, wout, g2, w1, w2, g3, wpg, wple, gf)
    return pl.pallas_call(
        functools.partial(_post_body, d=d, ff_chunk=ff_chunk),
        grid=(t_len // tm,),
        in_specs=[pl.BlockSpec((tm, a.shape[1]), row) for a in acts] + [resident(w) for w in params],
        out_specs=pl.BlockSpec((tm, d), row),
        out_shape=jax.ShapeDtypeStruct((t_len, d), F32),
        compiler_params=pltpu.CompilerParams(dimension_semantics=("arbitrary",),
                                             vmem_limit_bytes=VMEM_LIMIT),
        name="post",
    )(*acts, *params)


def _selection_overlap_t(ncp, s_len):
    ncmp = (s_len - CMP_LEN) // CMP_STRIDE + 1
    nsel = s_len // SEL_LEN
    ratio = SEL_LEN // CMP_STRIDE
    span = CMP_LEN // CMP_STRIDE
    i = np.arange(ncmp)[:, None]
    j = np.arange(nsel)[None, :]
    ov = np.maximum(np.minimum(i + span, ratio * (j + 1)) - np.maximum(i, ratio * j), 0)
    out = np.zeros((LANES, ncp), np.float32)
    out[:nsel, :ncmp] = ov.T
    return out


def _layer(h, p_i, norm_mix_g, w_in, pe_ck, w_ck1, w_ck2, pe_cv, w_cv1, w_cv2, w_up_a, w_up_b, w_out,
           norm_mlp_g, w_mlp1, w_mlp2, norm_ple_g, w_ple_gate, w_ple, final_g):
    bsz, s_len, d = h.shape
    t_len = bsz * s_len
    aw = A_HEADS * HEAD_DIM
    bw = B_HEADS * HEAD_DIM
    kvw = B_KV_GROUPS * HEAD_DIM
    assert s_len % A_TILE == 0 and s_len // SEL_LEN <= LANES and kvw == LANES

    o_qb = 3 * aw
    o_kv = o_qb + bw
    o_ng = o_kv + 6 * kvw
    o_ga = o_ng + 3 * B_HEADS
    kv = lambda i: w_in[:, o_kv + i * kvw:o_kv + (i + 1) * kvw]
    zeros_h = jnp.zeros((d, HEAD_DIM), w_in.dtype)
    grp = lambda w, g: jnp.concatenate([w[:, g * HEAD_DIM:(g + 1) * HEAD_DIM], zeros_h], axis=1)
    wa = jnp.concatenate([w_in[:, :aw] * (SCALE * LOG2E), w_in[:, aw:3 * aw]], axis=1)
    wc = jnp.concatenate([kv(0), kv(1)], axis=1)
    wq = w_in[:, o_qb:o_qb + bw] * (SCALE * LOG2E)
    wkv = jnp.concatenate([grp(kv(2), 0), grp(kv(2), 1), grp(kv(4), 0), grp(kv(4), 1)], axis=1)
    wvt = jnp.concatenate([kv(3), kv(5)], axis=1).T
    wng = jnp.concatenate([w_in[:, o_ng:o_ga], jnp.zeros((d, LANES - 3 * B_HEADS), w_in.dtype)], axis=1)
    wgab = w_in[:, o_ga:]
    x2 = h.reshape(t_len, d)
    a0, a1, a2, kvc, qb, kvb, ng, gab, vst, vwt = _proj(x2, norm_mix_g.reshape(1, d), *(w.astype(BF16) for w in
                                                        (wa, wc, wq, wkv, wng, wgab, wvt)), s_len=s_len)

    ncp = s_len // CMP_STRIDE
    half = CMP_STRIDE * HEAD_DIM
    chunks = kvc.reshape(bsz, ncp, CMP_STRIDE, 2 * B_KV_GROUPS, HEAD_DIM).transpose(0, 3, 1, 2, 4)
    chunks = chunks.reshape(bsz, 2 * B_KV_GROUPS, ncp, half)
    pe2 = jnp.stack([pe_ck.reshape(2, half), pe_cv.reshape(2, half)])
    w1s = jnp.stack([w_ck1.reshape(2, half, CMP_HIDDEN), w_cv1.reshape(2, half, CMP_HIDDEN)]).astype(BF16)
    zpad = jnp.zeros((CMP_HIDDEN, HEAD_DIM), w_ck2.dtype)
    w2s = jnp.stack([jnp.concatenate([w_ck2, zpad], axis=1), jnp.concatenate([w_ck2, zpad], axis=1),
                     jnp.concatenate([w_cv2, zpad], axis=1), jnp.concatenate([zpad, w_cv2], axis=1)]).astype(BF16)
    cmp = _compress(chunks, pe2, w1s, w2s)

    oa = _dilated((a0, a1, a2), bsz=bsz, s_len=s_len)
    onehot = (np.arange(s_len + SLC_TILE)[:, None] // SEL_LEN == np.arange(LANES)[None, :])
    onehot[s_len:] = True
    blocks_per_tile = SLC_TILE // SEL_LEN
    gmap = np.zeros((B_KV_GROUPS, LANES, LANES), np.float32)
    for g in range(B_KV_GROUPS):
        gmap[g, np.arange(LANES), FLAG_W * g + np.arange(LANES) // blocks_per_tile] = 1.0
    ob = _nsa2(qb, kvb, vst, vwt, cmp, ng, jnp.asarray(onehot, BF16),
               jnp.asarray(_selection_overlap_t(ncp, s_len), BF16), jnp.asarray(gmap, BF16), bsz=bsz, s_len=s_len)

    b16 = lambda w: w.astype(BF16)
    row = lambda v: v.reshape(1, d)
    return _post(x2, oa, ob, gab, p_i.reshape(t_len, -1), b16(w_up_a), b16(w_up_b), b16(w_out), row(norm_mlp_g),
                 b16(w_mlp1), b16(w_mlp2), row(norm_ple_g), b16(w_ple_gate), b16(w_ple), row(final_g)
                 ).reshape(bsz, s_len, d)


def kernel(x, p, norm_mix_g, w_in, pe_ck, w_ck1, w_ck2, pe_cv, w_cv1, w_cv2, w_up_a, w_up_b, w_out,
           norm_mlp_g, w_mlp1, w_mlp2, norm_ple_g, w_ple_gate, w_ple, norm_final_g):
    depth = w_in.shape[0]
    assert depth == 1, "the fused tail applies the final norm inside the single layer"
    return _layer(x, p[0], norm_mix_g[0], w_in[0], pe_ck[0], w_ck1[0], w_ck2[0], pe_cv[0], w_cv1[0], w_cv2[0],
                  w_up_a[0], w_up_b[0], w_out[0], norm_mlp_g[0], w_mlp1[0], w_mlp2[0], norm_ple_g[0],
                  w_ple_gate[0], w_ple[0], norm_final_g)
```

```python
import functools

import numpy as np
import jax
import jax.numpy as jnp
from jax import lax
from jax.experimental import pallas as pl
from jax.experimental.pallas import tpu as pltpu

HEAD_DIM = 64
A_HEADS = 8
A_CONFIGS = ((128, 1), (512, 4), (2048, 16))
B_HEADS = 8
B_KV_GROUPS = 2
B_REP = B_HEADS // B_KV_GROUPS
CMP_LEN = 32
CMP_STRIDE = 16
CMP_HIDDEN = 256
SEL_LEN = 64
SEL_TOPK = 16
WIN = 512
Q_BLK = 128
EPS = 1e-6
NEG = -1e30
FORCE = 1e9
MASK_BIG = 2.0 ** 100
SCALE = HEAD_DIM ** -0.5

LANES = 128
A_TILE = 2048
A_GROUP = 8
SLC_TILE = 256
VT_ROWS = HEAD_DIM + 16
FLAG_W = 32
LIST_W = FLAG_W
SLC_TRIP = 2
VMEM_LIMIT = 56 * 1024 * 1024

ALIBI_TERMS = 3
LANE_POS = 64
LANE_CMP = LANE_POS + 2 * ALIBI_TERMS
LOG2E = 1.4426950408889634

F32 = jnp.float32
BF16 = jnp.bfloat16


def _dot(a, b):
    return jnp.dot(a, b, preferred_element_type=F32)


def _dot_nt(a, b):
    return lax.dot_general(a, b, (((1,), (1,)), ((), ())), preferred_element_type=F32)


def _rms(x, g):
    inv = lax.rsqrt(jnp.mean(x * x, axis=-1, keepdims=True) + EPS)
    return (x * inv) * g


def _iota(shape, dim, dtype=jnp.int32):
    return lax.broadcasted_iota(dtype, shape, dim)


def _pair_columns(rel_lane, even_val, odd_val):
    inside = (rel_lane >= 0) & (rel_lane < 2 * ALIBI_TERMS)
    return jnp.where(inside, jnp.where(rel_lane % 2 == 0, even_val, odd_val), 0.0)


def _bf16_pieces(x):
    pieces, rest = [], np.float64(x)
    for _ in range(ALIBI_TERMS):
        piece = np.float64(np.asarray(rest, np.float32).astype(jnp.bfloat16).astype(np.float32))
        pieces.append(float(piece))
        rest = rest - piece
    return pieces


def _query_alibi_row(slope, lane):
    row = jnp.zeros(lane.shape, F32)
    coeffs = ((LANE_POS, SEL_LEN * slope), (LANE_POS + 1, slope),
              (LANE_CMP, 16 * CMP_STRIDE * slope), (LANE_CMP + 1, CMP_STRIDE * slope))
    for lane0, coeff in coeffs:
        for t, piece in enumerate(_bf16_pieces(coeff * LOG2E)):
            row = jnp.where(lane == lane0 + 2 * t, piece, row)
    return row


def _proj_body(x_ref, g_ref, wa_ref, wc_ref, wq_ref, wkv_ref, wng_ref, wgab_ref, wvt_ref,
               a0_ref, a1_ref, a2_ref, kvc_ref, qb_ref, kvb_ref, ng_ref, gab_ref, vst_ref, vwt_ref, res_sc,
               *, tm, s_len):
    n = _rms(x_ref[...], g_ref[...]).astype(BF16)
    vt = _dot_nt(wvt_ref[...], n)
    for branch, (out_ref, width) in enumerate(((vst_ref, SLC_TILE), (vwt_ref, Q_BLK))):
        for u in range(tm // width):
            for g in range(B_KV_GROUPS):
                r0 = (branch * B_KV_GROUPS + g) * HEAD_DIM
                out_ref[u, g, :HEAD_DIM, :] = vt[r0:r0 + HEAD_DIM, u * width:(u + 1) * width].astype(BF16)
                out_ref[u, g, HEAD_DIM:, :] = jnp.ones((VT_ROWS - HEAD_DIM, width), BF16)
    res = _dot(n, wa_ref[...])
    a_refs = (a0_ref, a1_ref, a2_ref)
    for s in range(res.shape[1] // LANES):
        cols = slice(s * LANES, (s + 1) * LANES)
        res_sc[s] = res[:, cols]
        for (window, dil), a_ref in zip(A_CONFIGS, a_refs):
            for r in range(dil):
                src = pl.ds(r, tm // dil, stride=dil) if dil > 1 else pl.ds(0, tm)
                a_ref[r, :, cols] = res_sc[s, src, :].astype(BF16)
    kvc_ref[...] = _dot(n, wc_ref[...])
    qb_ref[...] = _dot(n, wq_ref[...]).astype(BF16)
    ng_ref[...] = _dot(n, wng_ref[...])
    gab_ref[...] = _dot(n, wgab_ref[...]).astype(BF16)
    pos = (pl.program_id(0) * tm) % s_len + _iota((tm, LANES), 0)
    lane = _iota((tm, LANES), 1)
    posc = _pair_columns(lane - LANE_POS, (pos // SEL_LEN).astype(F32), (pos % SEL_LEN).astype(F32))
    kv = _dot(n, wkv_ref[...])
    for c in range(kv.shape[1] // LANES):
        kvb_ref[:, c * LANES:(c + 1) * LANES] = (kv[:, c * LANES:(c + 1) * LANES] + posc).astype(BF16)


def _proj(x2, g, wa, wc, wq, wkv, wng, wgab, wvt, *, s_len, tm=2 * SLC_TILE):
    t_len, d = x2.shape
    bsz = t_len // s_len
    nrt = s_len // tm
    const = lambda i: (0, 0)
    row = lambda i: (i, 0)
    ws = (wa, wc, wq, wkv, wng, wgab, wvt)
    flat = (wc, wq, wkv, wng, wgab)
    flat_dtypes = (F32, BF16, BF16, F32, BF16)
    aw = wa.shape[1]
    a_specs = [pl.BlockSpec((None, dil, tm // dil, aw), lambda i: (i // nrt, 0, i % nrt, 0))
               for _, dil in A_CONFIGS]
    a_shapes = [jax.ShapeDtypeStruct((bsz, dil, s_len // dil, aw), BF16) for _, dil in A_CONFIGS]
    return pl.pallas_call(
        functools.partial(_proj_body, tm=tm, s_len=s_len),
        grid=(t_len // tm,),
        in_specs=[pl.BlockSpec((tm, d), row), pl.BlockSpec((1, d), const)]
                 + [pl.BlockSpec(w.shape, const, pipeline_mode=pl.Buffered(1)) for w in ws],
        out_specs=a_specs + [pl.BlockSpec((tm, w.shape[1]), row) for w in flat]
                  + [pl.BlockSpec((None, tm // width, B_KV_GROUPS, VT_ROWS, width),
                                  lambda i: (i // nrt, i % nrt, 0, 0, 0)) for width in (SLC_TILE, Q_BLK)],
        out_shape=a_shapes + [jax.ShapeDtypeStruct((t_len, w.shape[1]), dt) for w, dt in zip(flat, flat_dtypes)]
                  + [jax.ShapeDtypeStruct((bsz, s_len // width, B_KV_GROUPS, VT_ROWS, width), BF16)
                     for width in (SLC_TILE, Q_BLK)],
        scratch_shapes=[pltpu.VMEM((aw // LANES, tm, LANES), F32)],
        compiler_params=pltpu.CompilerParams(dimension_semantics=("arbitrary",),
                                             vmem_limit_bytes=VMEM_LIMIT),
        name="proj",
    )(x2, g, *ws)


def _gelu_tanh(x):
    return 0.5 * x * (1.0 + jnp.tanh(np.sqrt(2.0 / np.pi).astype(np.float32) * (x + 0.044715 * (x * x * x))))


def _compress_body(x_ref, pe_ref, w1_ref, w2_ref, out_ref, *, ncp):
    first = jnp.zeros((ncp, B_KV_GROUPS * CMP_HIDDEN), F32)
    second = jnp.zeros((ncp, B_KV_GROUPS * CMP_HIDDEN), F32)
    for l in range(CMP_STRIDE):
        x_l = x_ref[pl.ds(l, ncp, stride=CMP_STRIDE), :]
        first = first + _dot((x_l + pe_ref[l:l + 1, :]).astype(BF16), w1_ref[l])
        second = second + _dot((x_l + pe_ref[CMP_STRIDE + l:CMP_STRIDE + l + 1, :]).astype(BF16),
                               w1_ref[CMP_STRIDE + l])
    pre = first + jnp.concatenate([second[1:], second[:1]], axis=0)
    hid = _gelu_tanh(pre).astype(BF16)
    is_key = pl.program_id(1) == 0
    n_idx = _iota((ncp, LANES), 0)
    lane = _iota((ncp, LANES), 1)
    nc = _pair_columns(lane - LANE_CMP, (n_idx // 16).astype(F32), (n_idx % 16).astype(F32))
    nc = jnp.where(is_key, nc, 0.0)
    for g in range(B_KV_GROUPS):
        out = _dot(hid[:, g * CMP_HIDDEN:(g + 1) * CMP_HIDDEN], w2_ref[g])
        out_ref[g] = (out + nc).astype(BF16)


def _compress(kvc, pe2, w1bd, w2s, *, bsz, s_len):
    ncp = s_len // CMP_STRIDE
    return pl.pallas_call(
        functools.partial(_compress_body, ncp=ncp),
        grid=(bsz, 2),
        in_specs=[pl.BlockSpec((s_len, LANES), lambda b, kv: (b, kv)),
                  pl.BlockSpec((None,) + pe2.shape[1:], lambda b, kv: (kv, 0, 0)),
                  pl.BlockSpec((None,) + w1bd.shape[1:], lambda b, kv: (kv, 0, 0, 0)),
                  pl.BlockSpec((B_KV_GROUPS, CMP_HIDDEN, LANES), lambda b, kv: (kv, 0, 0))],
        out_specs=pl.BlockSpec((None, B_KV_GROUPS, ncp, LANES), lambda b, kv: (b, kv, 0, 0)),
        out_shape=jax.ShapeDtypeStruct((bsz, 2 * B_KV_GROUPS, ncp, LANES), BF16),
        compiler_params=pltpu.CompilerParams(dimension_semantics=("arbitrary", "arbitrary"),
                                             vmem_limit_bytes=VMEM_LIMIT),
        name="compress",
    )(kvc, pe2, w1bd, w2s)


def _head_slope(h, n_heads):
    out = jnp.float32(2.0 ** (-8.0 * n_heads / n_heads))
    for k in range(n_heads - 1):
        out = jnp.where(h == k, jnp.float32(2.0 ** (-8.0 * (k + 1) / n_heads)), out)
    return out


def _dilated_body(*refs):
    ncfg = len(A_CONFIGS)
    in_refs = refs[:5 * ncfg]
    out_ref = refs[5 * ncfg]
    o_sc, l_sc, m_sc, bias_sc = refs[5 * ncfg + 1:5 * ncfg + 5]
    kbufs = refs[5 * ncfg + 5:5 * ncfg + 5 + ncfg]
    vbufs = refs[5 * ncfg + 5 + ncfg:]
    hp = pl.program_id(1)
    first_tile = pl.program_id(2) == 0
    low = _iota((1, LANES), 1) < HEAD_DIM

    @pl.when(first_tile)
    def _():
        qi = _iota((Q_BLK, 2 * Q_BLK), 0)
        kj = _iota((Q_BLK, 2 * Q_BLK), 1)
        dist = qi - kj + Q_BLK
        for c, (window, dil) in enumerate(A_CONFIGS):
            valid = (dist >= 0) & (dist <= window // dil)
            for hh in range(2):
                slope = _head_slope(2 * hp + hh, A_HEADS)
                bias = jnp.where(valid, -(slope * dil * LOG2E) * dist.astype(F32), NEG)
                bias_sc[(c * 2 + hh) * 2] = bias
                bias_sc[(c * 2 + hh) * 2 + 1] = jnp.where(kj < Q_BLK, NEG, bias)

    for c, (window, dil) in enumerate(A_CONFIGS):
        q_ref, k_ref, v_ref, kh_ref, vh_ref = in_refs[5 * c:5 * c + 5]
        kbuf, vbuf = kbufs[c], vbufs[c]
        rows = A_TILE // dil
        nsub = rows // Q_BLK
        kbuf[:, :Q_BLK, :] = kh_ref[...]
        kbuf[:, Q_BLK:, :] = k_ref[...]
        vbuf[:, :Q_BLK, :] = vh_ref[...]
        vbuf[:, Q_BLK:, :] = v_ref[...]

        def group(gidx, carry, c=c, dil=dil, nsub=nsub, q_ref=q_ref, kbuf=kbuf, vbuf=vbuf):
            subs = []
            for u in range(A_GROUP):
                idx = gidx * A_GROUP + u
                r = idx // nsub
                j = idx % nsub
                j0 = pl.multiple_of(j * Q_BLK, Q_BLK)
                seq_start = ((j == 0) & first_tile).astype(jnp.int32)
                subs.append((r, j, j0, seq_start))
            scores = []
            for r, j, j0, seq_start in subs:
                q = q_ref[r, pl.ds(j0, Q_BLK), :]
                k2 = kbuf[r, pl.ds(j0, 2 * Q_BLK), :]
                for hh in range(2):
                    qm = jnp.where(low == (hh == 0), q, jnp.zeros_like(q))
                    scores.append(_dot_nt(qm, k2) + bias_sc[(c * 2 + hh) * 2 + seq_start])
            probs = []
            for s in scores:
                m = jnp.max(s, axis=-1, keepdims=True)
                e = jnp.exp2(s - m)
                probs.append((e.astype(BF16), m, jnp.sum(e, axis=-1, keepdims=True)))
            for u, (r, j, j0, seq_start) in enumerate(subs):
                v2 = vbuf[r, pl.ds(j0, 2 * Q_BLK), :]
                outs = [_dot(probs[2 * u + hh][0], v2) for hh in range(2)]
                stats = [[jnp.broadcast_to(probs[2 * u + hh][i], (Q_BLK, LANES)) for hh in range(2)] for i in (1, 2)]
                row0 = j * (Q_BLK * dil) + r
                dst = pl.ds(row0, Q_BLK, stride=dil) if dil > 1 else pl.ds(row0, Q_BLK)
                o_sc[c, dst, :] = jnp.where(low, outs[0], outs[1])
                m_sc[c, dst, :] = jnp.where(low, stats[0][0], stats[0][1])
                l_sc[c, dst, :] = jnp.where(low, stats[1][0], stats[1][1])
            return carry

        lax.fori_loop(0, dil * nsub // A_GROUP, group, 0)

    m = jnp.maximum(jnp.maximum(m_sc[0], m_sc[1]), m_sc[2])
    num = jnp.zeros((A_TILE, LANES), F32)
    den = jnp.zeros((A_TILE, LANES), F32)
    for c in range(ncfg):
        e = jnp.exp2(m_sc[c] - m)
        num = num + e * o_sc[c]
        den = den + e * l_sc[c]
    out_ref[...] = (num / den).astype(BF16)


def _dilated(qkv_by_cfg, *, bsz, s_len):
    nt = s_len // A_TILE
    npair = A_HEADS // 2
    ncfg = len(A_CONFIGS)
    in_specs, operands, kv_scratch = [], [], []
    for (window, dil), arr in zip(A_CONFIGS, qkv_by_cfg):
        rows = A_TILE // dil
        nsub = rows // Q_BLK
        cur = lambda off: (lambda b, h, t: (b, 0, t, off + h))
        halo = lambda off, nsub=nsub: (lambda b, h, t: (b, 0, jnp.maximum(t * nsub - 1, 0), off + h))
        in_specs += [pl.BlockSpec((None, dil, rows, LANES), cur(0)),
                     pl.BlockSpec((None, dil, rows, LANES), cur(npair)),
                     pl.BlockSpec((None, dil, rows, LANES), cur(2 * npair)),
                     pl.BlockSpec((None, dil, Q_BLK, LANES), halo(npair)),
                     pl.BlockSpec((None, dil, Q_BLK, LANES), halo(2 * npair))]
        operands += [arr] * 5
        kv_scratch.append(pltpu.VMEM((dil, Q_BLK + rows, LANES), BF16))
    return pl.pallas_call(
        _dilated_body,
        grid=(bsz, npair, nt),
        in_specs=in_specs,
        out_specs=pl.BlockSpec((A_TILE, LANES), lambda b, h, t: (b * nt + t, h)),
        out_shape=jax.ShapeDtypeStruct((bsz * s_len, A_HEADS * HEAD_DIM), BF16),
        scratch_shapes=[pltpu.VMEM((ncfg, A_TILE, LANES), F32)] * 3
                       + [pltpu.VMEM((ncfg * 4, Q_BLK, 2 * Q_BLK), F32)] + kv_scratch + kv_scratch,
        compiler_params=pltpu.CompilerParams(dimension_semantics=("arbitrary",) * 3,
                                             vmem_limit_bytes=VMEM_LIMIT),
        name="dilated",
    )(*operands)


def _q_aug(q_all, g, lane, low):
    rows = []
    for r in range(B_REP):
        h = g * B_REP + r
        blk = q_all[:, (h // 2) * LANES:(h // 2 + 1) * LANES]
        if h % 2 == 1:
            blk = pltpu.roll(blk, HEAD_DIM, axis=1)
        rows.append(jnp.where(low, blk, _query_alibi_row(2.0 ** (-8.0 * (h + 1) / B_HEADS), lane)))
    return jnp.concatenate(rows, axis=0).astype(BF16)


def _place_heads(per_head, low):
    placed = []
    for h, o in enumerate(per_head):
        placed.append(pltpu.roll(o, HEAD_DIM, axis=1) if h % 2 != h // B_REP else o)
    return [jnp.where(low, placed[2 * hp], placed[2 * hp + 1]) for hp in range(B_HEADS // 2)]


def _nsa_select_body(q_ref, cmp_ref, ng_ref, ovt_ref, gmap_ref, selb_ref, oc_ref, flags_ref, *, ncp, topk):
    q0 = pl.program_id(1) * Q_BLK
    lane = _iota((1, LANES), 1)
    low = lane < HEAD_DIM
    t_col = q0 + _iota((Q_BLK, 1), 0)
    t_row = q0 + _iota((1, Q_BLK), 1)
    nsel_pad = LANES
    groups = range(B_KV_GROUPS)

    cmp_end = CMP_STRIDE * _iota((1, ncp), 1) + (CMP_LEN - 1)
    mask_c = cmp_end <= t_col
    bias_c = jnp.where(mask_c, 0.0, NEG)
    blk_t = _iota((nsel_pad, 1), 0)
    allowed_t = blk_t * SEL_LEN <= t_row
    cur_t = t_row // SEL_LEN
    forced_t = (blk_t == 0) | (blk_t == cur_t) | (blk_t == cur_t - 1)
    blk_f = blk_t.astype(F32)

    q_all = q_ref[...].astype(F32)
    qg = [_q_aug(q_all, g, lane, low) for g in groups]
    scores = [_dot_nt(qg[g], cmp_ref[g]).reshape(B_REP, Q_BLK, ncp) + bias_c[None] for g in groups]
    has_key = (t_col >= CMP_LEN - 1)[None]
    probs = []
    for s in scores:
        e = jnp.exp2(s - jnp.max(s, axis=-1, keepdims=True))
        den = jnp.sum(e, axis=-1, keepdims=True)
        probs.append(e * jnp.where(has_key, 1.0 / den, 0.0))
    o_cmp = [_dot(probs[g].reshape(B_REP * Q_BLK, ncp).astype(BF16), cmp_ref[B_KV_GROUPS + g]) for g in groups]
    ranks = []
    for p in probs:
        psum = p[0] + p[1] + p[2] + p[3]
        p_hi = psum.astype(BF16)
        p_lo = (psum - p_hi.astype(F32)).astype(BF16)
        imp_t = _dot_nt(ovt_ref[...], p_hi) + _dot_nt(ovt_ref[...], p_lo)
        ranks.append(jnp.where(allowed_t, imp_t + jnp.where(forced_t, FORCE, 0.0), NEG))

    def pick_one(_, carry):
        out = []
        for rank, sel in carry:
            best = jnp.max(rank, axis=0, keepdims=True)
            cand = jnp.where(rank == best, blk_f, float(nsel_pad))
            idx = jnp.min(cand, axis=0, keepdims=True)
            pick = blk_f == idx
            out.append((jnp.where(pick, -3e38, rank), jnp.where(pick, 1.0, sel)))
        return tuple(out)

    picked = lax.fori_loop(0, topk, pick_one,
                           tuple((rank, jnp.zeros((nsel_pad, Q_BLK), F32)) for rank in ranks))
    ones = jnp.ones((8, Q_BLK), BF16)
    tile_cnt = jnp.zeros((8, LANES), F32)
    for g in groups:
        sel = jnp.where(allowed_t, picked[g][1], 0.0).T
        selb_ref[:, g * LANES:(g + 1) * LANES] = jnp.where(sel > 0.5, 0.0, -MASK_BIG).astype(BF16)
        per_block = _dot(ones, sel.astype(BF16))
        tile_cnt = tile_cnt + _dot(per_block.astype(BF16), gmap_ref[g])
    flags_ref[...] = (tile_cnt > 0.5).astype(jnp.int32)

    sig = jax.nn.sigmoid(ng_ref[...])
    per_head = []
    for h in range(B_HEADS):
        g, r = divmod(h, B_REP)
        per_head.append(sig[:, 3 * h:3 * h + 1] * o_cmp[g][r * Q_BLK:(r + 1) * Q_BLK])
    for hp, tile in enumerate(_place_heads(per_head, low)):
        oc_ref[:, hp * LANES:(hp + 1) * LANES] = tile


def _nsa_attend_body(lists_ref, cnts_ref, q_ref, ks0_ref, ks1_ref, kw0_ref, kw1_ref, vst_ref, vwt_ref,
                     selb_ref, oc_ref, ng_ref, oh_ref, out_ref, *, nq, ntile):
    step = pl.program_id(0) * nq + pl.program_id(1)
    q0 = pl.program_id(1) * Q_BLK
    lane = _iota((1, LANES), 1)
    low = lane < HEAD_DIM
    t_col = q0 + _iota((Q_BLK, 1), 0)
    t_row = q0 + _iota((1, Q_BLK), 1)
    win_keys = WIN + Q_BLK
    rows = B_REP * Q_BLK
    groups = range(B_KV_GROUPS)
    ks_refs = (ks0_ref, ks1_ref)
    kw_refs = (kw0_ref, kw1_ref)

    q_all = q_ref[...].astype(F32)
    qg = [_q_aug(q_all, g, lane, low) for g in groups]

    def heads_on_rows(o_t):
        o_t = jnp.concatenate([o_t, o_t], axis=0)
        return jnp.concatenate([o_t[:, r * Q_BLK:(r + 1) * Q_BLK].T for r in range(B_REP)], axis=0)

    def softmax_step(s, m_i):
        m_new = jnp.maximum(m_i, jnp.max(s, axis=0, keepdims=True))
        return jnp.exp2(s - m_new).astype(BF16), jnp.exp2(m_i - m_new), m_new

    q_slc = []
    for g in groups:
        sel_bias = selb_ref[:, g * LANES:(g + 1) * LANES]
        q_slc.append(jnp.concatenate([qg[g], jnp.concatenate([sel_bias] * B_REP, axis=0)], axis=1))

    def qk(g, kt):
        k0 = pl.multiple_of(jnp.minimum(kt, ntile - 1) * SLC_TILE, SLC_TILE)
        o0 = pl.multiple_of(kt * SLC_TILE, SLC_TILE)
        k_aug = jnp.concatenate([ks_refs[g][pl.ds(k0, SLC_TILE), :], oh_ref[pl.ds(o0, SLC_TILE), :]], axis=1)
        return _dot_nt(k_aug, q_slc[g])

    def pv(g, p, kt):
        return _dot(vst_ref[jnp.minimum(kt, ntile - 1), g], p)

    def trip(k, carry):
        work = [(g, lists_ref[(step * B_KV_GROUPS + g) * LIST_W + k * SLC_TRIP + u])
                for u in range(SLC_TRIP) for g in groups]
        scores = [qk(g, kt) for g, kt in work]
        state = list(carry)
        for (g, kt), s in zip(work, scores):
            m_i, acc = state[g]
            p, a, m_i = softmax_step(s, m_i)
            state[g] = (m_i, a * acc + pv(g, p, kt))
        return tuple(state)

    n_trips = jnp.maximum(cnts_ref[step * B_KV_GROUPS], cnts_ref[step * B_KV_GROUPS + 1]) // SLC_TRIP
    init = tuple((jnp.full((1, rows), -3e38, F32), jnp.zeros((VT_ROWS, rows), F32)) for g in groups)
    state = lax.fori_loop(0, n_trips, trip, init)

    kt_diag = q0 // SLC_TILE
    key_pos = kt_diag * SLC_TILE + _iota((SLC_TILE, 1), 0)
    causal = jnp.where(key_pos <= t_row, 0.0, NEG)
    causal = jnp.concatenate([causal] * B_REP, axis=1)
    scores = [qk(g, kt_diag) + causal for g in groups]
    o_slc = []
    for g in groups:
        m_i, acc = state[g]
        p, a, m_i = softmax_step(scores[g], m_i)
        acc = a * acc + pv(g, p, kt_diag)
        o_slc.append(heads_on_rows(acc[:HEAD_DIM] / acc[HEAD_DIM:HEAD_DIM + 1]))

    kstart = pl.multiple_of(jnp.maximum(q0 - WIN, 0), Q_BLK)
    d_w = t_row - (kstart + _iota((win_keys, 1), 0))
    bias_w = jnp.where((d_w >= 0) & (d_w < WIN), 0.0, NEG)
    bias_w = jnp.concatenate([bias_w] * B_REP, axis=1)
    scores = [_dot_nt(kw_refs[g][pl.ds(kstart, win_keys), :], qg[g]) + bias_w for g in groups]
    o_win = []
    for g in groups:
        s = scores[g]
        p = jnp.exp2(s - jnp.max(s, axis=0, keepdims=True)).astype(BF16)
        v_t = jnp.concatenate([vwt_ref[kstart // Q_BLK + u, g] for u in range(win_keys // Q_BLK)], axis=1)
        acc = _dot(v_t, p)
        o_win.append(heads_on_rows(acc[:HEAD_DIM] / acc[HEAD_DIM:HEAD_DIM + 1]))

    sig = jax.nn.sigmoid(ng_ref[...])
    per_head = []
    for h in range(B_HEADS):
        g, r = divmod(h, B_REP)
        rs = slice(r * Q_BLK, (r + 1) * Q_BLK)
        per_head.append(sig[:, 3 * h + 1:3 * h + 2] * o_slc[g][rs] + sig[:, 3 * h + 2:3 * h + 3] * o_win[g][rs])
    for hp, tile in enumerate(_place_heads(per_head, low)):
        cols = slice(hp * LANES, (hp + 1) * LANES)
        out_ref[:, cols] = (oc_ref[:, cols] + tile).astype(BF16)


def _tile_lists(flags, *, bsz, s_len):
    nq = s_len // Q_BLK
    ntile = s_len // SLC_TILE
    f = flags[:, 0, :B_KV_GROUPS * FLAG_W].reshape(bsz * nq, B_KV_GROUPS, FLAG_W)[:, :, :ntile] > 0
    kt = jnp.arange(ntile, dtype=jnp.int32)
    diag = jnp.tile((jnp.arange(nq, dtype=jnp.int32) * Q_BLK) // SLC_TILE, bsz)[:, None, None]
    touched = f & (kt < diag)
    rank = jnp.cumsum(touched, axis=-1, dtype=jnp.int32) - 1
    n_touched = rank[..., -1] + 1
    cnt = SLC_TRIP * ((n_touched + SLC_TRIP - 1) // SLC_TRIP)
    pos = jnp.arange(LIST_W, dtype=jnp.int32)
    hit = touched[..., None, :] & (rank[..., None, :] == pos[:, None])
    order = jnp.sum(jnp.where(hit, kt, 0), axis=-1)
    lists = jnp.where(pos < n_touched[..., None], order, ntile)
    return lists.reshape(-1), cnt.reshape(-1)


def _nsa2(qb, kvb, vst, vwt, cmp, ng, onehot, ovt, gmap, *, bsz, s_len):
    nq = s_len // Q_BLK
    ntile = s_len // SLC_TILE
    ncp = cmp.shape[2]
    topk = min(SEL_TOPK, s_len // SEL_LEN)
    hw = B_HEADS * HEAD_DIM
    rowblk = lambda b, i: (b * nq + i, 0)
    params = pltpu.CompilerParams(dimension_semantics=("arbitrary", "arbitrary"), vmem_limit_bytes=VMEM_LIMIT)
    selb, oc, flags = pl.pallas_call(
        functools.partial(_nsa_select_body, ncp=ncp, topk=topk),
        grid=(bsz, nq),
        in_specs=[pl.BlockSpec((Q_BLK, hw), rowblk),
                  pl.BlockSpec((None, 2 * B_KV_GROUPS, ncp, LANES), lambda b, i: (b, 0, 0, 0)),
                  pl.BlockSpec((Q_BLK, LANES), rowblk),
                  pl.BlockSpec(ovt.shape, lambda b, i: (0, 0)),
                  pl.BlockSpec(gmap.shape, lambda b, i: (0, 0, 0))],
        out_specs=[pl.BlockSpec((Q_BLK, B_KV_GROUPS * LANES), rowblk),
                   pl.BlockSpec((Q_BLK, hw), rowblk),
                   pl.BlockSpec((None, 8, LANES), lambda b, i: (b * nq + i, 0, 0))],
        out_shape=[jax.ShapeDtypeStruct((bsz * s_len, B_KV_GROUPS * LANES), BF16),
                   jax.ShapeDtypeStruct((bsz * s_len, hw), F32),
                   jax.ShapeDtypeStruct((bsz * nq, 8, LANES), jnp.int32)],
        compiler_params=params,
        name="nsa_select",
    )(qb, cmp, ng, ovt, gmap)

    lists, cnts = _tile_lists(flags, bsz=bsz, s_len=s_len)
    rowblk2 = lambda b, i, lists, cnts: (b * nq + i, 0)
    res = lambda col: (lambda b, i, lists, cnts: (b, col))
    return pl.pallas_call(
        functools.partial(_nsa_attend_body, nq=nq, ntile=ntile),
        grid_spec=pltpu.PrefetchScalarGridSpec(
            num_scalar_prefetch=2,
            grid=(bsz, nq),
            in_specs=[pl.BlockSpec((Q_BLK, hw), rowblk2)]
                     + [pl.BlockSpec((s_len, LANES), res(col)) for col in range(2 * B_KV_GROUPS)]
                     + [pl.BlockSpec((None,) + v.shape[1:], lambda b, i, lists, cnts: (b, 0, 0, 0, 0))
                        for v in (vst, vwt)]
                     + [pl.BlockSpec((Q_BLK, B_KV_GROUPS * LANES), rowblk2),
                        pl.BlockSpec((Q_BLK, hw), rowblk2),
                        pl.BlockSpec((Q_BLK, LANES), rowblk2),
                        pl.BlockSpec(onehot.shape, lambda b, i, lists, cnts: (0, 0))],
            out_specs=pl.BlockSpec((Q_BLK, hw), rowblk2)),
        out_shape=jax.ShapeDtypeStruct((bsz * s_len, hw), BF16),
        compiler_params=params,
        name="nsa_attend",
    )(lists, cnts, qb, kvb, kvb, kvb, kvb, vst, vwt, selb, oc, ng, onehot)


def _post_body(x_ref, oa_ref, ob_ref, gab_ref, p_ref, wua_ref, wub_ref, wout_ref, g2_ref, w1_ref, w2_ref,
               g3_ref, wpg_ref, wple_ref, gf_ref, out_ref, *, d, ff_chunk):
    ya = _dot(oa_ref[...], wua_ref[...])
    yb = _dot(ob_ref[...], wub_ref[...])
    mixed = (jax.nn.sigmoid(gab_ref[:, :d].astype(F32)) * ya
             + jax.nn.sigmoid(gab_ref[:, d:].astype(F32)) * yb)
    h = x_ref[...] + _dot(mixed.astype(BF16), wout_ref[...])
    n2 = _rms(h, g2_ref[...]).astype(BF16)
    acc = h
    for c in range(w1_ref.shape[1] // ff_chunk):
        cs = slice(c * ff_chunk, (c + 1) * ff_chunk)
        hid = jnp.square(jnp.maximum(_dot(n2, w1_ref[:, cs]), 0.0))
        acc = acc + _dot(hid.astype(BF16), w2_ref[cs, :])
    n3 = _rms(acc, g3_ref[...]).astype(BF16)
    gate = jax.nn.sigmoid(_dot(n3, wpg_ref[...]))
    h3 = acc + gate * _dot(p_ref[...].astype(BF16), wple_ref[...])
    out_ref[...] = _rms(h3, gf_ref[...])


def _post(x2, oa, ob, gab, p2, wua, wub, wout, g2, w1, w2, g3, wpg, wple, gf, *, tm=512, ff_chunk=1024):
    t_len, d = x2.shape
    row = lambda i: (i, 0)
    const = lambda i: (0, 0)
    resident = lambda a: pl.BlockSpec(a.shape, const, pipeline_mode=pl.Buffered(1))
    acts = (x2, oa, ob, gab, p2)
    params = (wua, wub, wout, g2, w1, w2, g3, wpg, wple, gf)
    return pl.pallas_call(
        functools.partial(_post_body, d=d, ff_chunk=ff_chunk),
        grid=(t_len // tm,),
        in_specs=[pl.BlockSpec((tm, a.shape[1]), row) for a in acts] + [resident(w) for w in params],
        out_specs=pl.BlockSpec((tm, d), row),
        out_shape=jax.ShapeDtypeStruct((t_len, d), F32),
        compiler_params=pltpu.CompilerParams(dimension_semantics=("arbitrary",),
                                             vmem_limit_bytes=VMEM_LIMIT),
        name="post",
    )(*acts, *params)


def _selection_overlap_t(ncp, s_len):
    ncmp = (s_len - CMP_LEN) // CMP_STRIDE + 1
    nsel = s_len // SEL_LEN
    ratio = SEL_LEN // CMP_STRIDE
    span = CMP_LEN // CMP_STRIDE
    i = np.arange(ncmp)[:, None]
    j = np.arange(nsel)[None, :]
    ov = np.maximum(np.minimum(i + span, ratio * (j + 1)) - np.maximum(i, ratio * j), 0)
    out = np.zeros((LANES, ncp), np.float32)
    out[:nsel, :ncmp] = ov.T
    return out


def _layer(h, p_i, norm_mix_g, w_in, pe_ck, w_ck1, w_ck2, pe_cv, w_cv1, w_cv2, w_up_a, w_up_b, w_out,
           norm_mlp_g, w_mlp1, w_mlp2, norm_ple_g, w_ple_gate, w_ple, final_g):
    bsz, s_len, d = h.shape
    t_len = bsz * s_len
    aw = A_HEADS * HEAD_DIM
    bw = B_HEADS * HEAD_DIM
    kvw = B_KV_GROUPS * HEAD_DIM
    assert s_len % A_TILE == 0 and s_len // SEL_LEN <= LANES and kvw == LANES

    o_qb = 3 * aw
    o_kv = o_qb + bw
    o_ng = o_kv + 6 * kvw
    o_ga = o_ng + 3 * B_HEADS
    kv = lambda i: w_in[:, o_kv + i * kvw:o_kv + (i + 1) * kvw]
    zeros_h = jnp.zeros((d, HEAD_DIM), w_in.dtype)
    grp = lambda w, g: jnp.concatenate([w[:, g * HEAD_DIM:(g + 1) * HEAD_DIM], zeros_h], axis=1)
    wa = jnp.concatenate([w_in[:, :aw] * (SCALE * LOG2E), w_in[:, aw:3 * aw]], axis=1)
    wc = jnp.concatenate([kv(0), kv(1)], axis=1)
    wq = w_in[:, o_qb:o_qb + bw] * (SCALE * LOG2E)
    wkv = jnp.concatenate([grp(kv(2), 0), grp(kv(2), 1), grp(kv(4), 0), grp(kv(4), 1)], axis=1)
    wvt = jnp.concatenate([kv(3), kv(5)], axis=1).T
    wng = jnp.concatenate([w_in[:, o_ng:o_ga], jnp.zeros((d, LANES - 3 * B_HEADS), w_in.dtype)], axis=1)
    wgab = w_in[:, o_ga:]
    x2 = h.reshape(t_len, d)
    a0, a1, a2, kvc, qb, kvb, ng, gab, vst, vwt = _proj(x2, norm_mix_g.reshape(1, d), *(w.astype(BF16) for w in
                                                        (wa, wc, wq, wkv, wng, wgab, wvt)), s_len=s_len)

    ncp = s_len // CMP_STRIDE
    pe2 = jnp.stack([jnp.tile(pe, (1, B_KV_GROUPS)) for pe in (pe_ck, pe_cv)])

    def per_row_block_diag(w1):
        w = w1.reshape(CMP_LEN, HEAD_DIM, CMP_HIDDEN)
        z = jnp.zeros_like(w)
        return jnp.concatenate([jnp.concatenate([w, z], axis=2), jnp.concatenate([z, w], axis=2)], axis=1)

    w1bd = jnp.stack([per_row_block_diag(w_ck1), per_row_block_diag(w_cv1)]).astype(BF16)
    zpad = jnp.zeros((CMP_HIDDEN, HEAD_DIM), w_ck2.dtype)
    w2s = jnp.stack([jnp.concatenate([w_ck2, zpad], axis=1), jnp.concatenate([w_ck2, zpad], axis=1),
                     jnp.concatenate([w_cv2, zpad], axis=1), jnp.concatenate([zpad, w_cv2], axis=1)]).astype(BF16)
    cmp = _compress(kvc, pe2, w1bd, w2s, bsz=bsz, s_len=s_len)

    oa = _dilated((a0, a1, a2), bsz=bsz, s_len=s_len)
    onehot = (np.arange(s_len + SLC_TILE)[:, None] // SEL_LEN == np.arange(LANES)[None, :])
    onehot[s_len:] = True
    blocks_per_tile = SLC_TILE // SEL_LEN
    gmap = np.zeros((B_KV_GROUPS, LANES, LANES), np.float32)
    for g in range(B_KV_GROUPS):
        gmap[g, np.arange(LANES), FLAG_W * g + np.arange(LANES) // blocks_per_tile] = 1.0
    ob = _nsa2(qb, kvb, vst, vwt, cmp, ng, jnp.asarray(onehot, BF16),
               jnp.asarray(_selection_overlap_t(ncp, s_len), BF16), jnp.asarray(gmap, BF16), bsz=bsz, s_len=s_len)

    b16 = lambda w: w.astype(BF16)
    row = lambda v: v.reshape(1, d)
    return _post(x2, oa, ob, gab, p_i.reshape(t_len, -1), b16(w_up_a), b16(w_up_b), b16(w_out), row(norm_mlp_g),
                 b16(w_mlp1), b16(w_mlp2), row(norm_ple_g), b16(w_ple_gate), b16(w_ple), row(final_g)
                 ).reshape(bsz, s_len, d)


def kernel(x, p, norm_mix_g, w_in, pe_ck, w_ck1, w_ck2, pe_cv, w_cv1, w_cv2, w_up_a, w_up_b, w_out,
           norm_mlp_g, w_mlp1, w_mlp2, norm_ple_g, w_ple_gate, w_ple, norm_final_g):
    depth = w_in.shape[0]
    assert depth == 1, "the fused tail applies the final norm inside the single layer"
    return _layer(x, p[0], norm_mix_g[0], w_in[0], pe_ck[0], w_ck1[0], w_ck2[0], pe_cv[0], w_cv1[0], w_cv2[0],
                  w_up_a[0], w_up_b[0], w_out[0], norm_mlp_g[0], w_mlp1[0], w_mlp2[0], norm_ple_g[0],
                  w_ple_gate[0], w_ple[0], norm_final_g)
```

```python
import functools

import numpy as np
import jax
import jax.numpy as jnp
from jax import lax
from jax.experimental import pallas as pl
from jax.experimental.pallas import tpu as pltpu

HEAD_DIM = 64
A_HEADS = 8
A_CONFIGS = ((128, 1), (512, 4), (2048, 16))
B_HEADS = 8
B_KV_GROUPS = 2
B_REP = B_HEADS // B_KV_GROUPS
CMP_LEN = 32
CMP_STRIDE = 16
CMP_HIDDEN = 256
SEL_LEN = 64
SEL_TOPK = 16
WIN = 512
Q_BLK = 128
EPS = 1e-6
NEG = -1e30
FORCE = 1e9
MASK_BIG = 2.0 ** 100
SCALE = HEAD_DIM ** -0.5

LANES = 128
A_TILE = 2048
A_GROUP = 8
SLC_TILE = 256
VT_ROWS = HEAD_DIM + 16
FLAG_W = 32
SLC_TRIP = 3
LIST_W = FLAG_W + SLC_TRIP
VMEM_LIMIT = 56 * 1024 * 1024

ALIBI_TERMS = 3
LANE_POS = 64
LANE_CMP = LANE_POS + 2 * ALIBI_TERMS
LOG2E = 1.4426950408889634

F32 = jnp.float32
BF16 = jnp.bfloat16


def _dot(a, b):
    return jnp.dot(a, b, preferred_element_type=F32)


def _dot_nt(a, b):
    return lax.dot_general(a, b, (((1,), (1,)), ((), ())), preferred_element_type=F32)


def _rms(x, g):
    inv = lax.rsqrt(jnp.mean(x * x, axis=-1, keepdims=True) + EPS)
    return (x * inv) * g


def _iota(shape, dim, dtype=jnp.int32):
    return lax.broadcasted_iota(dtype, shape, dim)


def _pair_columns(rel_lane, even_val, odd_val):
    inside = (rel_lane >= 0) & (rel_lane < 2 * ALIBI_TERMS)
    return jnp.where(inside, jnp.where(rel_lane % 2 == 0, even_val, odd_val), 0.0)


def _bf16_pieces(x):
    pieces, rest = [], np.float64(x)
    for _ in range(ALIBI_TERMS):
        piece = np.float64(np.asarray(rest, np.float32).astype(jnp.bfloat16).astype(np.float32))
        pieces.append(float(piece))
        rest = rest - piece
    return pieces


def _query_alibi_row(slope, lane):
    row = jnp.zeros(lane.shape, F32)
    coeffs = ((LANE_POS, SEL_LEN * slope), (LANE_POS + 1, slope),
              (LANE_CMP, 16 * CMP_STRIDE * slope), (LANE_CMP + 1, CMP_STRIDE * slope))
    for lane0, coeff in coeffs:
        for t, piece in enumerate(_bf16_pieces(coeff * LOG2E)):
            row = jnp.where(lane == lane0 + 2 * t, piece, row)
    return row


def _proj_body(x_ref, g_ref, wa_ref, wc_ref, wq_ref, wkv_ref, wng_ref, wgab_ref, wvt_ref,
               a0_ref, a1_ref, a2_ref, kvc_ref, qb_ref, kvb_ref, ng_ref, gab_ref, vst_ref, vwt_ref, res_sc, mid_sc,
               *, tm, s_len):
    n = _rms(x_ref[...], g_ref[...]).astype(BF16)
    vt = _dot_nt(wvt_ref[...], n)
    for branch, (out_ref, width) in enumerate(((vst_ref, SLC_TILE), (vwt_ref, Q_BLK))):
        for u in range(tm // width):
            for g in range(B_KV_GROUPS):
                r0 = (branch * B_KV_GROUPS + g) * HEAD_DIM
                out_ref[u, g, :HEAD_DIM, :] = vt[r0:r0 + HEAD_DIM, u * width:(u + 1) * width].astype(BF16)
                out_ref[u, g, HEAD_DIM:, :] = jnp.ones((VT_ROWS - HEAD_DIM, width), BF16)
    res = _dot(n, wa_ref[...])
    (_, d0), (_, d1), (_, d2) = A_CONFIGS
    step = d2 // d1
    for s in range(res.shape[1] // LANES):
        cols = slice(s * LANES, (s + 1) * LANES)
        res_sc[s] = res[:, cols]
        a0_ref[0, :, cols] = res_sc[s].astype(BF16)
        for r in range(d1):
            part = res_sc[s, pl.ds(r, tm // d1, stride=d1), :]
            a1_ref[r, :, cols] = part.astype(BF16)
            mid_sc[r] = part
            for r2 in range(step):
                a2_ref[r + d1 * r2, :, cols] = mid_sc[r, pl.ds(r2, tm // d2, stride=step), :].astype(BF16)
    kvc_ref[...] = _dot(n, wc_ref[...])
    qb_ref[...] = _dot(n, wq_ref[...]).astype(BF16)
    ng_ref[...] = _dot(n, wng_ref[...])
    gab_ref[...] = _dot(n, wgab_ref[...]).astype(BF16)
    pos = (pl.program_id(0) * tm) % s_len + _iota((tm, LANES), 0)
    lane = _iota((tm, LANES), 1)
    posc = _pair_columns(lane - LANE_POS, (pos // SEL_LEN).astype(F32), (pos % SEL_LEN).astype(F32))
    kv = _dot(n, wkv_ref[...])
    for c in range(kv.shape[1] // LANES):
        kvb_ref[:, c * LANES:(c + 1) * LANES] = (kv[:, c * LANES:(c + 1) * LANES] + posc).astype(BF16)


def _proj(x2, g, wa, wc, wq, wkv, wng, wgab, wvt, *, s_len, tm=2 * SLC_TILE):
    t_len, d = x2.shape
    bsz = t_len // s_len
    nrt = s_len // tm
    const = lambda i: (0, 0)
    row = lambda i: (i, 0)
    ws = (wa, wc, wq, wkv, wng, wgab, wvt)
    flat = (wc, wq, wkv, wng, wgab)
    flat_dtypes = (F32, BF16, BF16, F32, BF16)
    aw = wa.shape[1]
    a_specs = [pl.BlockSpec((None, dil, tm // dil, aw), lambda i: (i // nrt, 0, i % nrt, 0))
               for _, dil in A_CONFIGS]
    a_shapes = [jax.ShapeDtypeStruct((bsz, dil, s_len // dil, aw), BF16) for _, dil in A_CONFIGS]
    return pl.pallas_call(
        functools.partial(_proj_body, tm=tm, s_len=s_len),
        grid=(t_len // tm,),
        in_specs=[pl.BlockSpec((tm, d), row), pl.BlockSpec((1, d), const)]
                 + [pl.BlockSpec(w.shape, const, pipeline_mode=pl.Buffered(1)) for w in ws],
        out_specs=a_specs + [pl.BlockSpec((tm, w.shape[1]), row) for w in flat]
                  + [pl.BlockSpec((None, tm // width, B_KV_GROUPS, VT_ROWS, width),
                                  lambda i: (i // nrt, i % nrt, 0, 0, 0)) for width in (SLC_TILE, Q_BLK)],
        out_shape=a_shapes + [jax.ShapeDtypeStruct((t_len, w.shape[1]), dt) for w, dt in zip(flat, flat_dtypes)]
                  + [jax.ShapeDtypeStruct((bsz, s_len // width, B_KV_GROUPS, VT_ROWS, width), BF16)
                     for width in (SLC_TILE, Q_BLK)],
        scratch_shapes=[pltpu.VMEM((aw // LANES, tm, LANES), F32),
                        pltpu.VMEM((A_CONFIGS[1][1], tm // A_CONFIGS[1][1], LANES), F32)],
        compiler_params=pltpu.CompilerParams(dimension_semantics=("arbitrary",),
                                             vmem_limit_bytes=VMEM_LIMIT),
        name="proj",
    )(x2, g, *ws)


def _gelu_tanh(x):
    return 0.5 * x * (1.0 + jnp.tanh(np.sqrt(2.0 / np.pi).astype(np.float32) * (x + 0.044715 * (x * x * x))))


def _compress_body(x_ref, pe_ref, w1_ref, w2_ref, out_ref, *, ncp):
    first = jnp.zeros((ncp, B_KV_GROUPS * CMP_HIDDEN), F32)
    second = jnp.zeros((ncp, B_KV_GROUPS * CMP_HIDDEN), F32)
    for l in range(CMP_STRIDE):
        x_l = x_ref[pl.ds(l, ncp, stride=CMP_STRIDE), :]
        first = first + _dot((x_l + pe_ref[l:l + 1, :]).astype(BF16), w1_ref[l])
        second = second + _dot((x_l + pe_ref[CMP_STRIDE + l:CMP_STRIDE + l + 1, :]).astype(BF16),
                               w1_ref[CMP_STRIDE + l])
    pre = first + jnp.concatenate([second[1:], second[:1]], axis=0)
    hid = _gelu_tanh(pre).astype(BF16)
    is_key = pl.program_id(1) == 0
    n_idx = _iota((ncp, LANES), 0)
    lane = _iota((ncp, LANES), 1)
    nc = _pair_columns(lane - LANE_CMP, (n_idx // 16).astype(F32), (n_idx % 16).astype(F32))
    nc = jnp.where(is_key, nc, 0.0)
    for g in range(B_KV_GROUPS):
        out = _dot(hid[:, g * CMP_HIDDEN:(g + 1) * CMP_HIDDEN], w2_ref[g])
        out_ref[g] = (out + nc).astype(BF16)


def _compress(kvc, pe2, w1bd, w2s, *, bsz, s_len):
    ncp = s_len // CMP_STRIDE
    return pl.pallas_call(
        functools.partial(_compress_body, ncp=ncp),
        grid=(bsz, 2),
        in_specs=[pl.BlockSpec((s_len, LANES), lambda b, kv: (b, kv)),
                  pl.BlockSpec((None,) + pe2.shape[1:], lambda b, kv: (kv, 0, 0)),
                  pl.BlockSpec((None,) + w1bd.shape[1:], lambda b, kv: (kv, 0, 0, 0)),
                  pl.BlockSpec((B_KV_GROUPS, CMP_HIDDEN, LANES), lambda b, kv: (kv, 0, 0))],
        out_specs=pl.BlockSpec((None, B_KV_GROUPS, ncp, LANES), lambda b, kv: (b, kv, 0, 0)),
        out_shape=jax.ShapeDtypeStruct((bsz, 2 * B_KV_GROUPS, ncp, LANES), BF16),
        compiler_params=pltpu.CompilerParams(dimension_semantics=("arbitrary", "arbitrary"),
                                             vmem_limit_bytes=VMEM_LIMIT),
        name="compress",
    )(kvc, pe2, w1bd, w2s)


def _head_slope(h, n_heads):
    out = jnp.float32(2.0 ** (-8.0 * n_heads / n_heads))
    for k in range(n_heads - 1):
        out = jnp.where(h == k, jnp.float32(2.0 ** (-8.0 * (k + 1) / n_heads)), out)
    return out


def _dilated_body(*refs):
    ncfg = len(A_CONFIGS)
    in_refs = refs[:5 * ncfg]
    out_ref = refs[5 * ncfg]
    o_sc, l_sc, m_sc, bias_sc = refs[5 * ncfg + 1:5 * ncfg + 5]
    kbufs = refs[5 * ncfg + 5:5 * ncfg + 5 + ncfg]
    vbufs = refs[5 * ncfg + 5 + ncfg:]
    hp = pl.program_id(1)
    first_tile = pl.program_id(2) == 0
    low = _iota((1, LANES), 1) < HEAD_DIM

    @pl.when(first_tile)
    def _():
        qi = _iota((Q_BLK, 2 * Q_BLK), 0)
        kj = _iota((Q_BLK, 2 * Q_BLK), 1)
        dist = qi - kj + Q_BLK
        for c, (window, dil) in enumerate(A_CONFIGS):
            valid = (dist >= 0) & (dist <= window // dil)
            for hh in range(2):
                slope = _head_slope(2 * hp + hh, A_HEADS)
                bias = jnp.where(valid, -(slope * dil * LOG2E) * dist.astype(F32), NEG)
                bias_sc[(c * 2 + hh) * 2] = bias
                bias_sc[(c * 2 + hh) * 2 + 1] = jnp.where(kj < Q_BLK, NEG, bias)

    for c, (window, dil) in enumerate(A_CONFIGS):
        q_ref, k_ref, v_ref, kh_ref, vh_ref = in_refs[5 * c:5 * c + 5]
        kbuf, vbuf = kbufs[c], vbufs[c]
        rows = A_TILE // dil
        nsub = rows // Q_BLK
        kbuf[:, :Q_BLK, :] = kh_ref[...]
        kbuf[:, Q_BLK:, :] = k_ref[...]
        vbuf[:, :Q_BLK, :] = vh_ref[...]
        vbuf[:, Q_BLK:, :] = v_ref[...]

        def group(gidx, carry, c=c, dil=dil, nsub=nsub, q_ref=q_ref, kbuf=kbuf, vbuf=vbuf):
            subs = []
            for u in range(A_GROUP):
                idx = gidx * A_GROUP + u
                r = idx // nsub
                j = idx % nsub
                j0 = pl.multiple_of(j * Q_BLK, Q_BLK)
                seq_start = ((j == 0) & first_tile).astype(jnp.int32)
                subs.append((r, j, j0, seq_start))
            scores = []
            for r, j, j0, seq_start in subs:
                q = q_ref[r, pl.ds(j0, Q_BLK), :]
                k2 = kbuf[r, pl.ds(j0, 2 * Q_BLK), :]
                for hh in range(2):
                    qm = jnp.where(low == (hh == 0), q, jnp.zeros_like(q))
                    scores.append(_dot_nt(qm, k2) + bias_sc[(c * 2 + hh) * 2 + seq_start])
            probs = []
            for s in scores:
                m = jnp.max(s, axis=-1, keepdims=True)
                e = jnp.exp2(s - m)
                probs.append((e.astype(BF16), m, jnp.sum(e, axis=-1, keepdims=True)))
            for u, (r, j, j0, seq_start) in enumerate(subs):
                v2 = vbuf[r, pl.ds(j0, 2 * Q_BLK), :]
                outs = [_dot(probs[2 * u + hh][0], v2) for hh in range(2)]
                stats = [[jnp.broadcast_to(probs[2 * u + hh][i], (Q_BLK, LANES)) for hh in range(2)] for i in (1, 2)]
                row0 = j * (Q_BLK * dil) + r
                dst = pl.ds(row0, Q_BLK, stride=dil) if dil > 1 else pl.ds(row0, Q_BLK)
                o_sc[c, dst, :] = jnp.where(low, outs[0], outs[1])
                m_sc[c, dst, :] = jnp.where(low, stats[0][0], stats[0][1])
                l_sc[c, dst, :] = jnp.where(low, stats[1][0], stats[1][1])
            return carry

        lax.fori_loop(0, dil * nsub // A_GROUP, group, 0)

    m = jnp.maximum(jnp.maximum(m_sc[0], m_sc[1]), m_sc[2])
    num = jnp.zeros((A_TILE, LANES), F32)
    den = jnp.zeros((A_TILE, LANES), F32)
    for c in range(ncfg):
        e = jnp.exp2(m_sc[c] - m)
        num = num + e * o_sc[c]
        den = den + e * l_sc[c]
    out_ref[...] = (num / den).astype(BF16)


def _dilated(qkv_by_cfg, *, bsz, s_len):
    nt = s_len // A_TILE
    npair = A_HEADS // 2
    ncfg = len(A_CONFIGS)
    in_specs, operands, kv_scratch = [], [], []
    for (window, dil), arr in zip(A_CONFIGS, qkv_by_cfg):
        rows = A_TILE // dil
        nsub = rows // Q_BLK
        cur = lambda off: (lambda b, h, t: (b, 0, t, off + h))
        halo = lambda off, nsub=nsub: (lambda b, h, t: (b, 0, jnp.maximum(t * nsub - 1, 0), off + h))
        in_specs += [pl.BlockSpec((None, dil, rows, LANES), cur(0)),
                     pl.BlockSpec((None, dil, rows, LANES), cur(npair)),
                     pl.BlockSpec((None, dil, rows, LANES), cur(2 * npair)),
                     pl.BlockSpec((None, dil, Q_BLK, LANES), halo(npair)),
                     pl.BlockSpec((None, dil, Q_BLK, LANES), halo(2 * npair))]
        operands += [arr] * 5
        kv_scratch.append(pltpu.VMEM((dil, Q_BLK + rows, LANES), BF16))
    return pl.pallas_call(
        _dilated_body,
        grid=(bsz, npair, nt),
        in_specs=in_specs,
        out_specs=pl.BlockSpec((A_TILE, LANES), lambda b, h, t: (b * nt + t, h)),
        out_shape=jax.ShapeDtypeStruct((bsz * s_len, A_HEADS * HEAD_DIM), BF16),
        scratch_shapes=[pltpu.VMEM((ncfg, A_TILE, LANES), F32)] * 3
                       + [pltpu.VMEM((ncfg * 4, Q_BLK, 2 * Q_BLK), F32)] + kv_scratch + kv_scratch,
        compiler_params=pltpu.CompilerParams(dimension_semantics=("arbitrary",) * 3,
                                             vmem_limit_bytes=VMEM_LIMIT),
        name="dilated",
    )(*operands)


def _q_aug(q_all, g, lane, low):
    rows = []
    for r in range(B_REP):
        h = g * B_REP + r
        blk = q_all[:, (h // 2) * LANES:(h // 2 + 1) * LANES]
        if h % 2 == 1:
            blk = pltpu.roll(blk, HEAD_DIM, axis=1)
        rows.append(jnp.where(low, blk, _query_alibi_row(2.0 ** (-8.0 * (h + 1) / B_HEADS), lane)))
    return jnp.concatenate(rows, axis=0).astype(BF16)


def _place_heads(per_head, low):
    placed = []
    for h, o in enumerate(per_head):
        placed.append(pltpu.roll(o, HEAD_DIM, axis=1) if h % 2 != h // B_REP else o)
    return [jnp.where(low, placed[2 * hp], placed[2 * hp + 1]) for hp in range(B_HEADS // 2)]


def _nsa_select_body(q_ref, cmp_ref, ng_ref, ovt_ref, gmap_ref, selb_ref, oc_ref, flags_ref, *, ncp, topk):
    q0 = pl.program_id(1) * Q_BLK
    lane = _iota((1, LANES), 1)
    low = lane < HEAD_DIM
    t_col = q0 + _iota((Q_BLK, 1), 0)
    t_row = q0 + _iota((1, Q_BLK), 1)
    nsel_pad = LANES
    groups = range(B_KV_GROUPS)

    cmp_end = CMP_STRIDE * _iota((1, ncp), 1) + (CMP_LEN - 1)
    mask_c = cmp_end <= t_col
    bias_c = jnp.where(mask_c, 0.0, NEG)
    blk_t = _iota((nsel_pad, 1), 0)
    allowed_t = blk_t * SEL_LEN <= t_row
    cur_t = t_row // SEL_LEN
    forced_t = (blk_t == 0) | (blk_t == cur_t) | (blk_t == cur_t - 1)
    blk_f = blk_t.astype(F32)

    q_all = q_ref[...].astype(F32)
    qg = [_q_aug(q_all, g, lane, low) for g in groups]
    scores = [_dot_nt(qg[g], cmp_ref[g]).reshape(B_REP, Q_BLK, ncp) + bias_c[None] for g in groups]
    has_key = (t_col >= CMP_LEN - 1)[None]
    probs = []
    for s in scores:
        e = jnp.exp2(s - jnp.max(s, axis=-1, keepdims=True))
        den = jnp.sum(e, axis=-1, keepdims=True)
        probs.append(e * jnp.where(has_key, 1.0 / den, 0.0))
    o_cmp = [_dot(probs[g].reshape(B_REP * Q_BLK, ncp).astype(BF16), cmp_ref[B_KV_GROUPS + g]) for g in groups]
    ranks = []
    for p in probs:
        psum = p[0] + p[1] + p[2] + p[3]
        p_hi = psum.astype(BF16)
        p_lo = (psum - p_hi.astype(F32)).astype(BF16)
        imp_t = _dot_nt(ovt_ref[...], p_hi) + _dot_nt(ovt_ref[...], p_lo)
        ranks.append(jnp.where(allowed_t, imp_t + jnp.where(forced_t, FORCE, 0.0), NEG))

    def pick_one(_, carry):
        out = []
        for rank, sel in carry:
            best = jnp.max(rank, axis=0, keepdims=True)
            cand = jnp.where(rank == best, blk_f, float(nsel_pad))
            idx = jnp.min(cand, axis=0, keepdims=True)
            pick = blk_f == idx
            out.append((jnp.where(pick, -3e38, rank), jnp.where(pick, 1.0, sel)))
        return tuple(out)

    picked = lax.fori_loop(0, topk, pick_one,
                           tuple((rank, jnp.zeros((nsel_pad, Q_BLK), F32)) for rank in ranks))
    ones = jnp.ones((8, Q_BLK), BF16)
    tile_cnt = jnp.zeros((8, LANES), F32)
    for g in groups:
        sel = jnp.where(allowed_t, picked[g][1], 0.0).T
        selb_ref[:, g * LANES:(g + 1) * LANES] = jnp.where(sel > 0.5, 0.0, -MASK_BIG).astype(BF16)
        per_block = _dot(ones, sel.astype(BF16))
        tile_cnt = tile_cnt + _dot(per_block.astype(BF16), gmap_ref[g])
    flags_ref[...] = (tile_cnt > 0.5).astype(jnp.int32)

    sig = jax.nn.sigmoid(ng_ref[...])
    per_head = []
    for h in range(B_HEADS):
        g, r = divmod(h, B_REP)
        per_head.append(sig[:, 3 * h:3 * h + 1] * o_cmp[g][r * Q_BLK:(r + 1) * Q_BLK])
    for hp, tile in enumerate(_place_heads(per_head, low)):
        oc_ref[:, hp * LANES:(hp + 1) * LANES] = tile


def _nsa_attend_body(lists_ref, cnts_ref, q_ref, ks0_ref, ks1_ref, kw0_ref, kw1_ref, vst_ref, vwt_ref,
                     selb_ref, oc_ref, ng_ref, oh_ref, out_ref, *, nq, ntile):
    step = pl.program_id(0) * nq + pl.program_id(1)
    q0 = pl.program_id(1) * Q_BLK
    lane = _iota((1, LANES), 1)
    low = lane < HEAD_DIM
    t_col = q0 + _iota((Q_BLK, 1), 0)
    t_row = q0 + _iota((1, Q_BLK), 1)
    win_keys = WIN + Q_BLK
    rows = B_REP * Q_BLK
    groups = range(B_KV_GROUPS)
    ks_refs = (ks0_ref, ks1_ref)
    kw_refs = (kw0_ref, kw1_ref)

    q_all = q_ref[...].astype(F32)
    qg = [_q_aug(q_all, g, lane, low) for g in groups]

    def heads_on_rows(o_t):
        o_t = jnp.concatenate([o_t, o_t], axis=0)
        return jnp.concatenate([o_t[:, r * Q_BLK:(r + 1) * Q_BLK].T for r in range(B_REP)], axis=0)

    def softmax_step(s, m_i):
        m_new = jnp.maximum(m_i, jnp.max(s, axis=0, keepdims=True))
        return jnp.exp2(s - m_new).astype(BF16), jnp.exp2(m_i - m_new), m_new

    q_slc = []
    for g in groups:
        sel_bias = selb_ref[:, g * LANES:(g + 1) * LANES]
        q_slc.append(jnp.concatenate([qg[g], jnp.concatenate([sel_bias] * B_REP, axis=0)], axis=1))

    def qk(g, kt):
        k0 = pl.multiple_of(jnp.minimum(kt, ntile - 1) * SLC_TILE, SLC_TILE)
        o0 = pl.multiple_of(kt * SLC_TILE, SLC_TILE)
        k_aug = jnp.concatenate([ks_refs[g][pl.ds(k0, SLC_TILE), :], oh_ref[pl.ds(o0, SLC_TILE), :]], axis=1)
        return _dot_nt(k_aug, q_slc[g])

    def pv(g, p, kt):
        return _dot(vst_ref[jnp.minimum(kt, ntile - 1), g], p)

    def trip(k, carry):
        work = [(g, lists_ref[(step * B_KV_GROUPS + g) * LIST_W + k * SLC_TRIP + u])
                for u in range(SLC_TRIP) for g in groups]
        scores = [qk(g, kt) for g, kt in work]
        state = list(carry)
        for (g, kt), s in zip(work, scores):
            m_i, acc = state[g]
            p, a, m_i = softmax_step(s, m_i)
            state[g] = (m_i, a * acc + pv(g, p, kt))
        return tuple(state)

    n_trips = jnp.maximum(cnts_ref[step * B_KV_GROUPS], cnts_ref[step * B_KV_GROUPS + 1]) // SLC_TRIP
    init = tuple((jnp.full((1, rows), -3e38, F32), jnp.zeros((VT_ROWS, rows), F32)) for g in groups)
    state = lax.fori_loop(0, n_trips, trip, init)

    kt_diag = q0 // SLC_TILE
    key_pos = kt_diag * SLC_TILE + _iota((SLC_TILE, 1), 0)
    causal = jnp.where(key_pos <= t_row, 0.0, NEG)
    causal = jnp.concatenate([causal] * B_REP, axis=1)
    scores = [qk(g, kt_diag) + causal for g in groups]
    o_slc = []
    for g in groups:
        m_i, acc = state[g]
        p, a, m_i = softmax_step(scores[g], m_i)
        acc = a * acc + pv(g, p, kt_diag)
        o_slc.append(heads_on_rows(acc[:HEAD_DIM] / acc[HEAD_DIM:HEAD_DIM + 1]))

    kstart = pl.multiple_of(jnp.maximum(q0 - WIN, 0), Q_BLK)
    d_w = t_row - (kstart + _iota((win_keys, 1), 0))
    bias_w = jnp.where((d_w >= 0) & (d_w < WIN), 0.0, NEG)
    bias_w = jnp.concatenate([bias_w] * B_REP, axis=1)
    scores = [_dot_nt(kw_refs[g][pl.ds(kstart, win_keys), :], qg[g]) + bias_w for g in groups]
    o_win = []
    for g in groups:
        s = scores[g]
        p = jnp.exp2(s - jnp.max(s, axis=0, keepdims=True)).astype(BF16)
        v_t = jnp.concatenate([vwt_ref[kstart // Q_BLK + u, g] for u in range(win_keys // Q_BLK)], axis=1)
        acc = _dot(v_t, p)
        o_win.append(heads_on_rows(acc[:HEAD_DIM] / acc[HEAD_DIM:HEAD_DIM + 1]))

    sig = jax.nn.sigmoid(ng_ref[...])
    per_head = []
    for h in range(B_HEADS):
        g, r = divmod(h, B_REP)
        rs = slice(r * Q_BLK, (r + 1) * Q_BLK)
        per_head.append(sig[:, 3 * h + 1:3 * h + 2] * o_slc[g][rs] + sig[:, 3 * h + 2:3 * h + 3] * o_win[g][rs])
    for hp, tile in enumerate(_place_heads(per_head, low)):
        cols = slice(hp * LANES, (hp + 1) * LANES)
        out_ref[:, cols] = (oc_ref[:, cols] + tile).astype(BF16)


def _tile_lists(flags, *, bsz, s_len):
    nq = s_len // Q_BLK
    ntile = s_len // SLC_TILE
    f = flags[:, 0, :B_KV_GROUPS * FLAG_W].reshape(bsz * nq, B_KV_GROUPS, FLAG_W)[:, :, :ntile] > 0
    kt = jnp.arange(ntile, dtype=jnp.int32)
    diag = jnp.tile((jnp.arange(nq, dtype=jnp.int32) * Q_BLK) // SLC_TILE, bsz)[:, None, None]
    touched = f & (kt < diag)
    rank = jnp.cumsum(touched, axis=-1, dtype=jnp.int32) - 1
    n_touched = rank[..., -1] + 1
    cnt = SLC_TRIP * ((n_touched + SLC_TRIP - 1) // SLC_TRIP)
    pos = jnp.arange(LIST_W, dtype=jnp.int32)
    hit = touched[..., None, :] & (rank[..., None, :] == pos[:, None])
    order = jnp.sum(jnp.where(hit, kt, 0), axis=-1)
    lists = jnp.where(pos < n_touched[..., None], order, ntile)
    return lists.reshape(-1), cnt.reshape(-1)


def _nsa2(qb, kvb, vst, vwt, cmp, ng, onehot, ovt, gmap, *, bsz, s_len):
    nq = s_len // Q_BLK
    ntile = s_len // SLC_TILE
    ncp = cmp.shape[2]
    topk = min(SEL_TOPK, s_len // SEL_LEN)
    hw = B_HEADS * HEAD_DIM
    rowblk = lambda b, i: (b * nq + i, 0)
    params = pltpu.CompilerParams(dimension_semantics=("arbitrary", "arbitrary"), vmem_limit_bytes=VMEM_LIMIT)
    selb, oc, flags = pl.pallas_call(
        functools.partial(_nsa_select_body, ncp=ncp, topk=topk),
        grid=(bsz, nq),
        in_specs=[pl.BlockSpec((Q_BLK, hw), rowblk),
                  pl.BlockSpec((None, 2 * B_KV_GROUPS, ncp, LANES), lambda b, i: (b, 0, 0, 0)),
                  pl.BlockSpec((Q_BLK, LANES), rowblk),
                  pl.BlockSpec(ovt.shape, lambda b, i: (0, 0)),
                  pl.BlockSpec(gmap.shape, lambda b, i: (0, 0, 0))],
        out_specs=[pl.BlockSpec((Q_BLK, B_KV_GROUPS * LANES), rowblk),
                   pl.BlockSpec((Q_BLK, hw), rowblk),
                   pl.BlockSpec((None, 8, LANES), lambda b, i: (b * nq + i, 0, 0))],
        out_shape=[jax.ShapeDtypeStruct((bsz * s_len, B_KV_GROUPS * LANES), BF16),
                   jax.ShapeDtypeStruct((bsz * s_len, hw), F32),
                   jax.ShapeDtypeStruct((bsz * nq, 8, LANES), jnp.int32)],
        compiler_params=params,
        name="nsa_select",
    )(qb, cmp, ng, ovt, gmap)

    lists, cnts = _tile_lists(flags, bsz=bsz, s_len=s_len)
    rowblk2 = lambda b, i, lists, cnts: (b * nq + i, 0)
    res = lambda col: (lambda b, i, lists, cnts: (b, col))
    return pl.pallas_call(
        functools.partial(_nsa_attend_body, nq=nq, ntile=ntile),
        grid_spec=pltpu.PrefetchScalarGridSpec(
            num_scalar_prefetch=2,
            grid=(bsz, nq),
            in_specs=[pl.BlockSpec((Q_BLK, hw), rowblk2)]
                     + [pl.BlockSpec((s_len, LANES), res(col)) for col in range(2 * B_KV_GROUPS)]
                     + [pl.BlockSpec((None,) + v.shape[1:], lambda b, i, lists, cnts: (b, 0, 0, 0, 0))
                        for v in (vst, vwt)]
                     + [pl.BlockSpec((Q_BLK, B_KV_GROUPS * LANES), rowblk2),
                        pl.BlockSpec((Q_BLK, hw), rowblk2),
                        pl.BlockSpec((Q_BLK, LANES), rowblk2),
                        pl.BlockSpec(onehot.shape, lambda b, i, lists, cnts: (0, 0))],
            out_specs=pl.BlockSpec((Q_BLK, hw), rowblk2)),
        out_shape=jax.ShapeDtypeStruct((bsz * s_len, hw), BF16),
        compiler_params=params,
        name="nsa_attend",
    )(lists, cnts, qb, kvb, kvb, kvb, kvb, vst, vwt, selb, oc, ng, onehot)


def _post_body(x_ref, oa_ref, ob_ref, gab_ref, p_ref, wua_ref, wub_ref, wout_ref, g2_ref, w1_ref, w2_ref,
               g3_ref, wpg_ref, wple_ref, gf_ref, out_ref, *, d, ff_chunk):
    ya = _dot(oa_ref[...], wua_ref[...])
    yb = _dot(ob_ref[...], wub_ref[...])
    mixed = (jax.nn.sigmoid(gab_ref[:, :d].astype(F32)) * ya
             + jax.nn.sigmoid(gab_ref[:, d:].astype(F32)) * yb)
    h = x_ref[...] + _dot(mixed.astype(BF16), wout_ref[...])
    n2 = _rms(h, g2_ref[...]).astype(BF16)
    acc = h
    for c in range(w1_ref.shape[1] // ff_chunk):
        cs = slice(c * ff_chunk, (c + 1) * ff_chunk)
        hid = jnp.square(jnp.maximum(_dot(n2, w1_ref[:, cs]), 0.0))
        acc = acc + _dot(hid.astype(BF16), w2_ref[cs, :])
    n3 = _rms(acc, g3_ref[...]).astype(BF16)
    gate = jax.nn.sigmoid(_dot(n3, wpg_ref[...]))
    h3 = acc + gate * _dot(p_ref[...].astype(BF16), wple_ref[...])
    out_ref[...] = _rms(h3, gf_ref[...])


def _post(x2, oa, ob, gab, p2, wua, wub, wout, g2, w1, w2, g3, wpg, wple, gf, *, tm=512, ff_chunk=1024):
    t_len, d = x2.shape
    row = lambda i: (i, 0)
    const = lambda i: (0, 0)
    resident = lambda a: pl.BlockSpec(a.shape, const, pipeline_mode=pl.Buffered(1))
    acts = (x2, oa, ob, gab, p2)
    params = (wua, wub, wout, g2, w1, w2, g3, wpg, wple, gf)
    return pl.pallas_call(
        functools.partial(_post_body, d=d, ff_chunk=ff_chunk),
        grid=(t_len // tm,),
        in_specs=[pl.BlockSpec((tm, a.shape[1]), row) for a in acts] + [resident(w) for w in params],
        out_specs=pl.BlockSpec((tm, d), row),
        out_shape=jax.ShapeDtypeStruct((t_len, d), F32),
        compiler_params=pltpu.CompilerParams(dimension_semantics=("arbitrary",),
                                             vmem_limit_bytes=VMEM_LIMIT),
        name="post",
    )(*acts, *params)


def _selection_overlap_t(ncp, s_len):
    ncmp = (s_len - CMP_LEN) // CMP_STRIDE + 1
    nsel = s_len // SEL_LEN
    ratio = SEL_LEN // CMP_STRIDE
    span = CMP_LEN // CMP_STRIDE
    i = np.arange(ncmp)[:, None]
    j = np.arange(nsel)[None, :]
    ov = np.maximum(np.minimum(i + span, ratio * (j + 1)) - np.maximum(i, ratio * j), 0)
    out = np.zeros((LANES, ncp), np.float32)
    out[:nsel, :ncmp] = ov.T
    return out


def _layer(h, p_i, norm_mix_g, w_in, pe_ck, w_ck1, w_ck2, pe_cv, w_cv1, w_cv2, w_up_a, w_up_b, w_out,
           norm_mlp_g, w_mlp1, w_mlp2, norm_ple_g, w_ple_gate, w_ple, final_g):
    bsz, s_len, d = h.shape
    t_len = bsz * s_len
    aw = A_HEADS * HEAD_DIM
    bw = B_HEADS * HEAD_DIM
    kvw = B_KV_GROUPS * HEAD_DIM
    assert s_len % A_TILE == 0 and s_len // SEL_LEN <= LANES and kvw == LANES

    o_qb = 3 * aw
    o_kv = o_qb + bw
    o_ng = o_kv + 6 * kvw
    o_ga = o_ng + 3 * B_HEADS
    kv = lambda i: w_in[:, o_kv + i * kvw:o_kv + (i + 1) * kvw]
    zeros_h = jnp.zeros((d, HEAD_DIM), w_in.dtype)
    grp = lambda w, g: jnp.concatenate([w[:, g * HEAD_DIM:(g + 1) * HEAD_DIM], zeros_h], axis=1)
    wa = jnp.concatenate([w_in[:, :aw] * (SCALE * LOG2E), w_in[:, aw:3 * aw]], axis=1)
    wc = jnp.concatenate([kv(0), kv(1)], axis=1)
    wq = w_in[:, o_qb:o_qb + bw] * (SCALE * LOG2E)
    wkv = jnp.concatenate([grp(kv(2), 0), grp(kv(2), 1), grp(kv(4), 0), grp(kv(4), 1)], axis=1)
    wvt = jnp.concatenate([kv(3), kv(5)], axis=1).T
    wng = jnp.concatenate([w_in[:, o_ng:o_ga], jnp.zeros((d, LANES - 3 * B_HEADS), w_in.dtype)], axis=1)
    wgab = w_in[:, o_ga:]
    x2 = h.reshape(t_len, d)
    a0, a1, a2, kvc, qb, kvb, ng, gab, vst, vwt = _proj(x2, norm_mix_g.reshape(1, d), *(w.astype(BF16) for w in
                                                        (wa, wc, wq, wkv, wng, wgab, wvt)), s_len=s_len)

    ncp = s_len // CMP_STRIDE
    pe2 = jnp.stack([jnp.tile(pe, (1, B_KV_GROUPS)) for pe in (pe_ck, pe_cv)])

    def per_row_block_diag(w1):
        w = w1.reshape(CMP_LEN, HEAD_DIM, CMP_HIDDEN)
        z = jnp.zeros_like(w)
        return jnp.concatenate([jnp.concatenate([w, z], axis=2), jnp.concatenate([z, w], axis=2)], axis=1)

    w1bd = jnp.stack([per_row_block_diag(w_ck1), per_row_block_diag(w_cv1)]).astype(BF16)
    zpad = jnp.zeros((CMP_HIDDEN, HEAD_DIM), w_ck2.dtype)
    w2s = jnp.stack([jnp.concatenate([w_ck2, zpad], axis=1), jnp.concatenate([w_ck2, zpad], axis=1),
                     jnp.concatenate([w_cv2, zpad], axis=1), jnp.concatenate([zpad, w_cv2], axis=1)]).astype(BF16)
    cmp = _compress(kvc, pe2, w1bd, w2s, bsz=bsz, s_len=s_len)

    oa = _dilated((a0, a1, a2), bsz=bsz, s_len=s_len)
    onehot = (np.arange(s_len + SLC_TILE)[:, None] // SEL_LEN == np.arange(LANES)[None, :])
    onehot[s_len:] = True
    blocks_per_tile = SLC_TILE // SEL_LEN
    gmap = np.zeros((B_KV_GROUPS, LANES, LANES), np.float32)
    for g in range(B_KV_GROUPS):
        gmap[g, np.arange(LANES), FLAG_W * g + np.arange(LANES) // blocks_per_tile] = 1.0
    ob = _nsa2(qb, kvb, vst, vwt, cmp, ng, jnp.asarray(onehot, BF16),
               jnp.asarray(_selection_overlap_t(ncp, s_len), BF16), jnp.asarray(gmap, BF16), bsz=bsz, s_len=s_len)

    b16 = lambda w: w.astype(BF16)
    row = lambda v: v.reshape(1, d)
    return _post(x2, oa, ob, gab, p_i.reshape(t_len, -1), b16(w_up_a), b16(w_up_b), b16(w_out), row(norm_mlp_g),
                 b16(w_mlp1), b16(w_mlp2), row(norm_ple_g), b16(w_ple_gate), b16(w_ple), row(final_g)
                 ).reshape(bsz, s_len, d)


def kernel(x, p, norm_mix_g, w_in, pe_ck, w_ck1, w_ck2, pe_cv, w_cv1, w_cv2, w_up_a, w_up_b, w_out,
           norm_mlp_g, w_mlp1, w_mlp2, norm_ple_g, w_ple_gate, w_ple, norm_final_g):
    depth = w_in.shape[0]
    assert depth == 1, "the fused tail applies the final norm inside the single layer"
    return _layer(x, p[0], norm_mix_g[0], w_in[0], pe_ck[0], w_ck1[0], w_ck2[0], pe_cv[0], w_cv1[0], w_cv2[0],
                  w_up_a[0], w_up_b[0], w_out[0], norm_mlp_g[0], w_mlp1[0], w_mlp2[0], norm_ple_g[0],
                  w_ple_gate[0], w_ple[0], norm_final_g)
```

```python
import functools

import numpy as np
import jax
import jax.numpy as jnp
from jax import lax
from jax.experimental import pallas as pl
from jax.experimental.pallas import tpu as pltpu

HEAD_DIM = 64
A_HEADS = 8
A_CONFIGS = ((128, 1), (512, 4), (2048, 16))
B_HEADS = 8
B_KV_GROUPS = 2
B_REP = B_HEADS // B_KV_GROUPS
CMP_LEN = 32
CMP_STRIDE = 16
CMP_HIDDEN = 256
SEL_LEN = 64
SEL_TOPK = 16
WIN = 512
Q_BLK = 128
EPS = 1e-6
NEG = -1e30
FORCE = 1e9
MASK_BIG = 2.0 ** 100
SCALE = HEAD_DIM ** -0.5

LANES = 128
A_TILE = 2048
A_GROUP = 8
SLC_TILE = 256
VT_ROWS = HEAD_DIM + 16
FLAG_W = 32
TOPK_SPAN = 2048
SLC_TRIP = 3
LIST_W = FLAG_W + SLC_TRIP
VMEM_LIMIT = 56 * 1024 * 1024

ALIBI_TERMS = 3
LANE_POS = 64
LANE_CMP = LANE_POS + 2 * ALIBI_TERMS
LOG2E = 1.4426950408889634

F32 = jnp.float32
BF16 = jnp.bfloat16


def _dot(a, b):
    return jnp.dot(a, b, preferred_element_type=F32)


def _dot_nt(a, b):
    return lax.dot_general(a, b, (((1,), (1,)), ((), ())), preferred_element_type=F32)


def _rms(x, g):
    inv = lax.rsqrt(jnp.mean(x * x, axis=-1, keepdims=True) + EPS)
    return (x * inv) * g


def _iota(shape, dim, dtype=jnp.int32):
    return lax.broadcasted_iota(dtype, shape, dim)


def _pair_columns(rel_lane, even_val, odd_val):
    inside = (rel_lane >= 0) & (rel_lane < 2 * ALIBI_TERMS)
    return jnp.where(inside, jnp.where(rel_lane % 2 == 0, even_val, odd_val), 0.0)


def _bf16_pieces(x):
    pieces, rest = [], np.float64(x)
    for _ in range(ALIBI_TERMS):
        piece = np.float64(np.asarray(rest, np.float32).astype(jnp.bfloat16).astype(np.float32))
        pieces.append(float(piece))
        rest = rest - piece
    return pieces


def _query_alibi_row(slope, lane):
    row = jnp.zeros(lane.shape, F32)
    coeffs = ((LANE_POS, SEL_LEN * slope), (LANE_POS + 1, slope),
              (LANE_CMP, 16 * CMP_STRIDE * slope), (LANE_CMP + 1, CMP_STRIDE * slope))
    for lane0, coeff in coeffs:
        for t, piece in enumerate(_bf16_pieces(coeff * LOG2E)):
            row = jnp.where(lane == lane0 + 2 * t, piece, row)
    return row


def _proj_body(x_ref, g_ref, wa_ref, wc_ref, wq_ref, wkv_ref, wng_ref, wgab_ref, wvt_ref,
               a0_ref, a1_ref, a2_ref, kvc_ref, qb_ref, kvb_ref, ng_ref, gab_ref, vst_ref, vwt_ref, res_sc, mid_sc,
               *, tm, s_len):
    n = _rms(x_ref[...], g_ref[...]).astype(BF16)
    vt = _dot_nt(wvt_ref[...], n)
    for branch, (out_ref, width) in enumerate(((vst_ref, SLC_TILE), (vwt_ref, Q_BLK))):
        for u in range(tm // width):
            for g in range(B_KV_GROUPS):
                r0 = (branch * B_KV_GROUPS + g) * HEAD_DIM
                out_ref[u, g, :HEAD_DIM, :] = vt[r0:r0 + HEAD_DIM, u * width:(u + 1) * width].astype(BF16)
                out_ref[u, g, HEAD_DIM:, :] = jnp.ones((VT_ROWS - HEAD_DIM, width), BF16)
    res = _dot(n, wa_ref[...])
    (_, d0), (_, d1), (_, d2) = A_CONFIGS
    step = d2 // d1
    for s in range(res.shape[1] // LANES):
        cols = slice(s * LANES, (s + 1) * LANES)
        res_sc[s] = res[:, cols]
        a0_ref[0, :, cols] = res_sc[s].astype(BF16)
        for r in range(d1):
            part = res_sc[s, pl.ds(r, tm // d1, stride=d1), :]
            a1_ref[r, :, cols] = part.astype(BF16)
            mid_sc[r] = part
            for r2 in range(step):
                a2_ref[r + d1 * r2, :, cols] = mid_sc[r, pl.ds(r2, tm // d2, stride=step), :].astype(BF16)
    kvc_ref[...] = _dot(n, wc_ref[...])
    qb_ref[...] = _dot(n, wq_ref[...]).astype(BF16)
    ng_ref[...] = _dot(n, wng_ref[...])
    gab_ref[...] = _dot(n, wgab_ref[...]).astype(BF16)
    pos = (pl.program_id(0) * tm) % s_len + _iota((tm, LANES), 0)
    lane = _iota((tm, LANES), 1)
    posc = _pair_columns(lane - LANE_POS, (pos // SEL_LEN).astype(F32), (pos % SEL_LEN).astype(F32))
    kv = _dot(n, wkv_ref[...])
    for c in range(kv.shape[1] // LANES):
        kvb_ref[:, c * LANES:(c + 1) * LANES] = (kv[:, c * LANES:(c + 1) * LANES] + posc).astype(BF16)


def _proj(x2, g, wa, wc, wq, wkv, wng, wgab, wvt, *, s_len, tm=2 * SLC_TILE):
    t_len, d = x2.shape
    bsz = t_len // s_len
    nrt = s_len // tm
    const = lambda i: (0, 0)
    row = lambda i: (i, 0)
    ws = (wa, wc, wq, wkv, wng, wgab, wvt)
    flat = (wc, wq, wkv, wng, wgab)
    flat_dtypes = (F32, BF16, BF16, F32, BF16)
    aw = wa.shape[1]
    a_specs = [pl.BlockSpec((None, dil, tm // dil, aw), lambda i: (i // nrt, 0, i % nrt, 0))
               for _, dil in A_CONFIGS]
    a_shapes = [jax.ShapeDtypeStruct((bsz, dil, s_len // dil, aw), BF16) for _, dil in A_CONFIGS]
    return pl.pallas_call(
        functools.partial(_proj_body, tm=tm, s_len=s_len),
        grid=(t_len // tm,),
        in_specs=[pl.BlockSpec((tm, d), row), pl.BlockSpec((1, d), const)]
                 + [pl.BlockSpec(w.shape, const, pipeline_mode=pl.Buffered(1)) for w in ws],
        out_specs=a_specs + [pl.BlockSpec((tm, w.shape[1]), row) for w in flat]
                  + [pl.BlockSpec((None, tm // width, B_KV_GROUPS, VT_ROWS, width),
                                  lambda i: (i // nrt, i % nrt, 0, 0, 0)) for width in (SLC_TILE, Q_BLK)],
        out_shape=a_shapes + [jax.ShapeDtypeStruct((t_len, w.shape[1]), dt) for w, dt in zip(flat, flat_dtypes)]
                  + [jax.ShapeDtypeStruct((bsz, s_len // width, B_KV_GROUPS, VT_ROWS, width), BF16)
                     for width in (SLC_TILE, Q_BLK)],
        scratch_shapes=[pltpu.VMEM((aw // LANES, tm, LANES), F32),
                        pltpu.VMEM((A_CONFIGS[1][1], tm // A_CONFIGS[1][1], LANES), F32)],
        compiler_params=pltpu.CompilerParams(dimension_semantics=("arbitrary",),
                                             vmem_limit_bytes=VMEM_LIMIT),
        name="proj",
    )(x2, g, *ws)


def _gelu_tanh(x):
    return 0.5 * x * (1.0 + jnp.tanh(np.sqrt(2.0 / np.pi).astype(np.float32) * (x + 0.044715 * (x * x * x))))


def _compress_body(x_ref, pe_ref, w1_ref, w2_ref, out_ref, *, ncp):
    first = jnp.zeros((ncp, B_KV_GROUPS * CMP_HIDDEN), F32)
    second = jnp.zeros((ncp, B_KV_GROUPS * CMP_HIDDEN), F32)
    for l in range(CMP_STRIDE):
        x_l = x_ref[pl.ds(l, ncp, stride=CMP_STRIDE), :]
        first = first + _dot((x_l + pe_ref[l:l + 1, :]).astype(BF16), w1_ref[l])
        second = second + _dot((x_l + pe_ref[CMP_STRIDE + l:CMP_STRIDE + l + 1, :]).astype(BF16),
                               w1_ref[CMP_STRIDE + l])
    pre = first + jnp.concatenate([second[1:], second[:1]], axis=0)
    hid = _gelu_tanh(pre).astype(BF16)
    is_key = pl.program_id(1) == 0
    n_idx = _iota((ncp, LANES), 0)
    lane = _iota((ncp, LANES), 1)
    nc = _pair_columns(lane - LANE_CMP, (n_idx // 16).astype(F32), (n_idx % 16).astype(F32))
    nc = jnp.where(is_key, nc, 0.0)
    for g in range(B_KV_GROUPS):
        out = _dot(hid[:, g * CMP_HIDDEN:(g + 1) * CMP_HIDDEN], w2_ref[g])
        out_ref[g] = (out + nc).astype(BF16)


def _compress(kvc, pe2, w1bd, w2s, *, bsz, s_len):
    ncp = s_len // CMP_STRIDE
    return pl.pallas_call(
        functools.partial(_compress_body, ncp=ncp),
        grid=(bsz, 2),
        in_specs=[pl.BlockSpec((s_len, LANES), lambda b, kv: (b, kv)),
                  pl.BlockSpec((None,) + pe2.shape[1:], lambda b, kv: (kv, 0, 0)),
                  pl.BlockSpec((None,) + w1bd.shape[1:], lambda b, kv: (kv, 0, 0, 0)),
                  pl.BlockSpec((B_KV_GROUPS, CMP_HIDDEN, LANES), lambda b, kv: (kv, 0, 0))],
        out_specs=pl.BlockSpec((None, B_KV_GROUPS, ncp, LANES), lambda b, kv: (b, kv, 0, 0)),
        out_shape=jax.ShapeDtypeStruct((bsz, 2 * B_KV_GROUPS, ncp, LANES), BF16),
        compiler_params=pltpu.CompilerParams(dimension_semantics=("arbitrary", "arbitrary"),
                                             vmem_limit_bytes=VMEM_LIMIT),
        name="compress",
    )(kvc, pe2, w1bd, w2s)


def _head_slope(h, n_heads):
    out = jnp.float32(2.0 ** (-8.0 * n_heads / n_heads))
    for k in range(n_heads - 1):
        out = jnp.where(h == k, jnp.float32(2.0 ** (-8.0 * (k + 1) / n_heads)), out)
    return out


def _dilated_body(*refs):
    ncfg = len(A_CONFIGS)
    in_refs = refs[:5 * ncfg]
    out_ref = refs[5 * ncfg]
    o_sc, l_sc, m_sc, bias_sc = refs[5 * ncfg + 1:5 * ncfg + 5]
    kbufs = refs[5 * ncfg + 5:5 * ncfg + 5 + ncfg]
    vbufs = refs[5 * ncfg + 5 + ncfg:]
    hp = pl.program_id(1)
    first_tile = pl.program_id(2) == 0
    low = _iota((1, LANES), 1) < HEAD_DIM

    @pl.when(first_tile)
    def _():
        qi = _iota((Q_BLK, 2 * Q_BLK), 0)
        kj = _iota((Q_BLK, 2 * Q_BLK), 1)
        dist = qi - kj + Q_BLK
        for c, (window, dil) in enumerate(A_CONFIGS):
            valid = (dist >= 0) & (dist <= window // dil)
            for hh in range(2):
                slope = _head_slope(2 * hp + hh, A_HEADS)
                bias = jnp.where(valid, -(slope * dil * LOG2E) * dist.astype(F32), NEG)
                bias_sc[(c * 2 + hh) * 2] = bias
                bias_sc[(c * 2 + hh) * 2 + 1] = jnp.where(kj < Q_BLK, NEG, bias)

    for c, (window, dil) in enumerate(A_CONFIGS):
        q_ref, k_ref, v_ref, kh_ref, vh_ref = in_refs[5 * c:5 * c + 5]
        kbuf, vbuf = kbufs[c], vbufs[c]
        rows = A_TILE // dil
        nsub = rows // Q_BLK
        kbuf[:, :Q_BLK, :] = kh_ref[...]
        kbuf[:, Q_BLK:, :] = k_ref[...]
        vbuf[:, :Q_BLK, :] = vh_ref[...]
        vbuf[:, Q_BLK:, :] = v_ref[...]

        def group(gidx, carry, c=c, dil=dil, nsub=nsub, q_ref=q_ref, kbuf=kbuf, vbuf=vbuf):
            subs = []
            for u in range(A_GROUP):
                idx = gidx * A_GROUP + u
                r = idx // nsub
                j = idx % nsub
                j0 = pl.multiple_of(j * Q_BLK, Q_BLK)
                seq_start = ((j == 0) & first_tile).astype(jnp.int32)
                subs.append((r, j, j0, seq_start))
            scores = []
            for r, j, j0, seq_start in subs:
                q = q_ref[r, pl.ds(j0, Q_BLK), :]
                k2 = kbuf[r, pl.ds(j0, 2 * Q_BLK), :]
                for hh in range(2):
                    qm = jnp.where(low == (hh == 0), q, jnp.zeros_like(q))
                    scores.append(_dot_nt(qm, k2) + bias_sc[(c * 2 + hh) * 2 + seq_start])
            probs = []
            for s in scores:
                m = jnp.max(s, axis=-1, keepdims=True)
                e = jnp.exp2(s - m)
                probs.append((e.astype(BF16), m, jnp.sum(e, axis=-1, keepdims=True)))
            for u, (r, j, j0, seq_start) in enumerate(subs):
                v2 = vbuf[r, pl.ds(j0, 2 * Q_BLK), :]
                outs = [_dot(probs[2 * u + hh][0], v2) for hh in range(2)]
                stats = [[jnp.broadcast_to(probs[2 * u + hh][i], (Q_BLK, LANES)) for hh in range(2)] for i in (1, 2)]
                row0 = j * (Q_BLK * dil) + r
                dst = pl.ds(row0, Q_BLK, stride=dil) if dil > 1 else pl.ds(row0, Q_BLK)
                o_sc[c, dst, :] = jnp.where(low, outs[0], outs[1])
                m_sc[c, dst, :] = jnp.where(low, stats[0][0], stats[0][1])
                l_sc[c, dst, :] = jnp.where(low, stats[1][0], stats[1][1])
            return carry

        lax.fori_loop(0, dil * nsub // A_GROUP, group, 0)

    m = jnp.maximum(jnp.maximum(m_sc[0], m_sc[1]), m_sc[2])
    num = jnp.zeros((A_TILE, LANES), F32)
    den = jnp.zeros((A_TILE, LANES), F32)
    for c in range(ncfg):
        e = jnp.exp2(m_sc[c] - m)
        num = num + e * o_sc[c]
        den = den + e * l_sc[c]
    out_ref[...] = (num / den).astype(BF16)


def _dilated(qkv_by_cfg, *, bsz, s_len):
    nt = s_len // A_TILE
    npair = A_HEADS // 2
    ncfg = len(A_CONFIGS)
    in_specs, operands, kv_scratch = [], [], []
    for (window, dil), arr in zip(A_CONFIGS, qkv_by_cfg):
        rows = A_TILE // dil
        nsub = rows // Q_BLK
        cur = lambda off: (lambda b, h, t: (b, 0, t, off + h))
        halo = lambda off, nsub=nsub: (lambda b, h, t: (b, 0, jnp.maximum(t * nsub - 1, 0), off + h))
        in_specs += [pl.BlockSpec((None, dil, rows, LANES), cur(0)),
                     pl.BlockSpec((None, dil, rows, LANES), cur(npair)),
                     pl.BlockSpec((None, dil, rows, LANES), cur(2 * npair)),
                     pl.BlockSpec((None, dil, Q_BLK, LANES), halo(npair)),
                     pl.BlockSpec((None, dil, Q_BLK, LANES), halo(2 * npair))]
        operands += [arr] * 5
        kv_scratch.append(pltpu.VMEM((dil, Q_BLK + rows, LANES), BF16))
    return pl.pallas_call(
        _dilated_body,
        grid=(bsz, npair, nt),
        in_specs=in_specs,
        out_specs=pl.BlockSpec((A_TILE, LANES), lambda b, h, t: (b * nt + t, h)),
        out_shape=jax.ShapeDtypeStruct((bsz * s_len, A_HEADS * HEAD_DIM), BF16),
        scratch_shapes=[pltpu.VMEM((ncfg, A_TILE, LANES), F32)] * 3
                       + [pltpu.VMEM((ncfg * 4, Q_BLK, 2 * Q_BLK), F32)] + kv_scratch + kv_scratch,
        compiler_params=pltpu.CompilerParams(dimension_semantics=("arbitrary",) * 3,
                                             vmem_limit_bytes=VMEM_LIMIT),
        name="dilated",
    )(*operands)


def _q_aug(q_all, g, lane, low):
    rows = []
    for r in range(B_REP):
        h = g * B_REP + r
        blk = q_all[:, (h // 2) * LANES:(h // 2 + 1) * LANES]
        if h % 2 == 1:
            blk = pltpu.roll(blk, HEAD_DIM, axis=1)
        rows.append(jnp.where(low, blk, _query_alibi_row(2.0 ** (-8.0 * (h + 1) / B_HEADS), lane)))
    return jnp.concatenate(rows, axis=0).astype(BF16)


def _place_heads(per_head, low):
    placed = []
    for h, o in enumerate(per_head):
        placed.append(pltpu.roll(o, HEAD_DIM, axis=1) if h % 2 != h // B_REP else o)
    return [jnp.where(low, placed[2 * hp], placed[2 * hp + 1]) for hp in range(B_HEADS // 2)]


def _nsa_select_body(q_ref, cmp_ref, ng_ref, ovt_ref, gmap_ref, selb_ref, oc_ref, flags_ref, *, ncp, topk):
    q0 = pl.program_id(1) * Q_BLK
    lane = _iota((1, LANES), 1)
    low = lane < HEAD_DIM
    t_col = q0 + _iota((Q_BLK, 1), 0)
    t_row = q0 + _iota((1, Q_BLK), 1)
    nsel_pad = LANES
    groups = range(B_KV_GROUPS)

    q_all = q_ref[...].astype(F32)
    qg = [_q_aug(q_all, g, lane, low) for g in groups]
    has_key = (t_col >= CMP_LEN - 1)[None]

    cmp_end = CMP_STRIDE * _iota((1, ncp), 1) + (CMP_LEN - 1)
    bias_c = jnp.where(cmp_end <= t_col, 0.0, NEG)
    blk_t = _iota((nsel_pad, 1), 0)
    allowed_t = blk_t * SEL_LEN <= t_row
    cur_t = t_row // SEL_LEN
    forced_t = (blk_t == 0) | (blk_t == cur_t) | (blk_t == cur_t - 1)
    blk_f = blk_t.astype(F32)

    scores = [_dot_nt(qg[g], cmp_ref[g]).reshape(B_REP, Q_BLK, ncp) + bias_c[None] for g in groups]
    probs = []
    for s in scores:
        e = jnp.exp2(s - jnp.max(s, axis=-1, keepdims=True))
        den = jnp.sum(e, axis=-1, keepdims=True)
        probs.append(e * jnp.where(has_key, 1.0 / den, 0.0))
    o_cmp = [_dot(probs[g].reshape(B_REP * Q_BLK, ncp).astype(BF16), cmp_ref[B_KV_GROUPS + g]) for g in groups]
    ranks = []
    for p in probs:
        psum = p[0] + p[1] + p[2] + p[3]
        p_hi = psum.astype(BF16)
        p_lo = (psum - p_hi.astype(F32)).astype(BF16)
        imp_t = _dot_nt(ovt_ref[...], p_hi) + _dot_nt(ovt_ref[...], p_lo)
        ranks.append(jnp.where(allowed_t, imp_t + jnp.where(forced_t, FORCE, 0.0), NEG))

    def topk_variant(nblk):
        def run(*ranks):
            def pick_one(_, carry):
                out = []
                for rank, sel in carry:
                    best = jnp.max(rank, axis=0, keepdims=True)
                    cand = jnp.where(rank == best, blk_f[:nblk], float(nsel_pad))
                    idx = jnp.min(cand, axis=0, keepdims=True)
                    pick = blk_f[:nblk] == idx
                    out.append((jnp.where(pick, -3e38, rank), jnp.where(pick, 1.0, sel)))
                return tuple(out)

            picked = lax.fori_loop(0, topk, pick_one,
                                   tuple((rank[:nblk], jnp.zeros((nblk, Q_BLK), F32)) for rank in ranks))
            sel_t = [sel for _, sel in picked]
            if nblk < nsel_pad:
                sel_t = [jnp.concatenate([s_t, jnp.zeros((nsel_pad - nblk, Q_BLK), F32)], axis=0) for s_t in sel_t]
            return tuple(sel_t)
        return run

    extents = [min(nsel_pad, (k + 1) * TOPK_SPAN // SEL_LEN) for k in range(-(-ncp * CMP_STRIDE // TOPK_SPAN))]
    picked = lax.switch(q0 // TOPK_SPAN, [topk_variant(nblk) for nblk in extents], *ranks)

    ones = jnp.ones((8, Q_BLK), BF16)
    tile_cnt = jnp.zeros((8, LANES), F32)
    for g in groups:
        sel = jnp.where(allowed_t, picked[g], 0.0).T
        selb_ref[:, g * LANES:(g + 1) * LANES] = jnp.where(sel > 0.5, 0.0, -MASK_BIG).astype(BF16)
        per_block = _dot(ones, sel.astype(BF16))
        tile_cnt = tile_cnt + _dot(per_block.astype(BF16), gmap_ref[g])
    flags_ref[...] = (tile_cnt > 0.5).astype(jnp.int32)

    sig = jax.nn.sigmoid(ng_ref[...])
    per_head = []
    for h in range(B_HEADS):
        g, r = divmod(h, B_REP)
        per_head.append(sig[:, 3 * h:3 * h + 1] * o_cmp[g][r * Q_BLK:(r + 1) * Q_BLK])
    for hp, tile in enumerate(_place_heads(per_head, low)):
        oc_ref[:, hp * LANES:(hp + 1) * LANES] = tile


def _nsa_attend_body(lists_ref, cnts_ref, q_ref, ks0_ref, ks1_ref, kw0_ref, kw1_ref, vst_ref, vwt_ref,
                     selb_ref, oc_ref, ng_ref, oh_ref, out_ref, *, nq, ntile):
    step = pl.program_id(0) * nq + pl.program_id(1)
    q0 = pl.program_id(1) * Q_BLK
    lane = _iota((1, LANES), 1)
    low = lane < HEAD_DIM
    t_col = q0 + _iota((Q_BLK, 1), 0)
    t_row = q0 + _iota((1, Q_BLK), 1)
    win_keys = WIN + Q_BLK
    rows = B_REP * Q_BLK
    groups = range(B_KV_GROUPS)
    ks_refs = (ks0_ref, ks1_ref)
    kw_refs = (kw0_ref, kw1_ref)

    q_all = q_ref[...].astype(F32)
    qg = [_q_aug(q_all, g, lane, low) for g in groups]

    def heads_on_rows(o_t):
        o_t = jnp.concatenate([o_t, o_t], axis=0)
        return jnp.concatenate([o_t[:, r * Q_BLK:(r + 1) * Q_BLK].T for r in range(B_REP)], axis=0)

    def softmax_step(s, m_i):
        m_new = jnp.maximum(m_i, jnp.max(s, axis=0, keepdims=True))
        return jnp.exp2(s - m_new).astype(BF16), jnp.exp2(m_i - m_new), m_new

    q_slc = []
    for g in groups:
        sel_bias = selb_ref[:, g * LANES:(g + 1) * LANES]
        q_slc.append(jnp.concatenate([qg[g], jnp.concatenate([sel_bias] * B_REP, axis=0)], axis=1))

    def qk(g, kt):
        k0 = pl.multiple_of(jnp.minimum(kt, ntile - 1) * SLC_TILE, SLC_TILE)
        o0 = pl.multiple_of(kt * SLC_TILE, SLC_TILE)
        k_aug = jnp.concatenate([ks_refs[g][pl.ds(k0, SLC_TILE), :], oh_ref[pl.ds(o0, SLC_TILE), :]], axis=1)
        return _dot_nt(k_aug, q_slc[g])

    def pv(g, p, kt):
        return _dot(vst_ref[jnp.minimum(kt, ntile - 1), g], p)

    def trip(k, carry):
        work = [(g, lists_ref[(step * B_KV_GROUPS + g) * LIST_W + k * SLC_TRIP + u])
                for u in range(SLC_TRIP) for g in groups]
        scores = [qk(g, kt) for g, kt in work]
        state = list(carry)
        for (g, kt), s in zip(work, scores):
            m_i, acc = state[g]
            p, a, m_i = softmax_step(s, m_i)
            state[g] = (m_i, a * acc + pv(g, p, kt))
        return tuple(state)

    n_trips = jnp.maximum(cnts_ref[step * B_KV_GROUPS], cnts_ref[step * B_KV_GROUPS + 1]) // SLC_TRIP
    init = tuple((jnp.full((1, rows), -3e38, F32), jnp.zeros((VT_ROWS, rows), F32)) for g in groups)
    state = lax.fori_loop(0, n_trips, trip, init)

    kt_diag = q0 // SLC_TILE
    key_pos = kt_diag * SLC_TILE + _iota((SLC_TILE, 1), 0)
    causal = jnp.where(key_pos <= t_row, 0.0, NEG)
    causal = jnp.concatenate([causal] * B_REP, axis=1)
    scores = [qk(g, kt_diag) + causal for g in groups]
    o_slc = []
    for g in groups:
        m_i, acc = state[g]
        p, a, m_i = softmax_step(scores[g], m_i)
        acc = a * acc + pv(g, p, kt_diag)
        o_slc.append(heads_on_rows(acc[:HEAD_DIM] / acc[HEAD_DIM:HEAD_DIM + 1]))

    kstart = pl.multiple_of(jnp.maximum(q0 - WIN, 0), Q_BLK)
    d_w = t_row - (kstart + _iota((win_keys, 1), 0))
    bias_w = jnp.where((d_w >= 0) & (d_w < WIN), 0.0, NEG)
    bias_w = jnp.concatenate([bias_w] * B_REP, axis=1)
    scores = [_dot_nt(kw_refs[g][pl.ds(kstart, win_keys), :], qg[g]) + bias_w for g in groups]
    o_win = []
    for g in groups:
        s = scores[g]
        p = jnp.exp2(s - jnp.max(s, axis=0, keepdims=True)).astype(BF16)
        v_t = jnp.concatenate([vwt_ref[kstart // Q_BLK + u, g] for u in range(win_keys // Q_BLK)], axis=1)
        acc = _dot(v_t, p)
        o_win.append(heads_on_rows(acc[:HEAD_DIM] / acc[HEAD_DIM:HEAD_DIM + 1]))

    sig = jax.nn.sigmoid(ng_ref[...])
    per_head = []
    for h in range(B_HEADS):
        g, r = divmod(h, B_REP)
        rs = slice(r * Q_BLK, (r + 1) * Q_BLK)
        per_head.append(sig[:, 3 * h + 1:3 * h + 2] * o_slc[g][rs] + sig[:, 3 * h + 2:3 * h + 3] * o_win[g][rs])
    for hp, tile in enumerate(_place_heads(per_head, low)):
        cols = slice(hp * LANES, (hp + 1) * LANES)
        out_ref[:, cols] = (oc_ref[:, cols] + tile).astype(BF16)


def _tile_lists(flags, *, bsz, s_len):
    nq = s_len // Q_BLK
    ntile = s_len // SLC_TILE
    f = flags[:, 0, :B_KV_GROUPS * FLAG_W].reshape(bsz * nq, B_KV_GROUPS, FLAG_W)[:, :, :ntile] > 0
    kt = jnp.arange(ntile, dtype=jnp.int32)
    diag = jnp.tile((jnp.arange(nq, dtype=jnp.int32) * Q_BLK) // SLC_TILE, bsz)[:, None, None]
    touched = f & (kt < diag)
    rank = jnp.cumsum(touched, axis=-1, dtype=jnp.int32) - 1
    n_touched = rank[..., -1] + 1
    cnt = SLC_TRIP * ((n_touched + SLC_TRIP - 1) // SLC_TRIP)
    pos = jnp.arange(LIST_W, dtype=jnp.int32)
    hit = touched[..., None, :] & (rank[..., None, :] == pos[:, None])
    order = jnp.sum(jnp.where(hit, kt, 0), axis=-1)
    lists = jnp.where(pos < n_touched[..., None], order, ntile)
    return lists.reshape(-1), cnt.reshape(-1)


def _nsa2(qb, kvb, vst, vwt, cmp, ng, onehot, ovt, gmap, *, bsz, s_len):
    nq = s_len // Q_BLK
    ntile = s_len // SLC_TILE
    ncp = cmp.shape[2]
    topk = min(SEL_TOPK, s_len // SEL_LEN)
    hw = B_HEADS * HEAD_DIM
    rowblk = lambda b, i: (b * nq + i, 0)
    params = pltpu.CompilerParams(dimension_semantics=("arbitrary", "arbitrary"), vmem_limit_bytes=VMEM_LIMIT)
    selb, oc, flags = pl.pallas_call(
        functools.partial(_nsa_select_body, ncp=ncp, topk=topk),
        grid=(bsz, nq),
        in_specs=[pl.BlockSpec((Q_BLK, hw), rowblk),
                  pl.BlockSpec((None, 2 * B_KV_GROUPS, ncp, LANES), lambda b, i: (b, 0, 0, 0)),
                  pl.BlockSpec((Q_BLK, LANES), rowblk),
                  pl.BlockSpec(ovt.shape, lambda b, i: (0, 0)),
                  pl.BlockSpec(gmap.shape, lambda b, i: (0, 0, 0))],
        out_specs=[pl.BlockSpec((Q_BLK, B_KV_GROUPS * LANES), rowblk),
                   pl.BlockSpec((Q_BLK, hw), rowblk),
                   pl.BlockSpec((None, 8, LANES), lambda b, i: (b * nq + i, 0, 0))],
        out_shape=[jax.ShapeDtypeStruct((bsz * s_len, B_KV_GROUPS * LANES), BF16),
                   jax.ShapeDtypeStruct((bsz * s_len, hw), F32),
                   jax.ShapeDtypeStruct((bsz * nq, 8, LANES), jnp.int32)],
        compiler_params=params,
        name="nsa_select",
    )(qb, cmp, ng, ovt, gmap)

    lists, cnts = _tile_lists(flags, bsz=bsz, s_len=s_len)
    rowblk2 = lambda b, i, lists, cnts: (b * nq + i, 0)
    res = lambda col: (lambda b, i, lists, cnts: (b, col))
    return pl.pallas_call(
        functools.partial(_nsa_attend_body, nq=nq, ntile=ntile),
        grid_spec=pltpu.PrefetchScalarGridSpec(
            num_scalar_prefetch=2,
            grid=(bsz, nq),
            in_specs=[pl.BlockSpec((Q_BLK, hw), rowblk2)]
                     + [pl.BlockSpec((s_len, LANES), res(col)) for col in range(2 * B_KV_GROUPS)]
                     + [pl.BlockSpec((None,) + v.shape[1:], lambda b, i, lists, cnts: (b, 0, 0, 0, 0))
                        for v in (vst, vwt)]
                     + [pl.BlockSpec((Q_BLK, B_KV_GROUPS * LANES), rowblk2),
                        pl.BlockSpec((Q_BLK, hw), rowblk2),
                        pl.BlockSpec((Q_BLK, LANES), rowblk2),
                        pl.BlockSpec(onehot.shape, lambda b, i, lists, cnts: (0, 0))],
            out_specs=pl.BlockSpec((Q_BLK, hw), rowblk2)),
        out_shape=jax.ShapeDtypeStruct((bsz * s_len, hw), BF16),
        compiler_params=params,
        name="nsa_attend",
    )(lists, cnts, qb, kvb, kvb, kvb, kvb, vst, vwt, selb, oc, ng, onehot)


def _post_body(x_ref, oa_ref, ob_ref, gab_ref, p_ref, wua_ref, wub_ref, wout_ref, g2_ref, w1_ref, w2_ref,
               g3_ref, wpg_ref, wple_ref, gf_ref, out_ref, *, d, ff_chunk):
    ya = _dot(oa_ref[...], wua_ref[...])
    yb = _dot(ob_ref[...], wub_ref[...])
    mixed = (jax.nn.sigmoid(gab_ref[:, :d].astype(F32)) * ya
             + jax.nn.sigmoid(gab_ref[:, d:].astype(F32)) * yb)
    h = x_ref[...] + _dot(mixed.astype(BF16), wout_ref[...])
    n2 = _rms(h, g2_ref[...]).astype(BF16)
    acc = h
    for c in range(w1_ref.shape[1] // ff_chunk):
        cs = slice(c * ff_chunk, (c + 1) * ff_chunk)
        hid = jnp.square(jnp.maximum(_dot(n2, w1_ref[:, cs]), 0.0))
        acc = acc + _dot(hid.astype(BF16), w2_ref[cs, :])
    n3 = _rms(acc, g3_ref[...]).astype(BF16)
    gate = jax.nn.sigmoid(_dot(n3, wpg_ref[...]))
    h3 = acc + gate * _dot(p_ref[...].astype(BF16), wple_ref[...])
    out_ref[...] = _rms(h3, gf_ref[...])


def _post(x2, oa, ob, gab, p2, wua, wub, wout, g2, w1, w2, g3, wpg, wple, gf, *, tm=512, ff_chunk=1024):
    t_len, d = x2.shape
    row = lambda i: (i, 0)
    const = lambda i: (0, 0)
    resident = lambda a: pl.BlockSpec(a.shape, const, pipeline_mode=pl.Buffered(1))
    acts = (x2, oa, ob, gab, p2)
    params = (wua, wub, wout, g2, w1, w2, g3, wpg, wple, gf)
    return pl.pallas_call(
        functools.partial(_post_body, d=d, ff_chunk=ff_chunk),
        grid=(t_len // tm,),
        in_specs=[pl.BlockSpec((tm, a.shape[1]), row) for a in acts] + [resident(w) for w in params],
        out_specs=pl.BlockSpec((tm, d), row),
        out_shape=jax.ShapeDtypeStruct((t_len, d), F32),
        compiler_params=pltpu.CompilerParams(dimension_semantics=("arbitrary",),
                                             vmem_limit_bytes=VMEM_LIMIT),
        name="post",
    )(*acts, *params)


def _selection_overlap_t(ncp, s_len):
    ncmp = (s_len - CMP_LEN) // CMP_STRIDE + 1
    nsel = s_len // SEL_LEN
    ratio = SEL_LEN // CMP_STRIDE
    span = CMP_LEN // CMP_STRIDE
    i = np.arange(ncmp)[:, None]
    j = np.arange(nsel)[None, :]
    ov = np.maximum(np.minimum(i + span, ratio * (j + 1)) - np.maximum(i, ratio * j), 0)
    out = np.zeros((LANES, ncp), np.float32)
    out[:nsel, :ncmp] = ov.T
    return out


def _layer(h, p_i, norm_mix_g, w_in, pe_ck, w_ck1, w_ck2, pe_cv, w_cv1, w_cv2, w_up_a, w_up_b, w_out,
           norm_mlp_g, w_mlp1, w_mlp2, norm_ple_g, w_ple_gate, w_ple, final_g):
    bsz, s_len, d = h.shape
    t_len = bsz * s_len
    aw = A_HEADS * HEAD_DIM
    bw = B_HEADS * HEAD_DIM
    kvw = B_KV_GROUPS * HEAD_DIM
    assert s_len % A_TILE == 0 and s_len // SEL_LEN <= LANES and kvw == LANES

    o_qb = 3 * aw
    o_kv = o_qb + bw
    o_ng = o_kv + 6 * kvw
    o_ga = o_ng + 3 * B_HEADS
    kv = lambda i: w_in[:, o_kv + i * kvw:o_kv + (i + 1) * kvw]
    zeros_h = jnp.zeros((d, HEAD_DIM), w_in.dtype)
    grp = lambda w, g: jnp.concatenate([w[:, g * HEAD_DIM:(g + 1) * HEAD_DIM], zeros_h], axis=1)
    wa = jnp.concatenate([w_in[:, :aw] * (SCALE * LOG2E), w_in[:, aw:3 * aw]], axis=1)
    wc = jnp.concatenate([kv(0), kv(1)], axis=1)
    wq = w_in[:, o_qb:o_qb + bw] * (SCALE * LOG2E)
    wkv = jnp.concatenate([grp(kv(2), 0), grp(kv(2), 1), grp(kv(4), 0), grp(kv(4), 1)], axis=1)
    wvt = jnp.concatenate([kv(3), kv(5)], axis=1).T
    wng = jnp.concatenate([w_in[:, o_ng:o_ga], jnp.zeros((d, LANES - 3 * B_HEADS), w_in.dtype)], axis=1)
    wgab = w_in[:, o_ga:]
    x2 = h.reshape(t_len, d)
    a0, a1, a2, kvc, qb, kvb, ng, gab, vst, vwt = _proj(x2, norm_mix_g.reshape(1, d), *(w.astype(BF16) for w in
                                                        (wa, wc, wq, wkv, wng, wgab, wvt)), s_len=s_len)

    ncp = s_len // CMP_STRIDE
    pe2 = jnp.stack([jnp.tile(pe, (1, B_KV_GROUPS)) for pe in (pe_ck, pe_cv)])

    def per_row_block_diag(w1):
        w = w1.reshape(CMP_LEN, HEAD_DIM, CMP_HIDDEN)
        z = jnp.zeros_like(w)
        return jnp.concatenate([jnp.concatenate([w, z], axis=2), jnp.concatenate([z, w], axis=2)], axis=1)

    w1bd = jnp.stack([per_row_block_diag(w_ck1), per_row_block_diag(w_cv1)]).astype(BF16)
    zpad = jnp.zeros((CMP_HIDDEN, HEAD_DIM), w_ck2.dtype)
    w2s = jnp.stack([jnp.concatenate([w_ck2, zpad], axis=1), jnp.concatenate([w_ck2, zpad], axis=1),
                     jnp.concatenate([w_cv2, zpad], axis=1), jnp.concatenate([zpad, w_cv2], axis=1)]).astype(BF16)
    cmp = _compress(kvc, pe2, w1bd, w2s, bsz=bsz, s_len=s_len)

    oa = _dilated((a0, a1, a2), bsz=bsz, s_len=s_len)
    onehot = (np.arange(s_len + SLC_TILE)[:, None] // SEL_LEN == np.arange(LANES)[None, :])
    onehot[s_len:] = True
    blocks_per_tile = SLC_TILE // SEL_LEN
    gmap = np.zeros((B_KV_GROUPS, LANES, LANES), np.float32)
    for g in range(B_KV_GROUPS):
        gmap[g, np.arange(LANES), FLAG_W * g + np.arange(LANES) // blocks_per_tile] = 1.0
    ob = _nsa2(qb, kvb, vst, vwt, cmp, ng, jnp.asarray(onehot, BF16),
               jnp.asarray(_selection_overlap_t(ncp, s_len), BF16), jnp.asarray(gmap, BF16), bsz=bsz, s_len=s_len)

    b16 = lambda w: w.astype(BF16)
    row = lambda v: v.reshape(1, d)
    return _post(x2, oa, ob, gab, p_i.reshape(t_len, -1), b16(w_up_a), b16(w_up_b), b16(w_out), row(norm_mlp_g),
                 b16(w_mlp1), b16(w_mlp2), row(norm_ple_g), b16(w_ple_gate), b16(w_ple), row(final_g)
                 ).reshape(bsz, s_len, d)


def kernel(x, p, norm_mix_g, w_in, pe_ck, w_ck1, w_ck2, pe_cv, w_cv1, w_cv2, w_up_a, w_up_b, w_out,
           norm_mlp_g, w_mlp1, w_mlp2, norm_ple_g, w_ple_gate, w_ple, norm_final_g):
    depth = w_in.shape[0]
    assert depth == 1, "the fused tail applies the final norm inside the single layer"
    return _layer(x, p[0], norm_mix_g[0], w_in[0], pe_ck[0], w_ck1[0], w_ck2[0], pe_cv[0], w_cv1[0], w_cv2[0],
                  w_up_a[0], w_up_b[0], w_out[0], norm_mlp_g[0], w_mlp1[0], w_mlp2[0], norm_ple_g[0],
                  w_ple_gate[0], w_ple[0], norm_final_g)
```

```python
import functools

import numpy as np
import jax
import jax.numpy as jnp
from jax import lax
from jax.experimental import pallas as pl
from jax.experimental.pallas import tpu as pltpu

HEAD_DIM = 64
A_HEADS = 8
A_CONFIGS = ((128, 1), (512, 4), (2048, 16))
B_HEADS = 8
B_KV_GROUPS = 2
B_REP = B_HEADS // B_KV_GROUPS
CMP_LEN = 32
CMP_STRIDE = 16
CMP_HIDDEN = 256
SEL_LEN = 64
SEL_TOPK = 16
WIN = 512
Q_BLK = 128
EPS = 1e-6
NEG = -1e30
FORCE = 1e9
MASK_BIG = 2.0 ** 100
SCALE = HEAD_DIM ** -0.5

LANES = 128
A_TILE = 2048
A_GROUP = 8
SLC_TILE = 256
VT_ROWS = HEAD_DIM + 16
FLAG_W = 32
SEL_SPAN = 2048
SLC_TRIP = 3
LIST_W = FLAG_W + SLC_TRIP
VMEM_LIMIT = 56 * 1024 * 1024

ALIBI_TERMS = 3
LANE_POS = 64
LANE_CMP = LANE_POS + 2 * ALIBI_TERMS
LOG2E = 1.4426950408889634

F32 = jnp.float32
BF16 = jnp.bfloat16


def _dot(a, b):
    return jnp.dot(a, b, preferred_element_type=F32)


def _dot_nt(a, b):
    return lax.dot_general(a, b, (((1,), (1,)), ((), ())), preferred_element_type=F32)


def _rms(x, g):
    inv = lax.rsqrt(jnp.mean(x * x, axis=-1, keepdims=True) + EPS)
    return (x * inv) * g


def _iota(shape, dim, dtype=jnp.int32):
    return lax.broadcasted_iota(dtype, shape, dim)


def _pair_columns(rel_lane, even_val, odd_val):
    inside = (rel_lane >= 0) & (rel_lane < 2 * ALIBI_TERMS)
    return jnp.where(inside, jnp.where(rel_lane % 2 == 0, even_val, odd_val), 0.0)


def _bf16_pieces(x):
    pieces, rest = [], np.float64(x)
    for _ in range(ALIBI_TERMS):
        piece = np.float64(np.asarray(rest, np.float32).astype(jnp.bfloat16).astype(np.float32))
        pieces.append(float(piece))
        rest = rest - piece
    return pieces


def _query_alibi_row(slope, lane):
    row = jnp.zeros(lane.shape, F32)
    coeffs = ((LANE_POS, SEL_LEN * slope), (LANE_POS + 1, slope),
              (LANE_CMP, 16 * CMP_STRIDE * slope), (LANE_CMP + 1, CMP_STRIDE * slope))
    for lane0, coeff in coeffs:
        for t, piece in enumerate(_bf16_pieces(coeff * LOG2E)):
            row = jnp.where(lane == lane0 + 2 * t, piece, row)
    return row


def _proj_body(x_ref, g_ref, wa_ref, wc_ref, wq_ref, wkv_ref, wng_ref, wgab_ref, wvt_ref,
               a0_ref, a1_ref, a2_ref, kvc_ref, qb_ref, kvb_ref, ng_ref, gab_ref, vst_ref, vwt_ref, res_sc, mid_sc,
               *, tm, s_len):
    n = _rms(x_ref[...], g_ref[...]).astype(BF16)
    vt = _dot_nt(wvt_ref[...], n)
    for branch, (out_ref, width) in enumerate(((vst_ref, SLC_TILE), (vwt_ref, Q_BLK))):
        for u in range(tm // width):
            for g in range(B_KV_GROUPS):
                r0 = (branch * B_KV_GROUPS + g) * HEAD_DIM
                out_ref[u, g, :HEAD_DIM, :] = vt[r0:r0 + HEAD_DIM, u * width:(u + 1) * width].astype(BF16)
                out_ref[u, g, HEAD_DIM:, :] = jnp.ones((VT_ROWS - HEAD_DIM, width), BF16)
    res = _dot(n, wa_ref[...])
    (_, d0), (_, d1), (_, d2) = A_CONFIGS
    step = d2 // d1
    for s in range(res.shape[1] // LANES):
        cols = slice(s * LANES, (s + 1) * LANES)
        res_sc[s] = res[:, cols]
        a0_ref[0, :, cols] = res_sc[s].astype(BF16)
        for r in range(d1):
            part = res_sc[s, pl.ds(r, tm // d1, stride=d1), :]
            a1_ref[r, :, cols] = part.astype(BF16)
            mid_sc[r] = part
            for r2 in range(step):
                a2_ref[r + d1 * r2, :, cols] = mid_sc[r, pl.ds(r2, tm // d2, stride=step), :].astype(BF16)
    kvc_ref[...] = _dot(n, wc_ref[...])
    qb_ref[...] = _dot(n, wq_ref[...]).astype(BF16)
    ng_ref[...] = _dot(n, wng_ref[...])
    gab_ref[...] = _dot(n, wgab_ref[...]).astype(BF16)
    pos = (pl.program_id(0) * tm) % s_len + _iota((tm, LANES), 0)
    lane = _iota((tm, LANES), 1)
    posc = _pair_columns(lane - LANE_POS, (pos // SEL_LEN).astype(F32), (pos % SEL_LEN).astype(F32))
    kv = _dot(n, wkv_ref[...])
    for c in range(kv.shape[1] // LANES):
        kvb_ref[:, c * LANES:(c + 1) * LANES] = (kv[:, c * LANES:(c + 1) * LANES] + posc).astype(BF16)


def _proj(x2, g, wa, wc, wq, wkv, wng, wgab, wvt, *, s_len, tm=2 * SLC_TILE):
    t_len, d = x2.shape
    bsz = t_len // s_len
    nrt = s_len // tm
    const = lambda i: (0, 0)
    row = lambda i: (i, 0)
    ws = (wa, wc, wq, wkv, wng, wgab, wvt)
    flat = (wc, wq, wkv, wng, wgab)
    flat_dtypes = (F32, BF16, BF16, F32, BF16)
    aw = wa.shape[1]
    a_specs = [pl.BlockSpec((None, dil, tm // dil, aw), lambda i: (i // nrt, 0, i % nrt, 0))
               for _, dil in A_CONFIGS]
    a_shapes = [jax.ShapeDtypeStruct((bsz, dil, s_len // dil, aw), BF16) for _, dil in A_CONFIGS]
    return pl.pallas_call(
        functools.partial(_proj_body, tm=tm, s_len=s_len),
        grid=(t_len // tm,),
        in_specs=[pl.BlockSpec((tm, d), row), pl.BlockSpec((1, d), const)]
                 + [pl.BlockSpec(w.shape, const, pipeline_mode=pl.Buffered(1)) for w in ws],
        out_specs=a_specs + [pl.BlockSpec((tm, w.shape[1]), row) for w in flat]
                  + [pl.BlockSpec((None, tm // width, B_KV_GROUPS, VT_ROWS, width),
                                  lambda i: (i // nrt, i % nrt, 0, 0, 0)) for width in (SLC_TILE, Q_BLK)],
        out_shape=a_shapes + [jax.ShapeDtypeStruct((t_len, w.shape[1]), dt) for w, dt in zip(flat, flat_dtypes)]
                  + [jax.ShapeDtypeStruct((bsz, s_len // width, B_KV_GROUPS, VT_ROWS, width), BF16)
                     for width in (SLC_TILE, Q_BLK)],
        scratch_shapes=[pltpu.VMEM((aw // LANES, tm, LANES), F32),
                        pltpu.VMEM((A_CONFIGS[1][1], tm // A_CONFIGS[1][1], LANES), F32)],
        compiler_params=pltpu.CompilerParams(dimension_semantics=("arbitrary",),
                                             vmem_limit_bytes=VMEM_LIMIT),
        name="proj",
    )(x2, g, *ws)


def _gelu_tanh(x):
    return 0.5 * x * (1.0 + jnp.tanh(np.sqrt(2.0 / np.pi).astype(np.float32) * (x + 0.044715 * (x * x * x))))


def _compress_body(x_ref, pe_ref, w1_ref, w2_ref, out_ref, *, ncp):
    first = jnp.zeros((ncp, B_KV_GROUPS * CMP_HIDDEN), F32)
    second = jnp.zeros((ncp, B_KV_GROUPS * CMP_HIDDEN), F32)
    for l in range(CMP_STRIDE):
        x_l = x_ref[pl.ds(l, ncp, stride=CMP_STRIDE), :]
        first = first + _dot((x_l + pe_ref[l:l + 1, :]).astype(BF16), w1_ref[l])
        second = second + _dot((x_l + pe_ref[CMP_STRIDE + l:CMP_STRIDE + l + 1, :]).astype(BF16),
                               w1_ref[CMP_STRIDE + l])
    pre = first + jnp.concatenate([second[1:], second[:1]], axis=0)
    hid = _gelu_tanh(pre).astype(BF16)
    is_key = pl.program_id(1) == 0
    n_idx = _iota((ncp, LANES), 0)
    lane = _iota((ncp, LANES), 1)
    nc = _pair_columns(lane - LANE_CMP, (n_idx // 16).astype(F32), (n_idx % 16).astype(F32))
    nc = jnp.where(is_key, nc, 0.0)
    for g in range(B_KV_GROUPS):
        out = _dot(hid[:, g * CMP_HIDDEN:(g + 1) * CMP_HIDDEN], w2_ref[g])
        out_ref[g] = (out + nc).astype(BF16)


def _compress(kvc, pe2, w1bd, w2s, *, bsz, s_len):
    ncp = s_len // CMP_STRIDE
    return pl.pallas_call(
        functools.partial(_compress_body, ncp=ncp),
        grid=(bsz, 2),
        in_specs=[pl.BlockSpec((s_len, LANES), lambda b, kv: (b, kv)),
                  pl.BlockSpec((None,) + pe2.shape[1:], lambda b, kv: (kv, 0, 0)),
                  pl.BlockSpec((None,) + w1bd.shape[1:], lambda b, kv: (kv, 0, 0, 0)),
                  pl.BlockSpec((B_KV_GROUPS, CMP_HIDDEN, LANES), lambda b, kv: (kv, 0, 0))],
        out_specs=pl.BlockSpec((None, B_KV_GROUPS, ncp, LANES), lambda b, kv: (b, kv, 0, 0)),
        out_shape=jax.ShapeDtypeStruct((bsz, 2 * B_KV_GROUPS, ncp, LANES), BF16),
        compiler_params=pltpu.CompilerParams(dimension_semantics=("arbitrary", "arbitrary"),
                                             vmem_limit_bytes=VMEM_LIMIT),
        name="compress",
    )(kvc, pe2, w1bd, w2s)


def _head_slope(h, n_heads):
    out = jnp.float32(2.0 ** (-8.0 * n_heads / n_heads))
    for k in range(n_heads - 1):
        out = jnp.where(h == k, jnp.float32(2.0 ** (-8.0 * (k + 1) / n_heads)), out)
    return out


def _dilated_body(*refs):
    ncfg = len(A_CONFIGS)
    in_refs = refs[:5 * ncfg]
    out_ref = refs[5 * ncfg]
    o_sc, l_sc, m_sc, bias_sc = refs[5 * ncfg + 1:5 * ncfg + 5]
    kbufs = refs[5 * ncfg + 5:5 * ncfg + 5 + ncfg]
    vbufs = refs[5 * ncfg + 5 + ncfg:]
    hp = pl.program_id(1)
    first_tile = pl.program_id(2) == 0
    low = _iota((1, LANES), 1) < HEAD_DIM

    @pl.when(first_tile)
    def _():
        qi = _iota((Q_BLK, 2 * Q_BLK), 0)
        kj = _iota((Q_BLK, 2 * Q_BLK), 1)
        dist = qi - kj + Q_BLK
        for c, (window, dil) in enumerate(A_CONFIGS):
            valid = (dist >= 0) & (dist <= window // dil)
            for hh in range(2):
                slope = _head_slope(2 * hp + hh, A_HEADS)
                bias = jnp.where(valid, -(slope * dil * LOG2E) * dist.astype(F32), NEG)
                bias_sc[(c * 2 + hh) * 2] = bias
                bias_sc[(c * 2 + hh) * 2 + 1] = jnp.where(kj < Q_BLK, NEG, bias)

    for c, (window, dil) in enumerate(A_CONFIGS):
        q_ref, k_ref, v_ref, kh_ref, vh_ref = in_refs[5 * c:5 * c + 5]
        kbuf, vbuf = kbufs[c], vbufs[c]
        rows = A_TILE // dil
        nsub = rows // Q_BLK
        kbuf[:, :Q_BLK, :] = kh_ref[...]
        kbuf[:, Q_BLK:, :] = k_ref[...]
        vbuf[:, :Q_BLK, :] = vh_ref[...]
        vbuf[:, Q_BLK:, :] = v_ref[...]

        def group(gidx, carry, c=c, dil=dil, nsub=nsub, q_ref=q_ref, kbuf=kbuf, vbuf=vbuf):
            subs = []
            for u in range(A_GROUP):
                idx = gidx * A_GROUP + u
                r = idx // nsub
                j = idx % nsub
                j0 = pl.multiple_of(j * Q_BLK, Q_BLK)
                seq_start = ((j == 0) & first_tile).astype(jnp.int32)
                subs.append((r, j, j0, seq_start))
            scores = []
            for r, j, j0, seq_start in subs:
                q = q_ref[r, pl.ds(j0, Q_BLK), :]
                k2 = kbuf[r, pl.ds(j0, 2 * Q_BLK), :]
                for hh in range(2):
                    qm = jnp.where(low == (hh == 0), q, jnp.zeros_like(q))
                    scores.append(_dot_nt(qm, k2) + bias_sc[(c * 2 + hh) * 2 + seq_start])
            probs = []
            for s in scores:
                m = jnp.max(s, axis=-1, keepdims=True)
                e = jnp.exp2(s - m)
                probs.append((e.astype(BF16), m, jnp.sum(e, axis=-1, keepdims=True)))
            for u, (r, j, j0, seq_start) in enumerate(subs):
                v2 = vbuf[r, pl.ds(j0, 2 * Q_BLK), :]
                outs = [_dot(probs[2 * u + hh][0], v2) for hh in range(2)]
                stats = [[jnp.broadcast_to(probs[2 * u + hh][i], (Q_BLK, LANES)) for hh in range(2)] for i in (1, 2)]
                row0 = j * (Q_BLK * dil) + r
                dst = pl.ds(row0, Q_BLK, stride=dil) if dil > 1 else pl.ds(row0, Q_BLK)
                o_sc[c, dst, :] = jnp.where(low, outs[0], outs[1])
                m_sc[c, dst, :] = jnp.where(low, stats[0][0], stats[0][1])
                l_sc[c, dst, :] = jnp.where(low, stats[1][0], stats[1][1])
            return carry

        lax.fori_loop(0, dil * nsub // A_GROUP, group, 0)

    m = jnp.maximum(jnp.maximum(m_sc[0], m_sc[1]), m_sc[2])
    num = jnp.zeros((A_TILE, LANES), F32)
    den = jnp.zeros((A_TILE, LANES), F32)
    for c in range(ncfg):
        e = jnp.exp2(m_sc[c] - m)
        num = num + e * o_sc[c]
        den = den + e * l_sc[c]
    out_ref[...] = (num / den).astype(BF16)


def _dilated(qkv_by_cfg, *, bsz, s_len):
    nt = s_len // A_TILE
    npair = A_HEADS // 2
    ncfg = len(A_CONFIGS)
    in_specs, operands, kv_scratch = [], [], []
    for (window, dil), arr in zip(A_CONFIGS, qkv_by_cfg):
        rows = A_TILE // dil
        nsub = rows // Q_BLK
        cur = lambda off: (lambda b, h, t: (b, 0, t, off + h))
        halo = lambda off, nsub=nsub: (lambda b, h, t: (b, 0, jnp.maximum(t * nsub - 1, 0), off + h))
        in_specs += [pl.BlockSpec((None, dil, rows, LANES), cur(0)),
                     pl.BlockSpec((None, dil, rows, LANES), cur(npair)),
                     pl.BlockSpec((None, dil, rows, LANES), cur(2 * npair)),
                     pl.BlockSpec((None, dil, Q_BLK, LANES), halo(npair)),
                     pl.BlockSpec((None, dil, Q_BLK, LANES), halo(2 * npair))]
        operands += [arr] * 5
        kv_scratch.append(pltpu.VMEM((dil, Q_BLK + rows, LANES), BF16))
    return pl.pallas_call(
        _dilated_body,
        grid=(bsz, npair, nt),
        in_specs=in_specs,
        out_specs=pl.BlockSpec((A_TILE, LANES), lambda b, h, t: (b * nt + t, h)),
        out_shape=jax.ShapeDtypeStruct((bsz * s_len, A_HEADS * HEAD_DIM), BF16),
        scratch_shapes=[pltpu.VMEM((ncfg, A_TILE, LANES), F32)] * 3
                       + [pltpu.VMEM((ncfg * 4, Q_BLK, 2 * Q_BLK), F32)] + kv_scratch + kv_scratch,
        compiler_params=pltpu.CompilerParams(dimension_semantics=("arbitrary",) * 3,
                                             vmem_limit_bytes=VMEM_LIMIT),
        name="dilated",
    )(*operands)


def _q_aug(q_all, g, lane, low):
    rows = []
    for r in range(B_REP):
        h = g * B_REP + r
        blk = q_all[:, (h // 2) * LANES:(h // 2 + 1) * LANES]
        if h % 2 == 1:
            blk = pltpu.roll(blk, HEAD_DIM, axis=1)
        rows.append(jnp.where(low, blk, _query_alibi_row(2.0 ** (-8.0 * (h + 1) / B_HEADS), lane)))
    return jnp.concatenate(rows, axis=0).astype(BF16)


def _place_heads(per_head, low):
    placed = []
    for h, o in enumerate(per_head):
        placed.append(pltpu.roll(o, HEAD_DIM, axis=1) if h % 2 != h // B_REP else o)
    return [jnp.where(low, placed[2 * hp], placed[2 * hp + 1]) for hp in range(B_HEADS // 2)]


def _nsa_select_body(q_ref, cmp_ref, ng_ref, ovt_ref, gmap_ref, selb_ref, oc_ref, flags_ref, *, stretch, topk):
    ncw = min(cmp_ref.shape[1], (stretch + 1) * SEL_SPAN // CMP_STRIDE)
    nblk = min(LANES, (stretch + 1) * SEL_SPAN // SEL_LEN)
    q0 = stretch * SEL_SPAN + pl.program_id(1) * Q_BLK
    lane = _iota((1, LANES), 1)
    low = lane < HEAD_DIM
    t_col = q0 + _iota((Q_BLK, 1), 0)
    t_row = q0 + _iota((1, Q_BLK), 1)
    groups = range(B_KV_GROUPS)

    q_all = q_ref[...].astype(F32)
    qg = [_q_aug(q_all, g, lane, low) for g in groups]
    has_key = (t_col >= CMP_LEN - 1)[None]

    cmp_end = CMP_STRIDE * _iota((1, ncw), 1) + (CMP_LEN - 1)
    bias_c = jnp.where(cmp_end <= t_col, 0.0, NEG)
    blk_t = _iota((nblk, 1), 0)
    allowed_t = blk_t * SEL_LEN <= t_row
    cur_t = t_row // SEL_LEN
    forced_t = (blk_t == 0) | (blk_t == cur_t) | (blk_t == cur_t - 1)
    blk_f = blk_t.astype(F32)

    scores = [_dot_nt(qg[g], cmp_ref[g, :ncw, :]).reshape(B_REP, Q_BLK, ncw) + bias_c[None] for g in groups]
    probs = []
    for s in scores:
        e = jnp.exp2(s - jnp.max(s, axis=-1, keepdims=True))
        den = jnp.sum(e, axis=-1, keepdims=True)
        probs.append(e * jnp.where(has_key, 1.0 / den, 0.0))
    o_cmp = [_dot(probs[g].reshape(B_REP * Q_BLK, ncw).astype(BF16), cmp_ref[B_KV_GROUPS + g, :ncw, :])
             for g in groups]
    ranks = []
    for p in probs:
        psum = p[0] + p[1] + p[2] + p[3]
        p_hi = psum.astype(BF16)
        p_lo = (psum - p_hi.astype(F32)).astype(BF16)
        ov = ovt_ref[:nblk, :ncw]
        imp_t = _dot_nt(ov, p_hi) + _dot_nt(ov, p_lo)
        ranks.append(jnp.where(allowed_t, imp_t + jnp.where(forced_t, FORCE, 0.0), NEG))

    def pick_one(_, carry):
        out = []
        for rank, sel in carry:
            best = jnp.max(rank, axis=0, keepdims=True)
            cand = jnp.where(rank == best, blk_f, float(LANES))
            idx = jnp.min(cand, axis=0, keepdims=True)
            pick = blk_f == idx
            out.append((jnp.where(pick, -3e38, rank), jnp.where(pick, 1.0, sel)))
        return tuple(out)

    picked = lax.fori_loop(0, topk, pick_one, tuple((rank, jnp.zeros((nblk, Q_BLK), F32)) for rank in ranks))

    ones = jnp.ones((8, Q_BLK), BF16)
    tile_cnt = jnp.zeros((8, LANES), F32)
    for g in groups:
        sel_t = jnp.where(allowed_t, picked[g][1], 0.0)
        if nblk < LANES:
            sel_t = jnp.concatenate([sel_t, jnp.zeros((LANES - nblk, Q_BLK), F32)], axis=0)
        sel = sel_t.T
        selb_ref[:, g * LANES:(g + 1) * LANES] = jnp.where(sel > 0.5, 0.0, -MASK_BIG).astype(BF16)
        per_block = _dot(ones, sel.astype(BF16))
        tile_cnt = tile_cnt + _dot(per_block.astype(BF16), gmap_ref[g])
    flags_ref[...] = (tile_cnt > 0.5).astype(jnp.int32)

    sig = jax.nn.sigmoid(ng_ref[...])
    per_head = []
    for h in range(B_HEADS):
        g, r = divmod(h, B_REP)
        per_head.append(sig[:, 3 * h:3 * h + 1] * o_cmp[g][r * Q_BLK:(r + 1) * Q_BLK])
    for hp, tile in enumerate(_place_heads(per_head, low)):
        oc_ref[:, hp * LANES:(hp + 1) * LANES] = tile


def _nsa_attend_body(lists_ref, cnts_ref, q_ref, ks0_ref, ks1_ref, kw0_ref, kw1_ref, vst_ref, vwt_ref, *rest,
                     nq, ntile, nstretch):
    selb_refs, oc_refs = rest[:nstretch], rest[nstretch:2 * nstretch]
    ng_ref, oh_ref, out_ref = rest[2 * nstretch:]
    step = pl.program_id(0) * nq + pl.program_id(1)
    q0 = pl.program_id(1) * Q_BLK
    stretch = q0 // SEL_SPAN

    def from_select(refs, cols):
        val = refs[0][:, cols]
        for k in range(1, nstretch):
            val = jnp.where(stretch == k, refs[k][:, cols], val)
        return val

    lane = _iota((1, LANES), 1)
    low = lane < HEAD_DIM
    t_col = q0 + _iota((Q_BLK, 1), 0)
    t_row = q0 + _iota((1, Q_BLK), 1)
    win_keys = WIN + Q_BLK
    rows = B_REP * Q_BLK
    groups = range(B_KV_GROUPS)
    ks_refs = (ks0_ref, ks1_ref)
    kw_refs = (kw0_ref, kw1_ref)

    q_all = q_ref[...].astype(F32)
    qg = [_q_aug(q_all, g, lane, low) for g in groups]

    def heads_on_rows(o_t):
        o_t = jnp.concatenate([o_t, o_t], axis=0)
        return jnp.concatenate([o_t[:, r * Q_BLK:(r + 1) * Q_BLK].T for r in range(B_REP)], axis=0)

    def softmax_step(s, m_i):
        m_new = jnp.maximum(m_i, jnp.max(s, axis=0, keepdims=True))
        return jnp.exp2(s - m_new).astype(BF16), jnp.exp2(m_i - m_new), m_new

    q_slc = []
    for g in groups:
        sel_bias = from_select(selb_refs, slice(g * LANES, (g + 1) * LANES))
        q_slc.append(jnp.concatenate([qg[g], jnp.concatenate([sel_bias] * B_REP, axis=0)], axis=1))

    def qk(g, kt):
        k0 = pl.multiple_of(jnp.minimum(kt, ntile - 1) * SLC_TILE, SLC_TILE)
        o0 = pl.multiple_of(kt * SLC_TILE, SLC_TILE)
        k_aug = jnp.concatenate([ks_refs[g][pl.ds(k0, SLC_TILE), :], oh_ref[pl.ds(o0, SLC_TILE), :]], axis=1)
        return _dot_nt(k_aug, q_slc[g])

    def pv(g, p, kt):
        return _dot(vst_ref[jnp.minimum(kt, ntile - 1), g], p)

    def trip(k, carry):
        work = [(g, lists_ref[(step * B_KV_GROUPS + g) * LIST_W + k * SLC_TRIP + u])
                for u in range(SLC_TRIP) for g in groups]
        scores = [qk(g, kt) for g, kt in work]
        state = list(carry)
        for (g, kt), s in zip(work, scores):
            m_i, acc = state[g]
            p, a, m_i = softmax_step(s, m_i)
            state[g] = (m_i, a * acc + pv(g, p, kt))
        return tuple(state)

    n_trips = jnp.maximum(cnts_ref[step * B_KV_GROUPS], cnts_ref[step * B_KV_GROUPS + 1]) // SLC_TRIP
    init = tuple((jnp.full((1, rows), -3e38, F32), jnp.zeros((VT_ROWS, rows), F32)) for g in groups)
    state = lax.fori_loop(0, n_trips, trip, init)

    kt_diag = q0 // SLC_TILE
    key_pos = kt_diag * SLC_TILE + _iota((SLC_TILE, 1), 0)
    causal = jnp.where(key_pos <= t_row, 0.0, NEG)
    causal = jnp.concatenate([causal] * B_REP, axis=1)
    scores = [qk(g, kt_diag) + causal for g in groups]
    o_slc = []
    for g in groups:
        m_i, acc = state[g]
        p, a, m_i = softmax_step(scores[g], m_i)
        acc = a * acc + pv(g, p, kt_diag)
        o_slc.append(heads_on_rows(acc[:HEAD_DIM] / acc[HEAD_DIM:HEAD_DIM + 1]))

    kstart = pl.multiple_of(jnp.maximum(q0 - WIN, 0), Q_BLK)
    d_w = t_row - (kstart + _iota((win_keys, 1), 0))
    bias_w = jnp.where((d_w >= 0) & (d_w < WIN), 0.0, NEG)
    bias_w = jnp.concatenate([bias_w] * B_REP, axis=1)
    scores = [_dot_nt(kw_refs[g][pl.ds(kstart, win_keys), :], qg[g]) + bias_w for g in groups]
    o_win = []
    for g in groups:
        s = scores[g]
        p = jnp.exp2(s - jnp.max(s, axis=0, keepdims=True)).astype(BF16)
        v_t = jnp.concatenate([vwt_ref[kstart // Q_BLK + u, g] for u in range(win_keys // Q_BLK)], axis=1)
        acc = _dot(v_t, p)
        o_win.append(heads_on_rows(acc[:HEAD_DIM] / acc[HEAD_DIM:HEAD_DIM + 1]))

    sig = jax.nn.sigmoid(ng_ref[...])
    per_head = []
    for h in range(B_HEADS):
        g, r = divmod(h, B_REP)
        rs = slice(r * Q_BLK, (r + 1) * Q_BLK)
        per_head.append(sig[:, 3 * h + 1:3 * h + 2] * o_slc[g][rs] + sig[:, 3 * h + 2:3 * h + 3] * o_win[g][rs])
    for hp, tile in enumerate(_place_heads(per_head, low)):
        cols = slice(hp * LANES, (hp + 1) * LANES)
        out_ref[:, cols] = (from_select(oc_refs, cols) + tile).astype(BF16)


def _tile_lists(flags, *, bsz, s_len):
    nq = s_len // Q_BLK
    ntile = s_len // SLC_TILE
    f = flags[:, 0, :B_KV_GROUPS * FLAG_W].reshape(bsz * nq, B_KV_GROUPS, FLAG_W)[:, :, :ntile] > 0
    kt = jnp.arange(ntile, dtype=jnp.int32)
    diag = jnp.tile((jnp.arange(nq, dtype=jnp.int32) * Q_BLK) // SLC_TILE, bsz)[:, None, None]
    touched = f & (kt < diag)
    rank = jnp.cumsum(touched, axis=-1, dtype=jnp.int32) - 1
    n_touched = rank[..., -1] + 1
    cnt = SLC_TRIP * ((n_touched + SLC_TRIP - 1) // SLC_TRIP)
    pos = jnp.arange(LIST_W, dtype=jnp.int32)
    hit = touched[..., None, :] & (rank[..., None, :] == pos[:, None])
    order = jnp.sum(jnp.where(hit, kt, 0), axis=-1)
    lists = jnp.where(pos < n_touched[..., None], order, ntile)
    return lists.reshape(-1), cnt.reshape(-1)


def _nsa2(qb, kvb, vst, vwt, cmp, ng, onehot, ovt, gmap, *, bsz, s_len):
    nq = s_len // Q_BLK
    ntile = s_len // SLC_TILE
    ncp = cmp.shape[2]
    topk = min(SEL_TOPK, s_len // SEL_LEN)
    hw = B_HEADS * HEAD_DIM
    params = pltpu.CompilerParams(dimension_semantics=("arbitrary", "arbitrary"), vmem_limit_bytes=VMEM_LIMIT)
    span_q = min(SEL_SPAN, s_len) // Q_BLK
    nstretch = nq // span_q
    selbs, ocs, flag_parts = [], [], []
    for k in range(nstretch):
        src = lambda b, i, k=k: (b * nq + k * span_q + i, 0)
        dst = lambda b, i: (b * span_q + i, 0)
        selb, oc, flags = pl.pallas_call(
            functools.partial(_nsa_select_body, stretch=k, topk=topk),
            grid=(bsz, span_q),
            in_specs=[pl.BlockSpec((Q_BLK, hw), src),
                      pl.BlockSpec((None, 2 * B_KV_GROUPS, ncp, LANES), lambda b, i: (b, 0, 0, 0)),
                      pl.BlockSpec((Q_BLK, LANES), src),
                      pl.BlockSpec(ovt.shape, lambda b, i: (0, 0)),
                      pl.BlockSpec(gmap.shape, lambda b, i: (0, 0, 0))],
            out_specs=[pl.BlockSpec((Q_BLK, B_KV_GROUPS * LANES), dst),
                       pl.BlockSpec((Q_BLK, hw), dst),
                       pl.BlockSpec((None, None, 8, LANES), lambda b, i: (b, i, 0, 0))],
            out_shape=[jax.ShapeDtypeStruct((bsz * span_q * Q_BLK, B_KV_GROUPS * LANES), BF16),
                       jax.ShapeDtypeStruct((bsz * span_q * Q_BLK, hw), F32),
                       jax.ShapeDtypeStruct((bsz, span_q, 8, LANES), jnp.int32)],
            compiler_params=params,
            name=f"nsa_select{k}",
        )(qb, cmp, ng, ovt, gmap)
        selbs.append(selb)
        ocs.append(oc)
        flag_parts.append(flags)
    flags = jnp.concatenate(flag_parts, axis=1).reshape(bsz * nq, 8, LANES)

    lists, cnts = _tile_lists(flags, bsz=bsz, s_len=s_len)
    rowblk2 = lambda b, i, lists, cnts: (b * nq + i, 0)
    res = lambda col: (lambda b, i, lists, cnts: (b, col))
    part = lambda k: (lambda b, i, lists, cnts: (b * span_q + jnp.clip(i - k * span_q, 0, span_q - 1), 0))
    return pl.pallas_call(
        functools.partial(_nsa_attend_body, nq=nq, ntile=ntile, nstretch=nstretch),
        grid_spec=pltpu.PrefetchScalarGridSpec(
            num_scalar_prefetch=2,
            grid=(bsz, nq),
            in_specs=[pl.BlockSpec((Q_BLK, hw), rowblk2)]
                     + [pl.BlockSpec((s_len, LANES), res(col)) for col in range(2 * B_KV_GROUPS)]
                     + [pl.BlockSpec((None,) + v.shape[1:], lambda b, i, lists, cnts: (b, 0, 0, 0, 0))
                        for v in (vst, vwt)]
                     + [pl.BlockSpec((Q_BLK, B_KV_GROUPS * LANES), part(k)) for k in range(nstretch)]
                     + [pl.BlockSpec((Q_BLK, hw), part(k)) for k in range(nstretch)]
                     + [pl.BlockSpec((Q_BLK, LANES), rowblk2),
                        pl.BlockSpec(onehot.shape, lambda b, i, lists, cnts: (0, 0))],
            out_specs=pl.BlockSpec((Q_BLK, hw), rowblk2)),
        out_shape=jax.ShapeDtypeStruct((bsz * s_len, hw), BF16),
        compiler_params=params,
        name="nsa_attend",
    )(lists, cnts, qb, kvb, kvb, kvb, kvb, vst, vwt, *selbs, *ocs, ng, onehot)


def _post_body(x_ref, oa_ref, ob_ref, gab_ref, p_ref, wua_ref, wub_ref, wout_ref, g2_ref, w1_ref, w2_ref,
               g3_ref, wpg_ref, wple_ref, gf_ref, out_ref, *, d, ff_chunk):
    ya = _dot(oa_ref[...], wua_ref[...])
    yb = _dot(ob_ref[...], wub_ref[...])
    mixed = (jax.nn.sigmoid(gab_ref[:, :d].astype(F32)) * ya
             + jax.nn.sigmoid(gab_ref[:, d:].astype(F32)) * yb)
    h = x_ref[...] + _dot(mixed.astype(BF16), wout_ref[...])
    n2 = _rms(h, g2_ref[...]).astype(BF16)
    acc = h
    for c in range(w1_ref.shape[1] // ff_chunk):
        cs = slice(c * ff_chunk, (c + 1) * ff_chunk)
        hid = jnp.square(jnp.maximum(_dot(n2, w1_ref[:, cs]), 0.0))
        acc = acc + _dot(hid.astype(BF16), w2_ref[cs, :])
    n3 = _rms(acc, g3_ref[...]).astype(BF16)
    gate = jax.nn.sigmoid(_dot(n3, wpg_ref[...]))
    h3 = acc + gate * _dot(p_ref[...].astype(BF16), wple_ref[...])
    out_ref[...] = _rms(h3, gf_ref[...])


def _post(x2, oa, ob, gab, p2, wua, wub, wout, g2, w1, w2, g3, wpg, wple, gf, *, tm=512, ff_chunk=1024):
    t_len, d = x2.shape
    row = lambda i: (i, 0)
    const = lambda i: (0, 0)
    resident = lambda a: pl.BlockSpec(a.shape, const, pipeline_mode=pl.Buffered(1))
    acts = (x2, oa, ob, gab, p2)
    params = (wua, wub, wout, g2, w1, w2, g3, wpg, wple, gf)
    return pl.pallas_call(
        functools.partial(_post_body, d=d, ff_chunk=ff_chunk),
        grid=(t_len // tm,),
        in_specs=[pl.BlockSpec((tm, a.shape[1]), row) for a in acts] + [resident(w) for w in params],
        out_specs=pl.BlockSpec((tm, d), row),
        out_shape=jax.ShapeDtypeStruct((t_len, d), F32),
        compiler_params=pltpu.CompilerParams(dimension_semantics=("arbitrary",),
                                             vmem_limit_bytes=VMEM_LIMIT),
        name="post",
    )(*acts, *params)


def _selection_overlap_t(ncp, s_len):
    ncmp = (s_len - CMP_LEN) // CMP_STRIDE + 1
    nsel = s_len // SEL_LEN
    ratio = SEL_LEN // CMP_STRIDE
    span = CMP_LEN // CMP_STRIDE
    i = np.arange(ncmp)[:, None]
    j = np.arange(nsel)[None, :]
    ov = np.maximum(np.minimum(i + span, ratio * (j + 1)) - np.maximum(i, ratio * j), 0)
    out = np.zeros((LANES, ncp), np.float32)
    out[:nsel, :ncmp] = ov.T
    return out


def _layer(h, p_i, norm_mix_g, w_in, pe_ck, w_ck1, w_ck2, pe_cv, w_cv1, w_cv2, w_up_a, w_up_b, w_out,
           norm_mlp_g, w_mlp1, w_mlp2, norm_ple_g, w_ple_gate, w_ple, final_g):
    bsz, s_len, d = h.shape
    t_len = bsz * s_len
    aw = A_HEADS * HEAD_DIM
    bw = B_HEADS * HEAD_DIM
    kvw = B_KV_GROUPS * HEAD_DIM
    assert s_len % A_TILE == 0 and s_len // SEL_LEN <= LANES and kvw == LANES

    o_qb = 3 * aw
    o_kv = o_qb + bw
    o_ng = o_kv + 6 * kvw
    o_ga = o_ng + 3 * B_HEADS
    kv = lambda i: w_in[:, o_kv + i * kvw:o_kv + (i + 1) * kvw]
    zeros_h = jnp.zeros((d, HEAD_DIM), w_in.dtype)
    grp = lambda w, g: jnp.concatenate([w[:, g * HEAD_DIM:(g + 1) * HEAD_DIM], zeros_h], axis=1)
    wa = jnp.concatenate([w_in[:, :aw] * (SCALE * LOG2E), w_in[:, aw:3 * aw]], axis=1)
    wc = jnp.concatenate([kv(0), kv(1)], axis=1)
    wq = w_in[:, o_qb:o_qb + bw] * (SCALE * LOG2E)
    wkv = jnp.concatenate([grp(kv(2), 0), grp(kv(2), 1), grp(kv(4), 0), grp(kv(4), 1)], axis=1)
    wvt = jnp.concatenate([kv(3), kv(5)], axis=1).T
    wng = jnp.concatenate([w_in[:, o_ng:o_ga], jnp.zeros((d, LANES - 3 * B_HEADS), w_in.dtype)], axis=1)
    wgab = w_in[:, o_ga:]
    x2 = h.reshape(t_len, d)
    a0, a1, a2, kvc, qb, kvb, ng, gab, vst, vwt = _proj(x2, norm_mix_g.reshape(1, d), *(w.astype(BF16) for w in
                                                        (wa, wc, wq, wkv, wng, wgab, wvt)), s_len=s_len)

    ncp = s_len // CMP_STRIDE
    pe2 = jnp.stack([jnp.tile(pe, (1, B_KV_GROUPS)) for pe in (pe_ck, pe_cv)])

    def per_row_block_diag(w1):
        w = w1.reshape(CMP_LEN, HEAD_DIM, CMP_HIDDEN)
        z = jnp.zeros_like(w)
        return jnp.concatenate([jnp.concatenate([w, z], axis=2), jnp.concatenate([z, w], axis=2)], axis=1)

    w1bd = jnp.stack([per_row_block_diag(w_ck1), per_row_block_diag(w_cv1)]).astype(BF16)
    zpad = jnp.zeros((CMP_HIDDEN, HEAD_DIM), w_ck2.dtype)
    w2s = jnp.stack([jnp.concatenate([w_ck2, zpad], axis=1), jnp.concatenate([w_ck2, zpad], axis=1),
                     jnp.concatenate([w_cv2, zpad], axis=1), jnp.concatenate([zpad, w_cv2], axis=1)]).astype(BF16)
    cmp = _compress(kvc, pe2, w1bd, w2s, bsz=bsz, s_len=s_len)

    oa = _dilated((a0, a1, a2), bsz=bsz, s_len=s_len)
    onehot = (np.arange(s_len + SLC_TILE)[:, None] // SEL_LEN == np.arange(LANES)[None, :])
    onehot[s_len:] = True
    blocks_per_tile = SLC_TILE // SEL_LEN
    gmap = np.zeros((B_KV_GROUPS, LANES, LANES), np.float32)
    for g in range(B_KV_GROUPS):
        gmap[g, np.arange(LANES), FLAG_W * g + np.arange(LANES) // blocks_per_tile] = 1.0
    ob = _nsa2(qb, kvb, vst, vwt, cmp, ng, jnp.asarray(onehot, BF16),
               jnp.asarray(_selection_overlap_t(ncp, s_len), BF16), jnp.asarray(gmap, BF16), bsz=bsz, s_len=s_len)

    b16 = lambda w: w.astype(BF16)
    row = lambda v: v.reshape(1, d)
    return _post(x2, oa, ob, gab, p_i.reshape(t_len, -1), b16(w_up_a), b16(w_up_b), b16(w_out), row(norm_mlp_g),
                 b16(w_mlp1), b16(w_mlp2), row(norm_ple_g), b16(w_ple_gate), b16(w_ple), row(final_g)
                 ).reshape(bsz, s_len, d)


def kernel(x, p, norm_mix_g, w_in, pe_ck, w_ck1, w_ck2, pe_cv, w_cv1, w_cv2, w_up_a, w_up_b, w_out,
           norm_mlp_g, w_mlp1, w_mlp2, norm_ple_g, w_ple_gate, w_ple, norm_final_g):
    depth = w_in.shape[0]
    assert depth == 1, "the fused tail applies the final norm inside the single layer"
    return _layer(x, p[0], norm_mix_g[0], w_in[0], pe_ck[0], w_ck1[0], w_ck2[0], pe_cv[0], w_cv1[0], w_cv2[0],
                  w_up_a[0], w_up_b[0], w_out[0], norm_mlp_g[0], w_mlp1[0], w_mlp2[0], norm_ple_g[0],
                  w_ple_gate[0], w_ple[0], norm_final_g)
```

```python
import functools

import numpy as np
import jax
import jax.numpy as jnp
from jax import lax
from jax.experimental import pallas as pl
from jax.experimental.pallas import tpu as pltpu

HEAD_DIM = 64
A_HEADS = 8
A_CONFIGS = ((128, 1), (512, 4), (2048, 16))
B_HEADS = 8
B_KV_GROUPS = 2
B_REP = B_HEADS // B_KV_GROUPS
CMP_LEN = 32
CMP_STRIDE = 16
CMP_HIDDEN = 256
SEL_LEN = 64
SEL_TOPK = 16
WIN = 512
Q_BLK = 128
EPS = 1e-6
NEG = -1e30
FORCE = 1e9
MASK_BIG = 2.0 ** 100
SCALE = HEAD_DIM ** -0.5

PICKED = -3e38
CMP_IDX_RADIX = 16

LANES = 128
SUBLANES = 8
BF16_SUBLANES = 16
A_TILE = 2048
A_GROUP = 8
SLC_TILE = 256
VT_ROWS = HEAD_DIM + BF16_SUBLANES
FLAG_W = 32
SEL_SPAN = 2048
SLC_TRIP = 3
LIST_W = FLAG_W + SLC_TRIP
VMEM_LIMIT = 56 * 1024 * 1024

ALIBI_TERMS = 3
LANE_POS = 64
LANE_CMP = LANE_POS + 2 * ALIBI_TERMS
LOG2E = 1.4426950408889634

F32 = jnp.float32
BF16 = jnp.bfloat16


def _dot(a, b):
    return jnp.dot(a, b, preferred_element_type=F32)


def _dot_nt(a, b):
    return lax.dot_general(a, b, (((1,), (1,)), ((), ())), preferred_element_type=F32)


def _rms(x, g):
    inv = lax.rsqrt(jnp.mean(x * x, axis=-1, keepdims=True) + EPS)
    return (x * inv) * g


def _iota(shape, dim, dtype=jnp.int32):
    return lax.broadcasted_iota(dtype, shape, dim)


def _pair_columns(rel_lane, even_val, odd_val):
    inside = (rel_lane >= 0) & (rel_lane < 2 * ALIBI_TERMS)
    return jnp.where(inside, jnp.where(rel_lane % 2 == 0, even_val, odd_val), 0.0)


def _bf16_pieces(x):
    pieces, rest = [], np.float64(x)
    for _ in range(ALIBI_TERMS):
        piece = np.float64(np.asarray(rest, np.float32).astype(jnp.bfloat16).astype(np.float32))
        pieces.append(float(piece))
        rest = rest - piece
    return pieces


def _query_alibi_row(slope, lane):
    row = jnp.zeros(lane.shape, F32)
    coeffs = ((LANE_POS, SEL_LEN * slope), (LANE_POS + 1, slope),
              (LANE_CMP, CMP_IDX_RADIX * CMP_STRIDE * slope), (LANE_CMP + 1, CMP_STRIDE * slope))
    for lane0, coeff in coeffs:
        for t, piece in enumerate(_bf16_pieces(coeff * LOG2E)):
            row = jnp.where(lane == lane0 + 2 * t, piece, row)
    return row


def _proj_body(x_ref, g_ref, wa_ref, wc_ref, wq_ref, wkv_ref, wng_ref, wgab_ref, wvt_ref,
               a0_ref, a1_ref, a2_ref, kvc_ref, qb_ref, kvb_ref, ng_ref, gab_ref, vst_ref, vwt_ref, res_sc, mid_sc,
               *, tm, s_len):
    n = _rms(x_ref[...], g_ref[...]).astype(BF16)
    vt = _dot_nt(wvt_ref[...], n)
    for branch, (out_ref, width) in enumerate(((vst_ref, SLC_TILE), (vwt_ref, Q_BLK))):
        for u in range(tm // width):
            for g in range(B_KV_GROUPS):
                r0 = (branch * B_KV_GROUPS + g) * HEAD_DIM
                out_ref[u, g, :HEAD_DIM, :] = vt[r0:r0 + HEAD_DIM, u * width:(u + 1) * width].astype(BF16)
                out_ref[u, g, HEAD_DIM:, :] = jnp.ones((VT_ROWS - HEAD_DIM, width), BF16)
    res = _dot(n, wa_ref[...])
    (_, d0), (_, d1), (_, d2) = A_CONFIGS
    step = d2 // d1
    for s in range(res.shape[1] // LANES):
        cols = slice(s * LANES, (s + 1) * LANES)
        res_sc[s] = res[:, cols]
        a0_ref[0, :, cols] = res_sc[s].astype(BF16)
        for r in range(d1):
            part = res_sc[s, pl.ds(r, tm // d1, stride=d1), :]
            a1_ref[r, :, cols] = part.astype(BF16)
            mid_sc[r] = part
            for r2 in range(step):
                a2_ref[r + d1 * r2, :, cols] = mid_sc[r, pl.ds(r2, tm // d2, stride=step), :].astype(BF16)
    kvc_ref[...] = _dot(n, wc_ref[...])
    qb_ref[...] = _dot(n, wq_ref[...]).astype(BF16)
    ng_ref[...] = _dot(n, wng_ref[...])
    gab_ref[...] = _dot(n, wgab_ref[...]).astype(BF16)
    pos = (pl.program_id(0) * tm) % s_len + _iota((tm, LANES), 0)
    lane = _iota((tm, LANES), 1)
    posc = _pair_columns(lane - LANE_POS, (pos // SEL_LEN).astype(F32), (pos % SEL_LEN).astype(F32))
    kv = _dot(n, wkv_ref[...])
    for c in range(kv.shape[1] // LANES):
        kvb_ref[:, c * LANES:(c + 1) * LANES] = (kv[:, c * LANES:(c + 1) * LANES] + posc).astype(BF16)


def _proj(x2, g, wa, wc, wq, wkv, wng, wgab, wvt, *, s_len, tm=2 * SLC_TILE):
    t_len, d = x2.shape
    bsz = t_len // s_len
    nrt = s_len // tm
    const = lambda i: (0, 0)
    row = lambda i: (i, 0)
    ws = (wa, wc, wq, wkv, wng, wgab, wvt)
    flat = (wc, wq, wkv, wng, wgab)
    flat_dtypes = (F32, BF16, BF16, F32, BF16)
    aw = wa.shape[1]
    a_specs = [pl.BlockSpec((None, dil, tm // dil, aw), lambda i: (i // nrt, 0, i % nrt, 0))
               for _, dil in A_CONFIGS]
    a_shapes = [jax.ShapeDtypeStruct((bsz, dil, s_len // dil, aw), BF16) for _, dil in A_CONFIGS]
    return pl.pallas_call(
        functools.partial(_proj_body, tm=tm, s_len=s_len),
        grid=(t_len // tm,),
        in_specs=[pl.BlockSpec((tm, d), row), pl.BlockSpec((1, d), const)]
                 + [pl.BlockSpec(w.shape, const, pipeline_mode=pl.Buffered(1)) for w in ws],
        out_specs=a_specs + [pl.BlockSpec((tm, w.shape[1]), row) for w in flat]
                  + [pl.BlockSpec((None, tm // width, B_KV_GROUPS, VT_ROWS, width),
                                  lambda i: (i // nrt, i % nrt, 0, 0, 0)) for width in (SLC_TILE, Q_BLK)],
        out_shape=a_shapes + [jax.ShapeDtypeStruct((t_len, w.shape[1]), dt) for w, dt in zip(flat, flat_dtypes)]
                  + [jax.ShapeDtypeStruct((bsz, s_len // width, B_KV_GROUPS, VT_ROWS, width), BF16)
                     for width in (SLC_TILE, Q_BLK)],
        scratch_shapes=[pltpu.VMEM((aw // LANES, tm, LANES), F32),
                        pltpu.VMEM((A_CONFIGS[1][1], tm // A_CONFIGS[1][1], LANES), F32)],
        compiler_params=pltpu.CompilerParams(dimension_semantics=("arbitrary",),
                                             vmem_limit_bytes=VMEM_LIMIT),
        name="proj",
    )(x2, g, *ws)


def _gelu_tanh(x):
    return 0.5 * x * (1.0 + jnp.tanh(np.sqrt(2.0 / np.pi).astype(np.float32) * (x + 0.044715 * (x * x * x))))


def _compress_body(x_ref, pe_ref, w1_ref, w2_ref, out_ref, *, ncp):
    first = jnp.zeros((ncp, B_KV_GROUPS * CMP_HIDDEN), F32)
    second = jnp.zeros((ncp, B_KV_GROUPS * CMP_HIDDEN), F32)
    for l in range(CMP_STRIDE):
        x_l = x_ref[pl.ds(l, ncp, stride=CMP_STRIDE), :]
        first = first + _dot((x_l + pe_ref[l:l + 1, :]).astype(BF16), w1_ref[l])
        second = second + _dot((x_l + pe_ref[CMP_STRIDE + l:CMP_STRIDE + l + 1, :]).astype(BF16),
                               w1_ref[CMP_STRIDE + l])
    pre = first + jnp.concatenate([second[1:], second[:1]], axis=0)
    hid = _gelu_tanh(pre).astype(BF16)
    is_key = pl.program_id(1) == 0
    n_idx = _iota((ncp, LANES), 0)
    lane = _iota((ncp, LANES), 1)
    nc = _pair_columns(lane - LANE_CMP, (n_idx // CMP_IDX_RADIX).astype(F32), (n_idx % CMP_IDX_RADIX).astype(F32))
    nc = jnp.where(is_key, nc, 0.0)
    for g in range(B_KV_GROUPS):
        out = _dot(hid[:, g * CMP_HIDDEN:(g + 1) * CMP_HIDDEN], w2_ref[g])
        out_ref[g] = (out + nc).astype(BF16)


def _compress(kvc, pe2, w1bd, w2s, *, bsz, s_len):
    ncp = s_len // CMP_STRIDE
    return pl.pallas_call(
        functools.partial(_compress_body, ncp=ncp),
        grid=(bsz, 2),
        in_specs=[pl.BlockSpec((s_len, LANES), lambda b, kv: (b, kv)),
                  pl.BlockSpec((None,) + pe2.shape[1:], lambda b, kv: (kv, 0, 0)),
                  pl.BlockSpec((None,) + w1bd.shape[1:], lambda b, kv: (kv, 0, 0, 0)),
                  pl.BlockSpec((B_KV_GROUPS, CMP_HIDDEN, LANES), lambda b, kv: (kv, 0, 0))],
        out_specs=pl.BlockSpec((None, B_KV_GROUPS, ncp, LANES), lambda b, kv: (b, kv, 0, 0)),
        out_shape=jax.ShapeDtypeStruct((bsz, 2 * B_KV_GROUPS, ncp, LANES), BF16),
        compiler_params=pltpu.CompilerParams(dimension_semantics=("arbitrary", "arbitrary"),
                                             vmem_limit_bytes=VMEM_LIMIT),
        name="compress",
    )(kvc, pe2, w1bd, w2s)


def _head_slope(h, n_heads):
    out = jnp.float32(2.0 ** (-8.0 * n_heads / n_heads))
    for k in range(n_heads - 1):
        out = jnp.where(h == k, jnp.float32(2.0 ** (-8.0 * (k + 1) / n_heads)), out)
    return out


def _dilated_body(*refs):
    ncfg = len(A_CONFIGS)
    in_refs = refs[:5 * ncfg]
    out_ref = refs[5 * ncfg]
    o_sc, l_sc, m_sc, bias_sc = refs[5 * ncfg + 1:5 * ncfg + 5]
    kbufs = refs[5 * ncfg + 5:5 * ncfg + 5 + ncfg]
    vbufs = refs[5 * ncfg + 5 + ncfg:]
    hp = pl.program_id(1)
    first_tile = pl.program_id(2) == 0
    low = _iota((1, LANES), 1) < HEAD_DIM

    @pl.when(first_tile)
    def _():
        qi = _iota((Q_BLK, 2 * Q_BLK), 0)
        kj = _iota((Q_BLK, 2 * Q_BLK), 1)
        dist = qi - kj + Q_BLK
        for c, (window, dil) in enumerate(A_CONFIGS):
            valid = (dist >= 0) & (dist <= window // dil)
            for hh in range(2):
                slope = _head_slope(2 * hp + hh, A_HEADS)
                bias = jnp.where(valid, -(slope * dil * LOG2E) * dist.astype(F32), NEG)
                bias_sc[(c * 2 + hh) * 2] = bias
                bias_sc[(c * 2 + hh) * 2 + 1] = jnp.where(kj < Q_BLK, NEG, bias)

    for c, (window, dil) in enumerate(A_CONFIGS):
        q_ref, k_ref, v_ref, kh_ref, vh_ref = in_refs[5 * c:5 * c + 5]
        kbuf, vbuf = kbufs[c], vbufs[c]
        rows = A_TILE // dil
        nsub = rows // Q_BLK
        kbuf[:, :Q_BLK, :] = kh_ref[...]
        kbuf[:, Q_BLK:, :] = k_ref[...]
        vbuf[:, :Q_BLK, :] = vh_ref[...]
        vbuf[:, Q_BLK:, :] = v_ref[...]

        def group(gidx, carry, c=c, dil=dil, nsub=nsub, q_ref=q_ref, kbuf=kbuf, vbuf=vbuf):
            subs = []
            for u in range(A_GROUP):
                idx = gidx * A_GROUP + u
                r = idx // nsub
                j = idx % nsub
                j0 = pl.multiple_of(j * Q_BLK, Q_BLK)
                seq_start = ((j == 0) & first_tile).astype(jnp.int32)
                subs.append((r, j, j0, seq_start))
            scores = []
            for r, j, j0, seq_start in subs:
                q = q_ref[r, pl.ds(j0, Q_BLK), :]
                k2 = kbuf[r, pl.ds(j0, 2 * Q_BLK), :]
                for hh in range(2):
                    qm = jnp.where(low == (hh == 0), q, jnp.zeros_like(q))
                    scores.append(_dot_nt(qm, k2) + bias_sc[(c * 2 + hh) * 2 + seq_start])
            probs = []
            for s in scores:
                m = jnp.max(s, axis=-1, keepdims=True)
                e = jnp.exp2(s - m)
                probs.append((e.astype(BF16), m, jnp.sum(e, axis=-1, keepdims=True)))
            for u, (r, j, j0, seq_start) in enumerate(subs):
                v2 = vbuf[r, pl.ds(j0, 2 * Q_BLK), :]
                outs = [_dot(probs[2 * u + hh][0], v2) for hh in range(2)]
                stats = [[jnp.broadcast_to(probs[2 * u + hh][i], (Q_BLK, LANES)) for hh in range(2)] for i in (1, 2)]
                row0 = j * (Q_BLK * dil) + r
                dst = pl.ds(row0, Q_BLK, stride=dil) if dil > 1 else pl.ds(row0, Q_BLK)
                o_sc[c, dst, :] = jnp.where(low, outs[0], outs[1])
                m_sc[c, dst, :] = jnp.where(low, stats[0][0], stats[0][1])
                l_sc[c, dst, :] = jnp.where(low, stats[1][0], stats[1][1])
            return carry

        lax.fori_loop(0, dil * nsub // A_GROUP, group, 0)

    m = jnp.maximum(jnp.maximum(m_sc[0], m_sc[1]), m_sc[2])
    num = jnp.zeros((A_TILE, LANES), F32)
    den = jnp.zeros((A_TILE, LANES), F32)
    for c in range(ncfg):
        e = jnp.exp2(m_sc[c] - m)
        num = num + e * o_sc[c]
        den = den + e * l_sc[c]
    out_ref[...] = (num / den).astype(BF16)


def _dilated(qkv_by_cfg, *, bsz, s_len):
    nt = s_len // A_TILE
    npair = A_HEADS // 2
    ncfg = len(A_CONFIGS)
    in_specs, operands, kv_scratch = [], [], []
    for (window, dil), arr in zip(A_CONFIGS, qkv_by_cfg):
        rows = A_TILE // dil
        nsub = rows // Q_BLK
        cur = lambda off: (lambda b, h, t: (b, 0, t, off + h))
        halo = lambda off, nsub=nsub: (lambda b, h, t: (b, 0, jnp.maximum(t * nsub - 1, 0), off + h))
        in_specs += [pl.BlockSpec((None, dil, rows, LANES), cur(0)),
                     pl.BlockSpec((None, dil, rows, LANES), cur(npair)),
                     pl.BlockSpec((None, dil, rows, LANES), cur(2 * npair)),
                     pl.BlockSpec((None, dil, Q_BLK, LANES), halo(npair)),
                     pl.BlockSpec((None, dil, Q_BLK, LANES), halo(2 * npair))]
        operands += [arr] * 5
        kv_scratch.append(pltpu.VMEM((dil, Q_BLK + rows, LANES), BF16))
    return pl.pallas_call(
        _dilated_body,
        grid=(bsz, npair, nt),
        in_specs=in_specs,
        out_specs=pl.BlockSpec((A_TILE, LANES), lambda b, h, t: (b * nt + t, h)),
        out_shape=jax.ShapeDtypeStruct((bsz * s_len, A_HEADS * HEAD_DIM), BF16),
        scratch_shapes=[pltpu.VMEM((ncfg, A_TILE, LANES), F32)] * 3
                       + [pltpu.VMEM((ncfg * 4, Q_BLK, 2 * Q_BLK), F32)] + kv_scratch + kv_scratch,
        compiler_params=pltpu.CompilerParams(dimension_semantics=("arbitrary",) * 3,
                                             vmem_limit_bytes=VMEM_LIMIT),
        name="dilated",
    )(*operands)


def _q_aug(q_all, g, lane, low):
    rows = []
    for r in range(B_REP):
        h = g * B_REP + r
        blk = q_all[:, (h // 2) * LANES:(h // 2 + 1) * LANES]
        if h % 2 == 1:
            blk = pltpu.roll(blk, HEAD_DIM, axis=1)
        rows.append(jnp.where(low, blk, _query_alibi_row(2.0 ** (-8.0 * (h + 1) / B_HEADS), lane)))
    return jnp.concatenate(rows, axis=0).astype(BF16)


def _place_heads(per_head, low):
    placed = []
    for h, o in enumerate(per_head):
        placed.append(pltpu.roll(o, HEAD_DIM, axis=1) if h % 2 != h // B_REP else o)
    return [jnp.where(low, placed[2 * hp], placed[2 * hp + 1]) for hp in range(B_HEADS // 2)]


def _nsa_select_body(q_ref, cmp_ref, ng_ref, ovt_ref, gmap_ref, selb_ref, oc_ref, flags_ref, *, stretch, topk):
    ncw = min(cmp_ref.shape[1], (stretch + 1) * SEL_SPAN // CMP_STRIDE)
    nblk = min(LANES, (stretch + 1) * SEL_SPAN // SEL_LEN)
    q0 = stretch * SEL_SPAN + pl.program_id(1) * Q_BLK
    lane = _iota((1, LANES), 1)
    low = lane < HEAD_DIM
    t_col = q0 + _iota((Q_BLK, 1), 0)
    t_row = q0 + _iota((1, Q_BLK), 1)
    groups = range(B_KV_GROUPS)

    q_all = q_ref[...].astype(F32)
    qg = [_q_aug(q_all, g, lane, low) for g in groups]
    has_key = (t_col >= CMP_LEN - 1)[None]

    cmp_end = CMP_STRIDE * _iota((1, ncw), 1) + (CMP_LEN - 1)
    bias_c = jnp.where(cmp_end <= t_col, 0.0, NEG)
    blk_t = _iota((nblk, 1), 0)
    allowed_t = blk_t * SEL_LEN <= t_row
    cur_t = t_row // SEL_LEN
    forced_t = (blk_t == 0) | (blk_t == cur_t) | (blk_t == cur_t - 1)
    blk_f = blk_t.astype(F32)

    scores = [_dot_nt(qg[g], cmp_ref[g, :ncw, :]).reshape(B_REP, Q_BLK, ncw) + bias_c[None] for g in groups]
    probs = []
    for s in scores:
        e = jnp.exp2(s - jnp.max(s, axis=-1, keepdims=True))
        den = jnp.sum(e, axis=-1, keepdims=True)
        probs.append(e * jnp.where(has_key, 1.0 / den, 0.0))
    o_cmp = [_dot(probs[g].reshape(B_REP * Q_BLK, ncw).astype(BF16), cmp_ref[B_KV_GROUPS + g, :ncw, :])
             for g in groups]
    ranks = []
    for p in probs:
        psum = p[0] + p[1] + p[2] + p[3]
        p_hi = psum.astype(BF16)
        p_lo = (psum - p_hi.astype(F32)).astype(BF16)
        ov = ovt_ref[:nblk, :ncw]
        imp_t = _dot_nt(ov, p_hi) + _dot_nt(ov, p_lo)
        ranks.append(jnp.where(allowed_t, imp_t + jnp.where(forced_t, FORCE, 0.0), NEG))

    def pick_one(_, carry):
        out = []
        for rank, sel in carry:
            best = jnp.max(rank, axis=0, keepdims=True)
            cand = jnp.where(rank == best, blk_f, float(LANES))
            idx = jnp.min(cand, axis=0, keepdims=True)
            pick = blk_f == idx
            out.append((jnp.where(pick, PICKED, rank), jnp.where(pick, 1.0, sel)))
        return tuple(out)

    picked = lax.fori_loop(0, topk, pick_one, tuple((rank, jnp.zeros((nblk, Q_BLK), F32)) for rank in ranks))

    ones = jnp.ones((SUBLANES, Q_BLK), BF16)
    tile_cnt = jnp.zeros((SUBLANES, LANES), F32)
    for g in groups:
        sel_t = jnp.where(allowed_t, picked[g][1], 0.0)
        if nblk < LANES:
            sel_t = jnp.concatenate([sel_t, jnp.zeros((LANES - nblk, Q_BLK), F32)], axis=0)
        sel = sel_t.T
        selb_ref[:, g * LANES:(g + 1) * LANES] = jnp.where(sel > 0.5, 0.0, -MASK_BIG).astype(BF16)
        per_block = _dot(ones, sel.astype(BF16))
        tile_cnt = tile_cnt + _dot(per_block.astype(BF16), gmap_ref[g])
    flags_ref[...] = (tile_cnt > 0.5).astype(jnp.int32)

    sig = jax.nn.sigmoid(ng_ref[...])
    per_head = []
    for h in range(B_HEADS):
        g, r = divmod(h, B_REP)
        per_head.append(sig[:, 3 * h:3 * h + 1] * o_cmp[g][r * Q_BLK:(r + 1) * Q_BLK])
    for hp, tile in enumerate(_place_heads(per_head, low)):
        oc_ref[:, hp * LANES:(hp + 1) * LANES] = tile


def _nsa_attend_body(lists_ref, cnts_ref, q_ref, ks0_ref, ks1_ref, kw0_ref, kw1_ref, vst_ref, vwt_ref, *rest,
                     nq, ntile, nstretch):
    selb_refs, oc_refs = rest[:nstretch], rest[nstretch:2 * nstretch]
    ng_ref, oh_ref, out_ref = rest[2 * nstretch:]
    step = pl.program_id(0) * nq + pl.program_id(1)
    q0 = pl.program_id(1) * Q_BLK
    stretch = q0 // SEL_SPAN

    def from_select(refs, cols):
        val = refs[0][:, cols]
        for k in range(1, nstretch):
            val = jnp.where(stretch == k, refs[k][:, cols], val)
        return val

    lane = _iota((1, LANES), 1)
    low = lane < HEAD_DIM
    t_row = q0 + _iota((1, Q_BLK), 1)
    win_keys = WIN + Q_BLK
    rows = B_REP * Q_BLK
    groups = range(B_KV_GROUPS)
    ks_refs = (ks0_ref, ks1_ref)
    kw_refs = (kw0_ref, kw1_ref)

    q_all = q_ref[...].astype(F32)
    qg = [_q_aug(q_all, g, lane, low) for g in groups]

    def heads_on_rows(o_t):
        o_t = jnp.concatenate([o_t, o_t], axis=0)
        return jnp.concatenate([o_t[:, r * Q_BLK:(r + 1) * Q_BLK].T for r in range(B_REP)], axis=0)

    def softmax_step(s, m_i):
        m_new = jnp.maximum(m_i, jnp.max(s, axis=0, keepdims=True))
        return jnp.exp2(s - m_new).astype(BF16), jnp.exp2(m_i - m_new), m_new

    q_slc = []
    for g in groups:
        sel_bias = from_select(selb_refs, slice(g * LANES, (g + 1) * LANES))
        q_slc.append(jnp.concatenate([qg[g], jnp.concatenate([sel_bias] * B_REP, axis=0)], axis=1))

    def qk(g, kt):
        k0 = pl.multiple_of(jnp.minimum(kt, ntile - 1) * SLC_TILE, SLC_TILE)
        o0 = pl.multiple_of(kt * SLC_TILE, SLC_TILE)
        k_aug = jnp.concatenate([ks_refs[g][pl.ds(k0, SLC_TILE), :], oh_ref[pl.ds(o0, SLC_TILE), :]], axis=1)
        return _dot_nt(k_aug, q_slc[g])

    def pv(g, p, kt):
        return _dot(vst_ref[jnp.minimum(kt, ntile - 1), g], p)

    def trip(k, carry):
        work = [(g, lists_ref[(step * B_KV_GROUPS + g) * LIST_W + k * SLC_TRIP + u])
                for u in range(SLC_TRIP) for g in groups]
        scores = [qk(g, kt) for g, kt in work]
        state = list(carry)
        for (g, kt), s in zip(work, scores):
            m_i, acc = state[g]
            p, a, m_i = softmax_step(s, m_i)
            state[g] = (m_i, a * acc + pv(g, p, kt))
        return tuple(state)

    n_trips = jnp.maximum(cnts_ref[step * B_KV_GROUPS], cnts_ref[step * B_KV_GROUPS + 1]) // SLC_TRIP
    init = tuple((jnp.full((1, rows), PICKED, F32), jnp.zeros((VT_ROWS, rows), F32)) for g in groups)
    state = lax.fori_loop(0, n_trips, trip, init)

    kt_diag = q0 // SLC_TILE
    key_pos = kt_diag * SLC_TILE + _iota((SLC_TILE, 1), 0)
    causal = jnp.where(key_pos <= t_row, 0.0, NEG)
    causal = jnp.concatenate([causal] * B_REP, axis=1)
    scores = [qk(g, kt_diag) + causal for g in groups]
    o_slc = []
    for g in groups:
        m_i, acc = state[g]
        p, a, m_i = softmax_step(scores[g], m_i)
        acc = a * acc + pv(g, p, kt_diag)
        o_slc.append(heads_on_rows(acc[:HEAD_DIM] / acc[HEAD_DIM:HEAD_DIM + 1]))

    kstart = pl.multiple_of(jnp.maximum(q0 - WIN, 0), Q_BLK)
    d_w = t_row - (kstart + _iota((win_keys, 1), 0))
    bias_w = jnp.where((d_w >= 0) & (d_w < WIN), 0.0, NEG)
    bias_w = jnp.concatenate([bias_w] * B_REP, axis=1)
    scores = [_dot_nt(kw_refs[g][pl.ds(kstart, win_keys), :], qg[g]) + bias_w for g in groups]
    o_win = []
    for g in groups:
        s = scores[g]
        p = jnp.exp2(s - jnp.max(s, axis=0, keepdims=True)).astype(BF16)
        v_t = jnp.concatenate([vwt_ref[kstart // Q_BLK + u, g] for u in range(win_keys // Q_BLK)], axis=1)
        acc = _dot(v_t, p)
        o_win.append(heads_on_rows(acc[:HEAD_DIM] / acc[HEAD_DIM:HEAD_DIM + 1]))

    sig = jax.nn.sigmoid(ng_ref[...])
    per_head = []
    for h in range(B_HEADS):
        g, r = divmod(h, B_REP)
        rs = slice(r * Q_BLK, (r + 1) * Q_BLK)
        per_head.append(sig[:, 3 * h + 1:3 * h + 2] * o_slc[g][rs] + sig[:, 3 * h + 2:3 * h + 3] * o_win[g][rs])
    for hp, tile in enumerate(_place_heads(per_head, low)):
        cols = slice(hp * LANES, (hp + 1) * LANES)
        out_ref[:, cols] = (from_select(oc_refs, cols) + tile).astype(BF16)


def _tile_lists(flags, *, bsz, s_len):
    nq = s_len // Q_BLK
    ntile = s_len // SLC_TILE
    f = flags[:, 0, :B_KV_GROUPS * FLAG_W].reshape(bsz * nq, B_KV_GROUPS, FLAG_W)[:, :, :ntile] > 0
    kt = jnp.arange(ntile, dtype=jnp.int32)
    diag = jnp.tile((jnp.arange(nq, dtype=jnp.int32) * Q_BLK) // SLC_TILE, bsz)[:, None, None]
    touched = f & (kt < diag)
    rank = jnp.cumsum(touched, axis=-1, dtype=jnp.int32) - 1
    n_touched = rank[..., -1] + 1
    cnt = SLC_TRIP * ((n_touched + SLC_TRIP - 1) // SLC_TRIP)
    pos = jnp.arange(LIST_W, dtype=jnp.int32)
    hit = touched[..., None, :] & (rank[..., None, :] == pos[:, None])
    order = jnp.sum(jnp.where(hit, kt, 0), axis=-1)
    lists = jnp.where(pos < n_touched[..., None], order, ntile)
    return lists.reshape(-1), cnt.reshape(-1)


def _nsa(qb, kvb, vst, vwt, cmp, ng, onehot, ovt, gmap, *, bsz, s_len):
    nq = s_len // Q_BLK
    ntile = s_len // SLC_TILE
    ncp = cmp.shape[2]
    topk = min(SEL_TOPK, s_len // SEL_LEN)
    hw = B_HEADS * HEAD_DIM
    params = pltpu.CompilerParams(dimension_semantics=("arbitrary", "arbitrary"), vmem_limit_bytes=VMEM_LIMIT)
    span_q = min(SEL_SPAN, s_len) // Q_BLK
    nstretch = nq // span_q
    selbs, ocs, flag_parts = [], [], []
    for k in range(nstretch):
        src = lambda b, i, k=k: (b * nq + k * span_q + i, 0)
        dst = lambda b, i: (b * span_q + i, 0)
        selb, oc, flags = pl.pallas_call(
            functools.partial(_nsa_select_body, stretch=k, topk=topk),
            grid=(bsz, span_q),
            in_specs=[pl.BlockSpec((Q_BLK, hw), src),
                      pl.BlockSpec((None, 2 * B_KV_GROUPS, ncp, LANES), lambda b, i: (b, 0, 0, 0)),
                      pl.BlockSpec((Q_BLK, LANES), src),
                      pl.BlockSpec(ovt.shape, lambda b, i: (0, 0)),
                      pl.BlockSpec(gmap.shape, lambda b, i: (0, 0, 0))],
            out_specs=[pl.BlockSpec((Q_BLK, B_KV_GROUPS * LANES), dst),
                       pl.BlockSpec((Q_BLK, hw), dst),
                       pl.BlockSpec((None, None, SUBLANES, LANES), lambda b, i: (b, i, 0, 0))],
            out_shape=[jax.ShapeDtypeStruct((bsz * span_q * Q_BLK, B_KV_GROUPS * LANES), BF16),
                       jax.ShapeDtypeStruct((bsz * span_q * Q_BLK, hw), F32),
                       jax.ShapeDtypeStruct((bsz, span_q, SUBLANES, LANES), jnp.int32)],
            compiler_params=params,
            name=f"nsa_select{k}",
        )(qb, cmp, ng, ovt, gmap)
        selbs.append(selb)
        ocs.append(oc)
        flag_parts.append(flags)
    flags = jnp.concatenate(flag_parts, axis=1).reshape(bsz * nq, SUBLANES, LANES)

    lists, cnts = _tile_lists(flags, bsz=bsz, s_len=s_len)
    rowblk2 = lambda b, i, lists, cnts: (b * nq + i, 0)
    res = lambda col: (lambda b, i, lists, cnts: (b, col))
    part = lambda k: (lambda b, i, lists, cnts: (b * span_q + jnp.clip(i - k * span_q, 0, span_q - 1), 0))
    return pl.pallas_call(
        functools.partial(_nsa_attend_body, nq=nq, ntile=ntile, nstretch=nstretch),
        grid_spec=pltpu.PrefetchScalarGridSpec(
            num_scalar_prefetch=2,
            grid=(bsz, nq),
            in_specs=[pl.BlockSpec((Q_BLK, hw), rowblk2)]
                     + [pl.BlockSpec((s_len, LANES), res(col)) for col in range(2 * B_KV_GROUPS)]
                     + [pl.BlockSpec((None,) + v.shape[1:], lambda b, i, lists, cnts: (b, 0, 0, 0, 0))
                        for v in (vst, vwt)]
                     + [pl.BlockSpec((Q_BLK, B_KV_GROUPS * LANES), part(k)) for k in range(nstretch)]
                     + [pl.BlockSpec((Q_BLK, hw), part(k)) for k in range(nstretch)]
                     + [pl.BlockSpec((Q_BLK, LANES), rowblk2),
                        pl.BlockSpec(onehot.shape, lambda b, i, lists, cnts: (0, 0))],
            out_specs=pl.BlockSpec((Q_BLK, hw), rowblk2)),
        out_shape=jax.ShapeDtypeStruct((bsz * s_len, hw), BF16),
        compiler_params=params,
        name="nsa_attend",
    )(lists, cnts, qb, kvb, kvb, kvb, kvb, vst, vwt, *selbs, *ocs, ng, onehot)


def _post_body(x_ref, oa_ref, ob_ref, gab_ref, p_ref, wua_ref, wub_ref, wout_ref, g2_ref, w1_ref, w2_ref,
               g3_ref, wpg_ref, wple_ref, gf_ref, out_ref, *, d, ff_chunk):
    ya = _dot(oa_ref[...], wua_ref[...])
    yb = _dot(ob_ref[...], wub_ref[...])
    mixed = (jax.nn.sigmoid(gab_ref[:, :d].astype(F32)) * ya
             + jax.nn.sigmoid(gab_ref[:, d:].astype(F32)) * yb)
    h = x_ref[...] + _dot(mixed.astype(BF16), wout_ref[...])
    n2 = _rms(h, g2_ref[...]).astype(BF16)
    acc = h
    for c in range(w1_ref.shape[1] // ff_chunk):
        cs = slice(c * ff_chunk, (c + 1) * ff_chunk)
        hid = jnp.square(jnp.maximum(_dot(n2, w1_ref[:, cs]), 0.0))
        acc = acc + _dot(hid.astype(BF16), w2_ref[cs, :])
    n3 = _rms(acc, g3_ref[...]).astype(BF16)
    gate = jax.nn.sigmoid(_dot(n3, wpg_ref[...]))
    h3 = acc + gate * _dot(p_ref[...].astype(BF16), wple_ref[...])
    out_ref[...] = _rms(h3, gf_ref[...])


def _post(x2, oa, ob, gab, p2, wua, wub, wout, g2, w1, w2, g3, wpg, wple, gf, *, tm=512, ff_chunk=1024):
    t_len, d = x2.shape
    row = lambda i: (i, 0)
    const = lambda i: (0, 0)
    resident = lambda a: pl.BlockSpec(a.shape, const, pipeline_mode=pl.Buffered(1))
    acts = (x2, oa, ob, gab, p2)
    params = (wua, wub, wout, g2, w1, w2, g3, wpg, wple, gf)
    return pl.pallas_call(
        functools.partial(_post_body, d=d, ff_chunk=ff_chunk),
        grid=(t_len // tm,),
        in_specs=[pl.BlockSpec((tm, a.shape[1]), row) for a in acts] + [resident(w) for w in params],
        out_specs=pl.BlockSpec((tm, d), row),
        out_shape=jax.ShapeDtypeStruct((t_len, d), F32),
        compiler_params=pltpu.CompilerParams(dimension_semantics=("arbitrary",),
                                             vmem_limit_bytes=VMEM_LIMIT),
        name="post",
    )(*acts, *params)


def _selection_overlap_t(ncp, s_len):
    ncmp = (s_len - CMP_LEN) // CMP_STRIDE + 1
    nsel = s_len // SEL_LEN
    ratio = SEL_LEN // CMP_STRIDE
    span = CMP_LEN // CMP_STRIDE
    i = np.arange(ncmp)[:, None]
    j = np.arange(nsel)[None, :]
    ov = np.maximum(np.minimum(i + span, ratio * (j + 1)) - np.maximum(i, ratio * j), 0)
    out = np.zeros((LANES, ncp), np.float32)
    out[:nsel, :ncmp] = ov.T
    return out


def _layer(h, p_i, norm_mix_g, w_in, pe_ck, w_ck1, w_ck2, pe_cv, w_cv1, w_cv2, w_up_a, w_up_b, w_out,
           norm_mlp_g, w_mlp1, w_mlp2, norm_ple_g, w_ple_gate, w_ple, final_g):
    bsz, s_len, d = h.shape
    t_len = bsz * s_len
    aw = A_HEADS * HEAD_DIM
    bw = B_HEADS * HEAD_DIM
    kvw = B_KV_GROUPS * HEAD_DIM
    assert s_len % A_TILE == 0 and s_len % SEL_SPAN == 0 and kvw == LANES
    assert s_len // SEL_LEN <= LANES and s_len // SLC_TILE <= FLAG_W

    o_qb = 3 * aw
    o_kv = o_qb + bw
    o_ng = o_kv + 6 * kvw
    o_ga = o_ng + 3 * B_HEADS
    kv = lambda i: w_in[:, o_kv + i * kvw:o_kv + (i + 1) * kvw]
    zeros_h = jnp.zeros((d, HEAD_DIM), w_in.dtype)
    grp = lambda w, g: jnp.concatenate([w[:, g * HEAD_DIM:(g + 1) * HEAD_DIM], zeros_h], axis=1)
    wa = jnp.concatenate([w_in[:, :aw] * (SCALE * LOG2E), w_in[:, aw:3 * aw]], axis=1)
    wc = jnp.concatenate([kv(0), kv(1)], axis=1)
    wq = w_in[:, o_qb:o_qb + bw] * (SCALE * LOG2E)
    wkv = jnp.concatenate([grp(kv(2), 0), grp(kv(2), 1), grp(kv(4), 0), grp(kv(4), 1)], axis=1)
    wvt = jnp.concatenate([kv(3), kv(5)], axis=1).T
    wng = jnp.concatenate([w_in[:, o_ng:o_ga], jnp.zeros((d, LANES - 3 * B_HEADS), w_in.dtype)], axis=1)
    wgab = w_in[:, o_ga:]
    x2 = h.reshape(t_len, d)
    a0, a1, a2, kvc, qb, kvb, ng, gab, vst, vwt = _proj(x2, norm_mix_g.reshape(1, d), *(w.astype(BF16) for w in
                                                        (wa, wc, wq, wkv, wng, wgab, wvt)), s_len=s_len)

    ncp = s_len // CMP_STRIDE
    pe2 = jnp.stack([jnp.tile(pe, (1, B_KV_GROUPS)) for pe in (pe_ck, pe_cv)])

    def per_row_block_diag(w1):
        w = w1.reshape(CMP_LEN, HEAD_DIM, CMP_HIDDEN)
        z = jnp.zeros_like(w)
        return jnp.concatenate([jnp.concatenate([w, z], axis=2), jnp.concatenate([z, w], axis=2)], axis=1)

    w1bd = jnp.stack([per_row_block_diag(w_ck1), per_row_block_diag(w_cv1)]).astype(BF16)
    zpad = jnp.zeros((CMP_HIDDEN, HEAD_DIM), w_ck2.dtype)
    w2s = jnp.stack([jnp.concatenate([w_ck2, zpad], axis=1), jnp.concatenate([w_ck2, zpad], axis=1),
                     jnp.concatenate([w_cv2, zpad], axis=1), jnp.concatenate([zpad, w_cv2], axis=1)]).astype(BF16)
    cmp = _compress(kvc, pe2, w1bd, w2s, bsz=bsz, s_len=s_len)

    oa = _dilated((a0, a1, a2), bsz=bsz, s_len=s_len)
    onehot = (np.arange(s_len + SLC_TILE)[:, None] // SEL_LEN == np.arange(LANES)[None, :])
    onehot[s_len:] = True
    blocks_per_tile = SLC_TILE // SEL_LEN
    gmap = np.zeros((B_KV_GROUPS, LANES, LANES), np.float32)
    for g in range(B_KV_GROUPS):
        gmap[g, np.arange(LANES), FLAG_W * g + np.arange(LANES) // blocks_per_tile] = 1.0
    ob = _nsa(qb, kvb, vst, vwt, cmp, ng, jnp.asarray(onehot, BF16),
               jnp.asarray(_selection_overlap_t(ncp, s_len), BF16), jnp.asarray(gmap, BF16), bsz=bsz, s_len=s_len)

    b16 = lambda w: w.astype(BF16)
    row = lambda v: v.reshape(1, d)
    return _post(x2, oa, ob, gab, p_i.reshape(t_len, -1), b16(w_up_a), b16(w_up_b), b16(w_out), row(norm_mlp_g),
                 b16(w_mlp1), b16(w_mlp2), row(norm_ple_g), b16(w_ple_gate), b16(w_ple), row(final_g)
                 ).reshape(bsz, s_len, d)


def kernel(x, p, norm_mix_g, w_in, pe_ck, w_ck1, w_ck2, pe_cv, w_cv1, w_cv2, w_up_a, w_up_b, w_out,
           norm_mlp_g, w_mlp1, w_mlp2, norm_ple_g, w_ple_gate, w_ple, norm_final_g):
    depth = w_in.shape[0]
    assert depth == 1, "the fused tail applies the final norm inside the single layer"
    return _layer(x, p[0], norm_mix_g[0], w_in[0], pe_ck[0], w_ck1[0], w_ck2[0], pe_cv[0], w_cv1[0], w_cv2[0],
                  w_up_a[0], w_up_b[0], w_out[0], norm_mlp_g[0], w_mlp1[0], w_mlp2[0], norm_ple_g[0],
                  w_ple_gate[0], w_ple[0], norm_final_g)
```

```python
import functools

import numpy as np
import jax
import jax.numpy as jnp
from jax import lax
from jax.experimental import pallas as pl
from jax.experimental.pallas import tpu as pltpu

HEAD_DIM = 64
A_HEADS = 8
A_CONFIGS = ((128, 1), (512, 4), (2048, 16))
B_HEADS = 8
B_KV_GROUPS = 2
B_REP = B_HEADS // B_KV_GROUPS
CMP_LEN = 32
CMP_STRIDE = 16
CMP_HIDDEN = 256
SEL_LEN = 64
SEL_TOPK = 16
WIN = 512
Q_BLK = 128
EPS = 1e-6
NEG = -1e30
FORCE = 1e9
MASK_BIG = 2.0 ** 100
SCALE = HEAD_DIM ** -0.5

PICKED = -3e38
CMP_IDX_RADIX = 16

LANES = 128
SUBLANES = 8
BF16_SUBLANES = 16
A_TILE = 2048
A_GROUP = 8
SLC_TILE = 256
VT_ROWS = HEAD_DIM + BF16_SUBLANES
FLAG_W = 32
SEL_SPAN = 2048
SLC_TRIP = 3
LIST_W = FLAG_W + SLC_TRIP
VMEM_LIMIT = 56 * 1024 * 1024

ALIBI_TERMS = 3
LANE_POS = 64
LANE_CMP = LANE_POS + 2 * ALIBI_TERMS
LOG2E = 1.4426950408889634

F32 = jnp.float32
BF16 = jnp.bfloat16


def _dot(a, b):
    return jnp.dot(a, b, preferred_element_type=F32)


def _dot_nt(a, b):
    return lax.dot_general(a, b, (((1,), (1,)), ((), ())), preferred_element_type=F32)


def _rms(x, g):
    inv = lax.rsqrt(jnp.mean(x * x, axis=-1, keepdims=True) + EPS)
    return (x * inv) * g


def _iota(shape, dim, dtype=jnp.int32):
    return lax.broadcasted_iota(dtype, shape, dim)


def _pair_columns(rel_lane, even_val, odd_val):
    inside = (rel_lane >= 0) & (rel_lane < 2 * ALIBI_TERMS)
    return jnp.where(inside, jnp.where(rel_lane % 2 == 0, even_val, odd_val), 0.0)


def _bf16_pieces(x):
    pieces, rest = [], np.float64(x)
    for _ in range(ALIBI_TERMS):
        piece = np.float64(np.asarray(rest, np.float32).astype(jnp.bfloat16).astype(np.float32))
        pieces.append(float(piece))
        rest = rest - piece
    return pieces


def _query_alibi_row(slope, lane):
    row = jnp.zeros(lane.shape, F32)
    coeffs = ((LANE_POS, SEL_LEN * slope), (LANE_POS + 1, slope),
              (LANE_CMP, CMP_IDX_RADIX * CMP_STRIDE * slope), (LANE_CMP + 1, CMP_STRIDE * slope))
    for lane0, coeff in coeffs:
        for t, piece in enumerate(_bf16_pieces(coeff * LOG2E)):
            row = jnp.where(lane == lane0 + 2 * t, piece, row)
    return row


def _proj_body(x_ref, g_ref, wa_ref, wc_ref, wq_ref, wkv_ref, wng_ref, wgab_ref, wvt_ref,
               a0_ref, a1_ref, a2_ref, kvc_ref, qb_ref, kvb_ref, ng_ref, gab_ref, vst_ref, vwt_ref, res_sc, mid_sc,
               *, tm, s_len):
    n = _rms(x_ref[...], g_ref[...]).astype(BF16)
    vt = _dot_nt(wvt_ref[...], n)
    for branch, (out_ref, width) in enumerate(((vst_ref, SLC_TILE), (vwt_ref, Q_BLK))):
        for u in range(tm // width):
            for g in range(B_KV_GROUPS):
                r0 = (branch * B_KV_GROUPS + g) * HEAD_DIM
                out_ref[u, g, :HEAD_DIM, :] = vt[r0:r0 + HEAD_DIM, u * width:(u + 1) * width].astype(BF16)
                out_ref[u, g, HEAD_DIM:, :] = jnp.ones((VT_ROWS - HEAD_DIM, width), BF16)
    res = _dot(n, wa_ref[...])
    (_, d0), (_, d1), (_, d2) = A_CONFIGS
    step = d2 // d1
    for s in range(res.shape[1] // LANES):
        cols = slice(s * LANES, (s + 1) * LANES)
        res_sc[s] = res[:, cols]
        a0_ref[0, :, cols] = res[:, cols].astype(BF16)
        for r in range(d1):
            part = res_sc[s, pl.ds(r, tm // d1, stride=d1), :]
            a1_ref[r, :, cols] = part.astype(BF16)
            mid_sc[r] = part
            for r2 in range(step):
                a2_ref[r + d1 * r2, :, cols] = mid_sc[r, pl.ds(r2, tm // d2, stride=step), :].astype(BF16)
    kvc_ref[...] = _dot(n, wc_ref[...])
    qb_ref[...] = _dot(n, wq_ref[...]).astype(BF16)
    ng_ref[...] = _dot(n, wng_ref[...])
    gab_ref[...] = _dot(n, wgab_ref[...]).astype(BF16)
    pos = (pl.program_id(0) * tm) % s_len + _iota((tm, LANES), 0)
    lane = _iota((tm, LANES), 1)
    posc = _pair_columns(lane - LANE_POS, (pos // SEL_LEN).astype(F32), (pos % SEL_LEN).astype(F32))
    kv = _dot(n, wkv_ref[...])
    for c in range(kv.shape[1] // LANES):
        kvb_ref[:, c * LANES:(c + 1) * LANES] = (kv[:, c * LANES:(c + 1) * LANES] + posc).astype(BF16)


def _proj(x2, g, wa, wc, wq, wkv, wng, wgab, wvt, *, s_len, tm=2 * SLC_TILE):
    t_len, d = x2.shape
    bsz = t_len // s_len
    nrt = s_len // tm
    const = lambda i: (0, 0)
    row = lambda i: (i, 0)
    ws = (wa, wc, wq, wkv, wng, wgab, wvt)
    flat = (wc, wq, wkv, wng, wgab)
    flat_dtypes = (F32, BF16, BF16, F32, BF16)
    aw = wa.shape[1]
    a_specs = [pl.BlockSpec((None, dil, tm // dil, aw), lambda i: (i // nrt, 0, i % nrt, 0))
               for _, dil in A_CONFIGS]
    a_shapes = [jax.ShapeDtypeStruct((bsz, dil, s_len // dil, aw), BF16) for _, dil in A_CONFIGS]
    return pl.pallas_call(
        functools.partial(_proj_body, tm=tm, s_len=s_len),
        grid=(t_len // tm,),
        in_specs=[pl.BlockSpec((tm, d), row), pl.BlockSpec((1, d), const)]
                 + [pl.BlockSpec(w.shape, const, pipeline_mode=pl.Buffered(1)) for w in ws],
        out_specs=a_specs + [pl.BlockSpec((tm, w.shape[1]), row) for w in flat]
                  + [pl.BlockSpec((None, tm // width, B_KV_GROUPS, VT_ROWS, width),
                                  lambda i: (i // nrt, i % nrt, 0, 0, 0)) for width in (SLC_TILE, Q_BLK)],
        out_shape=a_shapes + [jax.ShapeDtypeStruct((t_len, w.shape[1]), dt) for w, dt in zip(flat, flat_dtypes)]
                  + [jax.ShapeDtypeStruct((bsz, s_len // width, B_KV_GROUPS, VT_ROWS, width), BF16)
                     for width in (SLC_TILE, Q_BLK)],
        scratch_shapes=[pltpu.VMEM((aw // LANES, tm, LANES), F32),
                        pltpu.VMEM((A_CONFIGS[1][1], tm // A_CONFIGS[1][1], LANES), F32)],
        compiler_params=pltpu.CompilerParams(dimension_semantics=("arbitrary",),
                                             vmem_limit_bytes=VMEM_LIMIT),
        name="proj",
    )(x2, g, *ws)


def _gelu_tanh(x):
    return 0.5 * x * (1.0 + jnp.tanh(np.sqrt(2.0 / np.pi).astype(np.float32) * (x + 0.044715 * (x * x * x))))


def _compress_body(x_ref, pe_ref, w1_ref, w2_ref, out_ref, *, ncp):
    first = jnp.zeros((ncp, B_KV_GROUPS * CMP_HIDDEN), F32)
    second = jnp.zeros((ncp, B_KV_GROUPS * CMP_HIDDEN), F32)
    for l in range(CMP_STRIDE):
        x_l = x_ref[pl.ds(l, ncp, stride=CMP_STRIDE), :]
        first = first + _dot((x_l + pe_ref[l:l + 1, :]).astype(BF16), w1_ref[l])
        second = second + _dot((x_l + pe_ref[CMP_STRIDE + l:CMP_STRIDE + l + 1, :]).astype(BF16),
                               w1_ref[CMP_STRIDE + l])
    pre = first + jnp.concatenate([second[1:], second[:1]], axis=0)
    hid = _gelu_tanh(pre).astype(BF16)
    is_key = pl.program_id(1) == 0
    n_idx = _iota((ncp, LANES), 0)
    lane = _iota((ncp, LANES), 1)
    nc = _pair_columns(lane - LANE_CMP, (n_idx // CMP_IDX_RADIX).astype(F32), (n_idx % CMP_IDX_RADIX).astype(F32))
    nc = jnp.where(is_key, nc, 0.0)
    for g in range(B_KV_GROUPS):
        out = _dot(hid[:, g * CMP_HIDDEN:(g + 1) * CMP_HIDDEN], w2_ref[g])
        out_ref[g] = (out + nc).astype(BF16)


def _compress(kvc, pe2, w1bd, w2s, *, bsz, s_len):
    ncp = s_len // CMP_STRIDE
    return pl.pallas_call(
        functools.partial(_compress_body, ncp=ncp),
        grid=(bsz, 2),
        in_specs=[pl.BlockSpec((s_len, LANES), lambda b, kv: (b, kv)),
                  pl.BlockSpec((None,) + pe2.shape[1:], lambda b, kv: (kv, 0, 0)),
                  pl.BlockSpec((None,) + w1bd.shape[1:], lambda b, kv: (kv, 0, 0, 0)),
                  pl.BlockSpec((B_KV_GROUPS, CMP_HIDDEN, LANES), lambda b, kv: (kv, 0, 0))],
        out_specs=pl.BlockSpec((None, B_KV_GROUPS, ncp, LANES), lambda b, kv: (b, kv, 0, 0)),
        out_shape=jax.ShapeDtypeStruct((bsz, 2 * B_KV_GROUPS, ncp, LANES), BF16),
        compiler_params=pltpu.CompilerParams(dimension_semantics=("arbitrary", "arbitrary"),
                                             vmem_limit_bytes=VMEM_LIMIT),
        name="compress",
    )(kvc, pe2, w1bd, w2s)


def _head_slope(h, n_heads):
    out = jnp.float32(2.0 ** (-8.0 * n_heads / n_heads))
    for k in range(n_heads - 1):
        out = jnp.where(h == k, jnp.float32(2.0 ** (-8.0 * (k + 1) / n_heads)), out)
    return out


def _dilated_body(*refs):
    ncfg = len(A_CONFIGS)
    in_refs = refs[:5 * ncfg]
    out_ref = refs[5 * ncfg]
    o_sc, l_sc, m_sc, bias_sc = refs[5 * ncfg + 1:5 * ncfg + 5]
    kbufs = refs[5 * ncfg + 5:5 * ncfg + 5 + ncfg]
    vbufs = refs[5 * ncfg + 5 + ncfg:]
    hp = pl.program_id(1)
    first_tile = pl.program_id(2) == 0
    low = _iota((1, LANES), 1) < HEAD_DIM

    @pl.when(first_tile)
    def _():
        qi = _iota((Q_BLK, 2 * Q_BLK), 0)
        kj = _iota((Q_BLK, 2 * Q_BLK), 1)
        dist = qi - kj + Q_BLK
        for c, (window, dil) in enumerate(A_CONFIGS):
            valid = (dist >= 0) & (dist <= window // dil)
            for hh in range(2):
                slope = _head_slope(2 * hp + hh, A_HEADS)
                bias = jnp.where(valid, -(slope * dil * LOG2E) * dist.astype(F32), NEG)
                bias_sc[(c * 2 + hh) * 2] = bias
                bias_sc[(c * 2 + hh) * 2 + 1] = jnp.where(kj < Q_BLK, NEG, bias)

    for c, (window, dil) in enumerate(A_CONFIGS):
        q_ref, k_ref, v_ref, kh_ref, vh_ref = in_refs[5 * c:5 * c + 5]
        kbuf, vbuf = kbufs[c], vbufs[c]
        rows = A_TILE // dil
        nsub = rows // Q_BLK
        kbuf[:, :Q_BLK, :] = kh_ref[...]
        kbuf[:, Q_BLK:, :] = k_ref[...]
        vbuf[:, :Q_BLK, :] = vh_ref[...]
        vbuf[:, Q_BLK:, :] = v_ref[...]

        def group(gidx, carry, c=c, dil=dil, nsub=nsub, q_ref=q_ref, kbuf=kbuf, vbuf=vbuf):
            subs = []
            for u in range(A_GROUP):
                idx = gidx * A_GROUP + u
                r = idx // nsub
                j = idx % nsub
                j0 = pl.multiple_of(j * Q_BLK, Q_BLK)
                seq_start = ((j == 0) & first_tile).astype(jnp.int32)
                subs.append((r, j, j0, seq_start))
            scores = []
            for r, j, j0, seq_start in subs:
                q = q_ref[r, pl.ds(j0, Q_BLK), :]
                k2 = kbuf[r, pl.ds(j0, 2 * Q_BLK), :]
                for hh in range(2):
                    qm = jnp.where(low == (hh == 0), q, jnp.zeros_like(q))
                    scores.append(_dot_nt(qm, k2) + bias_sc[(c * 2 + hh) * 2 + seq_start])
            probs = []
            for s in scores:
                m = jnp.max(s, axis=-1, keepdims=True)
                e = jnp.exp2(s - m)
                probs.append((e.astype(BF16), m, jnp.sum(e, axis=-1, keepdims=True)))
            for u, (r, j, j0, seq_start) in enumerate(subs):
                v2 = vbuf[r, pl.ds(j0, 2 * Q_BLK), :]
                outs = [_dot(probs[2 * u + hh][0], v2) for hh in range(2)]
                stats = [[jnp.broadcast_to(probs[2 * u + hh][i], (Q_BLK, LANES)) for hh in range(2)] for i in (1, 2)]
                row0 = j * (Q_BLK * dil) + r
                dst = pl.ds(row0, Q_BLK, stride=dil) if dil > 1 else pl.ds(row0, Q_BLK)
                o_sc[c, dst, :] = jnp.where(low, outs[0], outs[1])
                m_sc[c, dst, :] = jnp.where(low, stats[0][0], stats[0][1])
                l_sc[c, dst, :] = jnp.where(low, stats[1][0], stats[1][1])
            return carry

        lax.fori_loop(0, dil * nsub // A_GROUP, group, 0)

    m = jnp.maximum(jnp.maximum(m_sc[0], m_sc[1]), m_sc[2])
    num = jnp.zeros((A_TILE, LANES), F32)
    den = jnp.zeros((A_TILE, LANES), F32)
    for c in range(ncfg):
        e = jnp.exp2(m_sc[c] - m)
        num = num + e * o_sc[c]
        den = den + e * l_sc[c]
    out_ref[...] = (num / den).astype(BF16)


def _dilated(qkv_by_cfg, *, bsz, s_len):
    nt = s_len // A_TILE
    npair = A_HEADS // 2
    ncfg = len(A_CONFIGS)
    in_specs, operands, kv_scratch = [], [], []
    for (window, dil), arr in zip(A_CONFIGS, qkv_by_cfg):
        rows = A_TILE // dil
        nsub = rows // Q_BLK
        cur = lambda off: (lambda b, h, t: (b, 0, t, off + h))
        halo = lambda off, nsub=nsub: (lambda b, h, t: (b, 0, jnp.maximum(t * nsub - 1, 0), off + h))
        in_specs += [pl.BlockSpec((None, dil, rows, LANES), cur(0)),
                     pl.BlockSpec((None, dil, rows, LANES), cur(npair)),
                     pl.BlockSpec((None, dil, rows, LANES), cur(2 * npair)),
                     pl.BlockSpec((None, dil, Q_BLK, LANES), halo(npair)),
                     pl.BlockSpec((None, dil, Q_BLK, LANES), halo(2 * npair))]
        operands += [arr] * 5
        kv_scratch.append(pltpu.VMEM((dil, Q_BLK + rows, LANES), BF16))
    return pl.pallas_call(
        _dilated_body,
        grid=(bsz, npair, nt),
        in_specs=in_specs,
        out_specs=pl.BlockSpec((A_TILE, LANES), lambda b, h, t: (b * nt + t, h)),
        out_shape=jax.ShapeDtypeStruct((bsz * s_len, A_HEADS * HEAD_DIM), BF16),
        scratch_shapes=[pltpu.VMEM((ncfg, A_TILE, LANES), F32)] * 3
                       + [pltpu.VMEM((ncfg * 4, Q_BLK, 2 * Q_BLK), F32)] + kv_scratch + kv_scratch,
        compiler_params=pltpu.CompilerParams(dimension_semantics=("arbitrary",) * 3,
                                             vmem_limit_bytes=VMEM_LIMIT),
        name="dilated",
    )(*operands)


def _q_aug(q_all, g, lane, low):
    rows = []
    for r in range(B_REP):
        h = g * B_REP + r
        blk = q_all[:, (h // 2) * LANES:(h // 2 + 1) * LANES]
        if h % 2 == 1:
            blk = pltpu.roll(blk, HEAD_DIM, axis=1)
        rows.append(jnp.where(low, blk, _query_alibi_row(2.0 ** (-8.0 * (h + 1) / B_HEADS), lane)))
    return jnp.concatenate(rows, axis=0).astype(BF16)


def _place_heads(per_head, low):
    placed = []
    for h, o in enumerate(per_head):
        placed.append(pltpu.roll(o, HEAD_DIM, axis=1) if h % 2 != h // B_REP else o)
    return [jnp.where(low, placed[2 * hp], placed[2 * hp + 1]) for hp in range(B_HEADS // 2)]


def _nsa_select_body(q_ref, cmp_ref, ng_ref, ovt_ref, gmap_ref, selb_ref, oc_ref, flags_ref, *, stretch, topk):
    ncw = min(cmp_ref.shape[1], (stretch + 1) * SEL_SPAN // CMP_STRIDE)
    nblk = min(LANES, (stretch + 1) * SEL_SPAN // SEL_LEN)
    q0 = stretch * SEL_SPAN + pl.program_id(1) * Q_BLK
    lane = _iota((1, LANES), 1)
    low = lane < HEAD_DIM
    t_col = q0 + _iota((Q_BLK, 1), 0)
    t_row = q0 + _iota((1, Q_BLK), 1)
    groups = range(B_KV_GROUPS)

    q_all = q_ref[...].astype(F32)
    qg = [_q_aug(q_all, g, lane, low) for g in groups]
    has_key = (t_col >= CMP_LEN - 1)[None]

    cmp_end = CMP_STRIDE * _iota((1, ncw), 1) + (CMP_LEN - 1)
    bias_c = jnp.where(cmp_end <= t_col, 0.0, NEG)
    blk_t = _iota((nblk, 1), 0)
    allowed_t = blk_t * SEL_LEN <= t_row
    cur_t = t_row // SEL_LEN
    forced_t = (blk_t == 0) | (blk_t == cur_t) | (blk_t == cur_t - 1)
    blk_f = blk_t.astype(F32)

    scores = [_dot_nt(qg[g], cmp_ref[g, :ncw, :]).reshape(B_REP, Q_BLK, ncw) + bias_c[None] for g in groups]
    probs = []
    for s in scores:
        e = jnp.exp2(s - jnp.max(s, axis=-1, keepdims=True))
        den = jnp.sum(e, axis=-1, keepdims=True)
        probs.append(e * jnp.where(has_key, 1.0 / den, 0.0))
    o_cmp = [_dot(probs[g].reshape(B_REP * Q_BLK, ncw).astype(BF16), cmp_ref[B_KV_GROUPS + g, :ncw, :])
             for g in groups]
    ranks = []
    for p in probs:
        psum = p[0] + p[1] + p[2] + p[3]
        p_hi = psum.astype(BF16)
        p_lo = (psum - p_hi.astype(F32)).astype(BF16)
        ov = ovt_ref[:nblk, :ncw]
        imp_t = _dot_nt(ov, p_hi) + _dot_nt(ov, p_lo)
        ranks.append(jnp.where(allowed_t, imp_t + jnp.where(forced_t, FORCE, 0.0), NEG))

    def pick_one(_, carry):
        out = []
        for rank, sel in carry:
            best = jnp.max(rank, axis=0, keepdims=True)
            cand = jnp.where(rank == best, blk_f, float(LANES))
            idx = jnp.min(cand, axis=0, keepdims=True)
            pick = blk_f == idx
            out.append((jnp.where(pick, PICKED, rank), jnp.where(pick, 1.0, sel)))
        return tuple(out)

    picked = lax.fori_loop(0, topk, pick_one, tuple((rank, jnp.zeros((nblk, Q_BLK), F32)) for rank in ranks))

    ones = jnp.ones((SUBLANES, Q_BLK), BF16)
    tile_cnt = jnp.zeros((SUBLANES, LANES), F32)
    for g in groups:
        sel_t = jnp.where(allowed_t, picked[g][1], 0.0)
        if nblk < LANES:
            sel_t = jnp.concatenate([sel_t, jnp.zeros((LANES - nblk, Q_BLK), F32)], axis=0)
        sel = sel_t.T
        selb_ref[:, g * LANES:(g + 1) * LANES] = jnp.where(sel > 0.5, 0.0, -MASK_BIG).astype(BF16)
        per_block = _dot(ones, sel.astype(BF16))
        tile_cnt = tile_cnt + _dot(per_block.astype(BF16), gmap_ref[g])
    flags_ref[...] = (tile_cnt > 0.5).astype(jnp.int32)

    sig = jax.nn.sigmoid(ng_ref[...])
    per_head = []
    for h in range(B_HEADS):
        g, r = divmod(h, B_REP)
        per_head.append(sig[:, 3 * h:3 * h + 1] * o_cmp[g][r * Q_BLK:(r + 1) * Q_BLK])
    for hp, tile in enumerate(_place_heads(per_head, low)):
        oc_ref[:, hp * LANES:(hp + 1) * LANES] = tile.astype(BF16)


def _nsa_attend_body(lists_ref, cnts_ref, q_ref, ks0_ref, ks1_ref, kw0_ref, kw1_ref, vst_ref, vwt_ref, *rest,
                     nq, ntile, nstretch):
    selb_refs, oc_refs = rest[:nstretch], rest[nstretch:2 * nstretch]
    ng_ref, oh_ref, out_ref = rest[2 * nstretch:]
    step = pl.program_id(0) * nq + pl.program_id(1)
    q0 = pl.program_id(1) * Q_BLK
    stretch = q0 // SEL_SPAN

    def from_select(refs, cols):
        val = refs[0][:, cols]
        for k in range(1, nstretch):
            val = jnp.where(stretch == k, refs[k][:, cols], val)
        return val

    lane = _iota((1, LANES), 1)
    low = lane < HEAD_DIM
    t_row = q0 + _iota((1, Q_BLK), 1)
    win_keys = WIN + Q_BLK
    rows = B_REP * Q_BLK
    groups = range(B_KV_GROUPS)
    ks_refs = (ks0_ref, ks1_ref)
    kw_refs = (kw0_ref, kw1_ref)

    q_all = q_ref[...].astype(F32)
    qg = [_q_aug(q_all, g, lane, low) for g in groups]

    def heads_on_rows(o_t):
        o_t = jnp.concatenate([o_t, o_t], axis=0)
        return jnp.concatenate([o_t[:, r * Q_BLK:(r + 1) * Q_BLK].T for r in range(B_REP)], axis=0)

    def softmax_step(s, m_i):
        m_new = jnp.maximum(m_i, jnp.max(s, axis=0, keepdims=True))
        return jnp.exp2(s - m_new).astype(BF16), jnp.exp2(m_i - m_new), m_new

    q_slc = []
    for g in groups:
        sel_bias = from_select(selb_refs, slice(g * LANES, (g + 1) * LANES))
        q_slc.append(jnp.concatenate([qg[g], jnp.concatenate([sel_bias] * B_REP, axis=0)], axis=1))

    def qk(g, kt):
        k0 = pl.multiple_of(jnp.minimum(kt, ntile - 1) * SLC_TILE, SLC_TILE)
        o0 = pl.multiple_of(kt * SLC_TILE, SLC_TILE)
        k_aug = jnp.concatenate([ks_refs[g][pl.ds(k0, SLC_TILE), :], oh_ref[pl.ds(o0, SLC_TILE), :]], axis=1)
        return _dot_nt(k_aug, q_slc[g])

    def pv(g, p, kt):
        return _dot(vst_ref[jnp.minimum(kt, ntile - 1), g], p)

    def trip(k, carry):
        work = [(g, lists_ref[(step * B_KV_GROUPS + g) * LIST_W + k * SLC_TRIP + u])
                for u in range(SLC_TRIP) for g in groups]
        scores = [qk(g, kt) for g, kt in work]
        state = list(carry)
        for (g, kt), s in zip(work, scores):
            m_i, acc = state[g]
            p, a, m_i = softmax_step(s, m_i)
            state[g] = (m_i, a * acc + pv(g, p, kt))
        return tuple(state)

    n_trips = jnp.maximum(cnts_ref[step * B_KV_GROUPS], cnts_ref[step * B_KV_GROUPS + 1]) // SLC_TRIP
    init = tuple((jnp.full((1, rows), PICKED, F32), jnp.zeros((VT_ROWS, rows), F32)) for g in groups)
    state = lax.fori_loop(0, n_trips, trip, init)

    kt_diag = q0 // SLC_TILE
    key_pos = kt_diag * SLC_TILE + _iota((SLC_TILE, 1), 0)
    causal = jnp.where(key_pos <= t_row, 0.0, NEG)
    causal = jnp.concatenate([causal] * B_REP, axis=1)
    scores = [qk(g, kt_diag) + causal for g in groups]
    o_slc = []
    for g in groups:
        m_i, acc = state[g]
        p, a, m_i = softmax_step(scores[g], m_i)
        acc = a * acc + pv(g, p, kt_diag)
        o_slc.append(acc[:HEAD_DIM] / acc[HEAD_DIM:HEAD_DIM + 1])

    kstart = pl.multiple_of(jnp.maximum(q0 - WIN, 0), Q_BLK)
    d_w = t_row - (kstart + _iota((win_keys, 1), 0))
    bias_w = jnp.where((d_w >= 0) & (d_w < WIN), 0.0, NEG)
    bias_w = jnp.concatenate([bias_w] * B_REP, axis=1)
    scores = [_dot_nt(kw_refs[g][pl.ds(kstart, win_keys), :], qg[g]) + bias_w for g in groups]
    o_win = []
    for g in groups:
        s = scores[g]
        p = jnp.exp2(s - jnp.max(s, axis=0, keepdims=True)).astype(BF16)
        v_t = jnp.concatenate([vwt_ref[kstart // Q_BLK + u, g] for u in range(win_keys // Q_BLK)], axis=1)
        acc = _dot(v_t, p)
        o_win.append(acc[:HEAD_DIM] / acc[HEAD_DIM:HEAD_DIM + 1])

    sig_t = jax.nn.sigmoid(ng_ref[...]).T
    per_head = []
    for g in groups:
        gated = []
        for r in range(B_REP):
            h = g * B_REP + r
            qs = slice(r * Q_BLK, (r + 1) * Q_BLK)
            gated.append(sig_t[3 * h + 1:3 * h + 2] * o_slc[g][:, qs] + sig_t[3 * h + 2:3 * h + 3] * o_win[g][:, qs])
        rows_g = heads_on_rows(jnp.concatenate(gated, axis=1))
        per_head += [rows_g[r * Q_BLK:(r + 1) * Q_BLK] for r in range(B_REP)]
    for hp in range(B_HEADS // 2):
        cols = slice(hp * LANES, (hp + 1) * LANES)
        tile = jnp.where(low, per_head[2 * hp], per_head[2 * hp + 1])
        out_ref[:, cols] = (from_select(oc_refs, cols).astype(F32) + tile).astype(BF16)


def _tile_lists(flags, *, bsz, s_len):
    nq = s_len // Q_BLK
    ntile = s_len // SLC_TILE
    f = flags[:, 0, :B_KV_GROUPS * FLAG_W].reshape(bsz * nq, B_KV_GROUPS, FLAG_W)[:, :, :ntile] > 0
    kt = jnp.arange(ntile, dtype=jnp.int32)
    diag = jnp.tile((jnp.arange(nq, dtype=jnp.int32) * Q_BLK) // SLC_TILE, bsz)[:, None, None]
    touched = f & (kt < diag)
    rank = jnp.cumsum(touched, axis=-1, dtype=jnp.int32) - 1
    n_touched = rank[..., -1] + 1
    cnt = SLC_TRIP * ((n_touched + SLC_TRIP - 1) // SLC_TRIP)
    pos = jnp.arange(LIST_W, dtype=jnp.int32)
    hit = touched[..., None, :] & (rank[..., None, :] == pos[:, None])
    order = jnp.sum(jnp.where(hit, kt, 0), axis=-1)
    lists = jnp.where(pos < n_touched[..., None], order, ntile)
    return lists.reshape(-1), cnt.reshape(-1)


def _nsa(qb, kvb, vst, vwt, cmp, ng, onehot, ovt, gmap, *, bsz, s_len):
    nq = s_len // Q_BLK
    ntile = s_len // SLC_TILE
    ncp = cmp.shape[2]
    topk = min(SEL_TOPK, s_len // SEL_LEN)
    hw = B_HEADS * HEAD_DIM
    params = pltpu.CompilerParams(dimension_semantics=("arbitrary", "arbitrary"), vmem_limit_bytes=VMEM_LIMIT)
    span_q = min(SEL_SPAN, s_len) // Q_BLK
    nstretch = nq // span_q
    selbs, ocs, flag_parts = [], [], []
    for k in range(nstretch):
        src = lambda b, i, k=k: (b * nq + k * span_q + i, 0)
        dst = lambda b, i: (b * span_q + i, 0)
        selb, oc, flags = pl.pallas_call(
            functools.partial(_nsa_select_body, stretch=k, topk=topk),
            grid=(bsz, span_q),
            in_specs=[pl.BlockSpec((Q_BLK, hw), src),
                      pl.BlockSpec((None, 2 * B_KV_GROUPS, ncp, LANES), lambda b, i: (b, 0, 0, 0)),
                      pl.BlockSpec((Q_BLK, LANES), src),
                      pl.BlockSpec(ovt.shape, lambda b, i: (0, 0)),
                      pl.BlockSpec(gmap.shape, lambda b, i: (0, 0, 0))],
            out_specs=[pl.BlockSpec((Q_BLK, B_KV_GROUPS * LANES), dst),
                       pl.BlockSpec((Q_BLK, hw), dst),
                       pl.BlockSpec((None, None, SUBLANES, LANES), lambda b, i: (b, i, 0, 0))],
            out_shape=[jax.ShapeDtypeStruct((bsz * span_q * Q_BLK, B_KV_GROUPS * LANES), BF16),
                       jax.ShapeDtypeStruct((bsz * span_q * Q_BLK, hw), BF16),
                       jax.ShapeDtypeStruct((bsz, span_q, SUBLANES, LANES), jnp.int32)],
            compiler_params=params,
            name=f"nsa_select{k}",
        )(qb, cmp, ng, ovt, gmap)
        selbs.append(selb)
        ocs.append(oc)
        flag_parts.append(flags)
    flags = jnp.concatenate(flag_parts, axis=1).reshape(bsz * nq, SUBLANES, LANES)

    lists, cnts = _tile_lists(flags, bsz=bsz, s_len=s_len)
    rowblk2 = lambda b, i, lists, cnts: (b * nq + i, 0)
    res = lambda col: (lambda b, i, lists, cnts: (b, col))
    part = lambda k: (lambda b, i, lists, cnts: (b * span_q + jnp.clip(i - k * span_q, 0, span_q - 1), 0))
    return pl.pallas_call(
        functools.partial(_nsa_attend_body, nq=nq, ntile=ntile, nstretch=nstretch),
        grid_spec=pltpu.PrefetchScalarGridSpec(
            num_scalar_prefetch=2,
            grid=(bsz, nq),
            in_specs=[pl.BlockSpec((Q_BLK, hw), rowblk2)]
                     + [pl.BlockSpec((s_len, LANES), res(col)) for col in range(2 * B_KV_GROUPS)]
                     + [pl.BlockSpec((None,) + v.shape[1:], lambda b, i, lists, cnts: (b, 0, 0, 0, 0))
                        for v in (vst, vwt)]
                     + [pl.BlockSpec((Q_BLK, B_KV_GROUPS * LANES), part(k)) for k in range(nstretch)]
                     + [pl.BlockSpec((Q_BLK, hw), part(k)) for k in range(nstretch)]
                     + [pl.BlockSpec((Q_BLK, LANES), rowblk2),
                        pl.BlockSpec(onehot.shape, lambda b, i, lists, cnts: (0, 0))],
            out_specs=pl.BlockSpec((Q_BLK, hw), rowblk2)),
        out_shape=jax.ShapeDtypeStruct((bsz * s_len, hw), BF16),
        compiler_params=params,
        name="nsa_attend",
    )(lists, cnts, qb, kvb, kvb, kvb, kvb, vst, vwt, *selbs, *ocs, ng, onehot)


def _post_body(x_ref, oa_ref, ob_ref, gab_ref, p_ref, wua_ref, wub_ref, wout_ref, g2_ref, w1_ref, w2_ref,
               g3_ref, wpg_ref, wple_ref, gf_ref, out_ref, *, d, ff_chunk):
    ya = _dot(oa_ref[...], wua_ref[...])
    yb = _dot(ob_ref[...], wub_ref[...])
    mixed = (jax.nn.sigmoid(gab_ref[:, :d].astype(F32)) * ya
             + jax.nn.sigmoid(gab_ref[:, d:].astype(F32)) * yb)
    h = x_ref[...] + _dot(mixed.astype(BF16), wout_ref[...])
    n2 = _rms(h, g2_ref[...]).astype(BF16)
    acc = h
    for c in range(w1_ref.shape[1] // ff_chunk):
        cs = slice(c * ff_chunk, (c + 1) * ff_chunk)
        hid = jnp.square(jnp.maximum(_dot(n2, w1_ref[:, cs]), 0.0))
        acc = acc + _dot(hid.astype(BF16), w2_ref[cs, :])
    n3 = _rms(acc, g3_ref[...]).astype(BF16)
    gate = jax.nn.sigmoid(_dot(n3, wpg_ref[...]))
    h3 = acc + gate * _dot(p_ref[...].astype(BF16), wple_ref[...])
    out_ref[...] = _rms(h3, gf_ref[...])


def _post(x2, oa, ob, gab, p2, wua, wub, wout, g2, w1, w2, g3, wpg, wple, gf, *, tm=512, ff_chunk=1024):
    t_len, d = x2.shape
    row = lambda i: (i, 0)
    const = lambda i: (0, 0)
    resident = lambda a: pl.BlockSpec(a.shape, const, pipeline_mode=pl.Buffered(1))
    acts = (x2, oa, ob, gab, p2)
    params = (wua, wub, wout, g2, w1, w2, g3, wpg, wple, gf)
    return pl.pallas_call(
        functools.partial(_post_body, d=d, ff_chunk=ff_chunk),
        grid=(t_len // tm,),
        in_specs=[pl.BlockSpec((tm, a.shape[1]), row) for a in acts] + [resident(w) for w in params],
        out_specs=pl.BlockSpec((tm, d), row),
        out_shape=jax.ShapeDtypeStruct((t_len, d), F32),
        compiler_params=pltpu.CompilerParams(dimension_semantics=("arbitrary",),
                                             vmem_limit_bytes=VMEM_LIMIT),
        name="post",
    )(*acts, *params)


def _selection_overlap_t(ncp, s_len):
    ncmp = (s_len - CMP_LEN) // CMP_STRIDE + 1
    nsel = s_len // SEL_LEN
    ratio = SEL_LEN // CMP_STRIDE
    span = CMP_LEN // CMP_STRIDE
    i = np.arange(ncmp)[:, None]
    j = np.arange(nsel)[None, :]
    ov = np.maximum(np.minimum(i + span, ratio * (j + 1)) - np.maximum(i, ratio * j), 0)
    out = np.zeros((LANES, ncp), np.float32)
    out[:nsel, :ncmp] = ov.T
    return out


def _layer(h, p_i, norm_mix_g, w_in, pe_ck, w_ck1, w_ck2, pe_cv, w_cv1, w_cv2, w_up_a, w_up_b, w_out,
           norm_mlp_g, w_mlp1, w_mlp2, norm_ple_g, w_ple_gate, w_ple, final_g):
    bsz, s_len, d = h.shape
    t_len = bsz * s_len
    aw = A_HEADS * HEAD_DIM
    bw = B_HEADS * HEAD_DIM
    kvw = B_KV_GROUPS * HEAD_DIM
    assert s_len % A_TILE == 0 and s_len % SEL_SPAN == 0 and kvw == LANES
    assert s_len // SEL_LEN <= LANES and s_len // SLC_TILE <= FLAG_W

    o_qb = 3 * aw
    o_kv = o_qb + bw
    o_ng = o_kv + 6 * kvw
    o_ga = o_ng + 3 * B_HEADS
    kv = lambda i: w_in[:, o_kv + i * kvw:o_kv + (i + 1) * kvw]
    zeros_h = jnp.zeros((d, HEAD_DIM), w_in.dtype)
    grp = lambda w, g: jnp.concatenate([w[:, g * HEAD_DIM:(g + 1) * HEAD_DIM], zeros_h], axis=1)
    wa = jnp.concatenate([w_in[:, :aw] * (SCALE * LOG2E), w_in[:, aw:3 * aw]], axis=1)
    wc = jnp.concatenate([kv(0), kv(1)], axis=1)
    wq = w_in[:, o_qb:o_qb + bw] * (SCALE * LOG2E)
    wkv = jnp.concatenate([grp(kv(2), 0), grp(kv(2), 1), grp(kv(4), 0), grp(kv(4), 1)], axis=1)
    wvt = jnp.concatenate([kv(3), kv(5)], axis=1).T
    wng = jnp.concatenate([w_in[:, o_ng:o_ga], jnp.zeros((d, LANES - 3 * B_HEADS), w_in.dtype)], axis=1)
    wgab = w_in[:, o_ga:]
    x2 = h.reshape(t_len, d)
    a0, a1, a2, kvc, qb, kvb, ng, gab, vst, vwt = _proj(x2, norm_mix_g.reshape(1, d), *(w.astype(BF16) for w in
                                                        (wa, wc, wq, wkv, wng, wgab, wvt)), s_len=s_len)

    ncp = s_len // CMP_STRIDE
    pe2 = jnp.stack([jnp.tile(pe, (1, B_KV_GROUPS)) for pe in (pe_ck, pe_cv)])

    def per_row_block_diag(w1):
        w = w1.reshape(CMP_LEN, HEAD_DIM, CMP_HIDDEN)
        z = jnp.zeros_like(w)
        return jnp.concatenate([jnp.concatenate([w, z], axis=2), jnp.concatenate([z, w], axis=2)], axis=1)

    w1bd = jnp.stack([per_row_block_diag(w_ck1), per_row_block_diag(w_cv1)]).astype(BF16)
    zpad = jnp.zeros((CMP_HIDDEN, HEAD_DIM), w_ck2.dtype)
    w2s = jnp.stack([jnp.concatenate([w_ck2, zpad], axis=1), jnp.concatenate([w_ck2, zpad], axis=1),
                     jnp.concatenate([w_cv2, zpad], axis=1), jnp.concatenate([zpad, w_cv2], axis=1)]).astype(BF16)
    cmp = _compress(kvc, pe2, w1bd, w2s, bsz=bsz, s_len=s_len)

    oa = _dilated((a0, a1, a2), bsz=bsz, s_len=s_len)
    onehot = (np.arange(s_len + SLC_TILE)[:, None] // SEL_LEN == np.arange(LANES)[None, :])
    onehot[s_len:] = True
    blocks_per_tile = SLC_TILE // SEL_LEN
    gmap = np.zeros((B_KV_GROUPS, LANES, LANES), np.float32)
    for g in range(B_KV_GROUPS):
        gmap[g, np.arange(LANES), FLAG_W * g + np.arange(LANES) // blocks_per_tile] = 1.0
    ob = _nsa(qb, kvb, vst, vwt, cmp, ng, jnp.asarray(onehot, BF16),
               jnp.asarray(_selection_overlap_t(ncp, s_len), BF16), jnp.asarray(gmap, BF16), bsz=bsz, s_len=s_len)

    b16 = lambda w: w.astype(BF16)
    row = lambda v: v.reshape(1, d)
    return _post(x2, oa, ob, gab, p_i.reshape(t_len, -1), b16(w_up_a), b16(w_up_b), b16(w_out), row(norm_mlp_g),
                 b16(w_mlp1), b16(w_mlp2), row(norm_ple_g), b16(w_ple_gate), b16(w_ple), row(final_g)
                 ).reshape(bsz, s_len, d)


def kernel(x, p, norm_mix_g, w_in, pe_ck, w_ck1, w_ck2, pe_cv, w_cv1, w_cv2, w_up_a, w_up_b, w_out,
           norm_mlp_g, w_mlp1, w_mlp2, norm_ple_g, w_ple_gate, w_ple, norm_final_g):
    depth = w_in.shape[0]
    assert depth == 1, "the fused tail applies the final norm inside the single layer"
    return _layer(x, p[0], norm_mix_g[0], w_in[0], pe_ck[0], w_ck1[0], w_ck2[0], pe_cv[0], w_cv1[0], w_cv2[0],
                  w_up_a[0], w_up_b[0], w_out[0], norm_mlp_g[0], w_mlp1[0], w_mlp2[0], norm_ple_g[0],
                  w_ple_gate[0], w_ple[0], norm_final_g)
```

```python
import functools

import numpy as np
import jax
import jax.numpy as jnp
from jax import lax
from jax.experimental import pallas as pl
from jax.experimental.pallas import tpu as pltpu

HEAD_DIM = 64
A_HEADS = 8
A_CONFIGS = ((128, 1), (512, 4), (2048, 16))
B_HEADS = 8
B_KV_GROUPS = 2
B_REP = B_HEADS // B_KV_GROUPS
CMP_LEN = 32
CMP_STRIDE = 16
CMP_HIDDEN = 256
SEL_LEN = 64
SEL_TOPK = 16
WIN = 512
Q_BLK = 128
EPS = 1e-6
NEG = -1e30
FORCE = 1e9
MASK_BIG = 2.0 ** 100
SCALE = HEAD_DIM ** -0.5

PICKED = -3e38
CMP_IDX_RADIX = 16

LANES = 128
SUBLANES = 8
BF16_SUBLANES = 16
A_TILE = 2048
A_GROUP = 8
SLC_TILE = 256
VT_ROWS = HEAD_DIM + BF16_SUBLANES
FLAG_W = 32
SEL_SPAN = 2048
SLC_TRIP = 3
LIST_W = FLAG_W + SLC_TRIP
VMEM_LIMIT = 56 * 1024 * 1024

ALIBI_TERMS = 3
LANE_POS = 64
LANE_CMP = LANE_POS + 2 * ALIBI_TERMS
LOG2E = 1.4426950408889634

F32 = jnp.float32
BF16 = jnp.bfloat16


def _dot(a, b):
    return jnp.dot(a, b, preferred_element_type=F32)


def _dot_nt(a, b):
    return lax.dot_general(a, b, (((1,), (1,)), ((), ())), preferred_element_type=F32)


def _rms(x, g):
    inv = lax.rsqrt(jnp.mean(x * x, axis=-1, keepdims=True) + EPS)
    return (x * inv) * g


def _iota(shape, dim, dtype=jnp.int32):
    return lax.broadcasted_iota(dtype, shape, dim)


def _pair_columns(rel_lane, even_val, odd_val):
    inside = (rel_lane >= 0) & (rel_lane < 2 * ALIBI_TERMS)
    return jnp.where(inside, jnp.where(rel_lane % 2 == 0, even_val, odd_val), 0.0)


def _bf16_pieces(x):
    pieces, rest = [], np.float64(x)
    for _ in range(ALIBI_TERMS):
        piece = np.float64(np.asarray(rest, np.float32).astype(jnp.bfloat16).astype(np.float32))
        pieces.append(float(piece))
        rest = rest - piece
    return pieces


def _query_alibi_row(slope, lane):
    row = jnp.zeros(lane.shape, F32)
    coeffs = ((LANE_POS, SEL_LEN * slope), (LANE_POS + 1, slope),
              (LANE_CMP, CMP_IDX_RADIX * CMP_STRIDE * slope), (LANE_CMP + 1, CMP_STRIDE * slope))
    for lane0, coeff in coeffs:
        for t, piece in enumerate(_bf16_pieces(coeff * LOG2E)):
            row = jnp.where(lane == lane0 + 2 * t, piece, row)
    return row


def _proj_body(x_ref, g_ref, wa_ref, wc_ref, wq_ref, wkv_ref, wng_ref, wgab_ref, wvt_ref,
               a0_ref, a1_ref, a2_ref, kvc_ref, qb_ref, kvb_ref, ng_ref, gab_ref, vst_ref, vwt_ref, res_sc, mid_sc,
               *, tm, s_len):
    n = _rms(x_ref[...], g_ref[...]).astype(BF16)
    vt = _dot_nt(wvt_ref[...], n)
    for branch, (out_ref, width) in enumerate(((vst_ref, SLC_TILE), (vwt_ref, Q_BLK))):
        for u in range(tm // width):
            for g in range(B_KV_GROUPS):
                r0 = (branch * B_KV_GROUPS + g) * HEAD_DIM
                out_ref[u, g, :HEAD_DIM, :] = vt[r0:r0 + HEAD_DIM, u * width:(u + 1) * width].astype(BF16)
                out_ref[u, g, HEAD_DIM:, :] = jnp.ones((VT_ROWS - HEAD_DIM, width), BF16)
    res = _dot(n, wa_ref[...])
    (_, d0), (_, d1), (_, d2) = A_CONFIGS
    step = d2 // d1
    for s in range(res.shape[1] // LANES):
        cols = slice(s * LANES, (s + 1) * LANES)
        res_sc[s] = res[:, cols]
        a0_ref[0, :, cols] = res[:, cols].astype(BF16)
        for r in range(d1):
            part = res_sc[s, pl.ds(r, tm // d1, stride=d1), :]
            a1_ref[r, :, cols] = part.astype(BF16)
            mid_sc[r] = part
            for r2 in range(step):
                a2_ref[r + d1 * r2, :, cols] = mid_sc[r, pl.ds(r2, tm // d2, stride=step), :].astype(BF16)
    kvc_ref[...] = _dot(n, wc_ref[...])
    qb_ref[...] = _dot(n, wq_ref[...]).astype(BF16)
    ng_ref[...] = _dot(n, wng_ref[...])
    gab_ref[...] = _dot(n, wgab_ref[...]).astype(BF16)
    pos = (pl.program_id(0) * tm) % s_len + _iota((tm, LANES), 0)
    lane = _iota((tm, LANES), 1)
    posc = _pair_columns(lane - LANE_POS, (pos // SEL_LEN).astype(F32), (pos % SEL_LEN).astype(F32))
    kv = _dot(n, wkv_ref[...])
    for c in range(kv.shape[1] // LANES):
        kvb_ref[:, c * LANES:(c + 1) * LANES] = (kv[:, c * LANES:(c + 1) * LANES] + posc).astype(BF16)


def _proj(x2, g, wa, wc, wq, wkv, wng, wgab, wvt, *, s_len, tm=2 * SLC_TILE):
    t_len, d = x2.shape
    bsz = t_len // s_len
    nrt = s_len // tm
    const = lambda i: (0, 0)
    row = lambda i: (i, 0)
    ws = (wa, wc, wq, wkv, wng, wgab, wvt)
    flat = (wc, wq, wkv, wng, wgab)
    flat_dtypes = (F32, BF16, BF16, F32, BF16)
    aw = wa.shape[1]
    a_specs = [pl.BlockSpec((None, dil, tm // dil, aw), lambda i: (i // nrt, 0, i % nrt, 0))
               for _, dil in A_CONFIGS]
    a_shapes = [jax.ShapeDtypeStruct((bsz, dil, s_len // dil, aw), BF16) for _, dil in A_CONFIGS]
    return pl.pallas_call(
        functools.partial(_proj_body, tm=tm, s_len=s_len),
        grid=(t_len // tm,),
        in_specs=[pl.BlockSpec((tm, d), row), pl.BlockSpec((1, d), const)]
                 + [pl.BlockSpec(w.shape, const, pipeline_mode=pl.Buffered(1)) for w in ws],
        out_specs=a_specs + [pl.BlockSpec((tm, w.shape[1]), row) for w in flat]
                  + [pl.BlockSpec((None, tm // width, B_KV_GROUPS, VT_ROWS, width),
                                  lambda i: (i // nrt, i % nrt, 0, 0, 0)) for width in (SLC_TILE, Q_BLK)],
        out_shape=a_shapes + [jax.ShapeDtypeStruct((t_len, w.shape[1]), dt) for w, dt in zip(flat, flat_dtypes)]
                  + [jax.ShapeDtypeStruct((bsz, s_len // width, B_KV_GROUPS, VT_ROWS, width), BF16)
                     for width in (SLC_TILE, Q_BLK)],
        scratch_shapes=[pltpu.VMEM((aw // LANES, tm, LANES), F32),
                        pltpu.VMEM((A_CONFIGS[1][1], tm // A_CONFIGS[1][1], LANES), F32)],
        compiler_params=pltpu.CompilerParams(dimension_semantics=("arbitrary",),
                                             vmem_limit_bytes=VMEM_LIMIT),
        name="proj",
    )(x2, g, *ws)


def _gelu_tanh(x):
    return 0.5 * x * (1.0 + jnp.tanh(np.sqrt(2.0 / np.pi).astype(np.float32) * (x + 0.044715 * (x * x * x))))


def _compress_body(x_ref, pe_ref, w1_ref, w2_ref, out_ref, *, ncp):
    first = jnp.zeros((ncp, B_KV_GROUPS * CMP_HIDDEN), F32)
    second = jnp.zeros((ncp, B_KV_GROUPS * CMP_HIDDEN), F32)
    for l in range(CMP_STRIDE):
        x_l = x_ref[pl.ds(l, ncp, stride=CMP_STRIDE), :]
        first = first + _dot((x_l + pe_ref[l:l + 1, :]).astype(BF16), w1_ref[l])
        second = second + _dot((x_l + pe_ref[CMP_STRIDE + l:CMP_STRIDE + l + 1, :]).astype(BF16),
                               w1_ref[CMP_STRIDE + l])
    pre = first + jnp.concatenate([second[1:], second[:1]], axis=0)
    hid = _gelu_tanh(pre).astype(BF16)
    is_key = pl.program_id(1) == 0
    n_idx = _iota((ncp, LANES), 0)
    lane = _iota((ncp, LANES), 1)
    nc = _pair_columns(lane - LANE_CMP, (n_idx // CMP_IDX_RADIX).astype(F32), (n_idx % CMP_IDX_RADIX).astype(F32))
    nc = jnp.where(is_key, nc, 0.0)
    for g in range(B_KV_GROUPS):
        out = _dot(hid[:, g * CMP_HIDDEN:(g + 1) * CMP_HIDDEN], w2_ref[g])
        out_ref[g] = (out + nc).astype(BF16)


def _compress(kvc, pe2, w1bd, w2s, *, bsz, s_len):
    ncp = s_len // CMP_STRIDE
    return pl.pallas_call(
        functools.partial(_compress_body, ncp=ncp),
        grid=(bsz, 2),
        in_specs=[pl.BlockSpec((s_len, LANES), lambda b, kv: (b, kv)),
                  pl.BlockSpec((None,) + pe2.shape[1:], lambda b, kv: (kv, 0, 0)),
                  pl.BlockSpec((None,) + w1bd.shape[1:], lambda b, kv: (kv, 0, 0, 0)),
                  pl.BlockSpec((B_KV_GROUPS, CMP_HIDDEN, LANES), lambda b, kv: (kv, 0, 0))],
        out_specs=pl.BlockSpec((None, B_KV_GROUPS, ncp, LANES), lambda b, kv: (b, kv, 0, 0)),
        out_shape=jax.ShapeDtypeStruct((bsz, 2 * B_KV_GROUPS, ncp, LANES), BF16),
        compiler_params=pltpu.CompilerParams(dimension_semantics=("arbitrary", "arbitrary"),
                                             vmem_limit_bytes=VMEM_LIMIT),
        name="compress",
    )(kvc, pe2, w1bd, w2s)


def _head_slope(h, n_heads):
    out = jnp.float32(2.0 ** (-8.0 * n_heads / n_heads))
    for k in range(n_heads - 1):
        out = jnp.where(h == k, jnp.float32(2.0 ** (-8.0 * (k + 1) / n_heads)), out)
    return out


def _dilated_body(*refs):
    ncfg = len(A_CONFIGS)
    in_refs = refs[:5 * ncfg]
    out_ref = refs[5 * ncfg]
    o_sc, l_sc, m_sc, bias_sc = refs[5 * ncfg + 1:5 * ncfg + 5]
    kbufs = refs[5 * ncfg + 5:5 * ncfg + 5 + ncfg]
    vbufs = refs[5 * ncfg + 5 + ncfg:]
    hp = pl.program_id(1)
    first_tile = pl.program_id(2) == 0
    low = _iota((1, LANES), 1) < HEAD_DIM

    @pl.when(first_tile)
    def _():
        qi = _iota((Q_BLK, 2 * Q_BLK), 0)
        kj = _iota((Q_BLK, 2 * Q_BLK), 1)
        dist = qi - kj + Q_BLK
        for c, (window, dil) in enumerate(A_CONFIGS):
            valid = (dist >= 0) & (dist <= window // dil)
            for hh in range(2):
                slope = _head_slope(2 * hp + hh, A_HEADS)
                bias = jnp.where(valid, -(slope * dil * LOG2E) * dist.astype(F32), NEG)
                bias_sc[(c * 2 + hh) * 2] = bias
                bias_sc[(c * 2 + hh) * 2 + 1] = jnp.where(kj < Q_BLK, NEG, bias)

    groups_by_cfg = []
    for c, (window, dil) in enumerate(A_CONFIGS):
        q_ref, k_ref, v_ref, kh_ref, vh_ref = in_refs[5 * c:5 * c + 5]
        kbuf, vbuf = kbufs[c], vbufs[c]
        rows = A_TILE // dil
        nsub = rows // Q_BLK
        kbuf[:, :Q_BLK, :] = kh_ref[...]
        kbuf[:, Q_BLK:, :] = k_ref[...]
        vbuf[:, :Q_BLK, :] = vh_ref[...]
        vbuf[:, Q_BLK:, :] = v_ref[...]

        def group(gidx, carry, c=c, dil=dil, nsub=nsub, q_ref=q_ref, kbuf=kbuf, vbuf=vbuf):
            subs = []
            for u in range(A_GROUP):
                idx = gidx * A_GROUP + u
                r = idx // nsub
                j = idx % nsub
                j0 = pl.multiple_of(j * Q_BLK, Q_BLK)
                seq_start = ((j == 0) & first_tile).astype(jnp.int32)
                subs.append((r, j, j0, seq_start))
            scores = []
            for r, j, j0, seq_start in subs:
                q = q_ref[r, pl.ds(j0, Q_BLK), :]
                k2 = kbuf[r, pl.ds(j0, 2 * Q_BLK), :]
                for hh in range(2):
                    qm = jnp.where(low == (hh == 0), q, jnp.zeros_like(q))
                    scores.append(_dot_nt(qm, k2) + bias_sc[(c * 2 + hh) * 2 + seq_start])
            probs = []
            for s in scores:
                m = jnp.max(s, axis=-1, keepdims=True)
                e = jnp.exp2(s - m)
                probs.append((e.astype(BF16), m, jnp.sum(e, axis=-1, keepdims=True)))
            for u, (r, j, j0, seq_start) in enumerate(subs):
                v2 = vbuf[r, pl.ds(j0, 2 * Q_BLK), :]
                outs = [_dot(probs[2 * u + hh][0], v2) for hh in range(2)]
                stats = [[jnp.broadcast_to(probs[2 * u + hh][i], (Q_BLK, LANES)) for hh in range(2)] for i in (1, 2)]
                row0 = j * (Q_BLK * dil) + r
                dst = pl.ds(row0, Q_BLK, stride=dil) if dil > 1 else pl.ds(row0, Q_BLK)
                o_sc[c, dst, :] = jnp.where(low, outs[0], outs[1])
                m_sc[c, dst, :] = jnp.where(low, stats[0][0], stats[0][1])
                l_sc[c, dst, :] = jnp.where(low, stats[1][0], stats[1][1])
            return carry

        groups_by_cfg.append(group)

    def trip(gidx, carry):
        for group in groups_by_cfg:
            carry = group(gidx, carry)
        return carry

    lax.fori_loop(0, A_TILE // Q_BLK // A_GROUP, trip, 0)

    m = jnp.maximum(jnp.maximum(m_sc[0], m_sc[1]), m_sc[2])
    num = jnp.zeros((A_TILE, LANES), F32)
    den = jnp.zeros((A_TILE, LANES), F32)
    for c in range(ncfg):
        e = jnp.exp2(m_sc[c] - m)
        num = num + e * o_sc[c]
        den = den + e * l_sc[c]
    out_ref[...] = (num / den).astype(BF16)


def _dilated(qkv_by_cfg, *, bsz, s_len):
    nt = s_len // A_TILE
    npair = A_HEADS // 2
    ncfg = len(A_CONFIGS)
    in_specs, operands, kv_scratch = [], [], []
    for (window, dil), arr in zip(A_CONFIGS, qkv_by_cfg):
        rows = A_TILE // dil
        nsub = rows // Q_BLK
        cur = lambda off: (lambda b, h, t: (b, 0, t, off + h))
        halo = lambda off, nsub=nsub: (lambda b, h, t: (b, 0, jnp.maximum(t * nsub - 1, 0), off + h))
        in_specs += [pl.BlockSpec((None, dil, rows, LANES), cur(0)),
                     pl.BlockSpec((None, dil, rows, LANES), cur(npair)),
                     pl.BlockSpec((None, dil, rows, LANES), cur(2 * npair)),
                     pl.BlockSpec((None, dil, Q_BLK, LANES), halo(npair)),
                     pl.BlockSpec((None, dil, Q_BLK, LANES), halo(2 * npair))]
        operands += [arr] * 5
        kv_scratch.append(pltpu.VMEM((dil, Q_BLK + rows, LANES), BF16))
    return pl.pallas_call(
        _dilated_body,
        grid=(bsz, npair, nt),
        in_specs=in_specs,
        out_specs=pl.BlockSpec((A_TILE, LANES), lambda b, h, t: (b * nt + t, h)),
        out_shape=jax.ShapeDtypeStruct((bsz * s_len, A_HEADS * HEAD_DIM), BF16),
        scratch_shapes=[pltpu.VMEM((ncfg, A_TILE, LANES), F32)] * 3
                       + [pltpu.VMEM((ncfg * 4, Q_BLK, 2 * Q_BLK), F32)] + kv_scratch + kv_scratch,
        compiler_params=pltpu.CompilerParams(dimension_semantics=("arbitrary",) * 3,
                                             vmem_limit_bytes=VMEM_LIMIT),
        name="dilated",
    )(*operands)


def _q_aug(q_all, g, lane, low):
    rows = []
    for r in range(B_REP):
        h = g * B_REP + r
        blk = q_all[:, (h // 2) * LANES:(h // 2 + 1) * LANES]
        if h % 2 == 1:
            blk = pltpu.roll(blk, HEAD_DIM, axis=1)
        rows.append(jnp.where(low, blk, _query_alibi_row(2.0 ** (-8.0 * (h + 1) / B_HEADS), lane)))
    return jnp.concatenate(rows, axis=0).astype(BF16)


def _place_heads(per_head, low):
    placed = []
    for h, o in enumerate(per_head):
        placed.append(pltpu.roll(o, HEAD_DIM, axis=1) if h % 2 != h // B_REP else o)
    return [jnp.where(low, placed[2 * hp], placed[2 * hp + 1]) for hp in range(B_HEADS // 2)]


def _nsa_select_body(q_ref, cmp_ref, ng_ref, ovt_ref, gmap_ref, selb_ref, oc_ref, flags_ref, *, stretch, topk):
    ncw = min(cmp_ref.shape[1], (stretch + 1) * SEL_SPAN // CMP_STRIDE)
    nblk = min(LANES, (stretch + 1) * SEL_SPAN // SEL_LEN)
    q0 = stretch * SEL_SPAN + pl.program_id(1) * Q_BLK
    lane = _iota((1, LANES), 1)
    low = lane < HEAD_DIM
    t_col = q0 + _iota((Q_BLK, 1), 0)
    t_row = q0 + _iota((1, Q_BLK), 1)
    groups = range(B_KV_GROUPS)

    q_all = q_ref[...].astype(F32)
    qg = [_q_aug(q_all, g, lane, low) for g in groups]
    has_key = (t_col >= CMP_LEN - 1)[None]

    cmp_end = CMP_STRIDE * _iota((1, ncw), 1) + (CMP_LEN - 1)
    bias_c = jnp.where(cmp_end <= t_col, 0.0, NEG)
    blk_t = _iota((nblk, 1), 0)
    allowed_t = blk_t * SEL_LEN <= t_row
    cur_t = t_row // SEL_LEN
    forced_t = (blk_t == 0) | (blk_t == cur_t) | (blk_t == cur_t - 1)
    blk_f = blk_t.astype(F32)

    scores = [_dot_nt(qg[g], cmp_ref[g, :ncw, :]).reshape(B_REP, Q_BLK, ncw) + bias_c[None] for g in groups]
    probs = []
    for s in scores:
        e = jnp.exp2(s - jnp.max(s, axis=-1, keepdims=True))
        den = jnp.sum(e, axis=-1, keepdims=True)
        probs.append(e * jnp.where(has_key, 1.0 / den, 0.0))
    o_cmp = [_dot(probs[g].reshape(B_REP * Q_BLK, ncw).astype(BF16), cmp_ref[B_KV_GROUPS + g, :ncw, :])
             for g in groups]
    ranks = []
    for p in probs:
        psum = p[0] + p[1] + p[2] + p[3]
        p_hi = psum.astype(BF16)
        p_lo = (psum - p_hi.astype(F32)).astype(BF16)
        ov = ovt_ref[:nblk, :ncw]
        imp_t = _dot_nt(ov, p_hi) + _dot_nt(ov, p_lo)
        ranks.append(jnp.where(allowed_t, imp_t + jnp.where(forced_t, FORCE, 0.0), NEG))

    def pick_one(_, carry):
        out = []
        for rank, sel in carry:
            best = jnp.max(rank, axis=0, keepdims=True)
            cand = jnp.where(rank == best, blk_f, float(LANES))
            idx = jnp.min(cand, axis=0, keepdims=True)
            pick = blk_f == idx
            out.append((jnp.where(pick, PICKED, rank), jnp.where(pick, 1.0, sel)))
        return tuple(out)

    picked = lax.fori_loop(0, topk, pick_one, tuple((rank, jnp.zeros((nblk, Q_BLK), F32)) for rank in ranks))

    ones = jnp.ones((SUBLANES, Q_BLK), BF16)
    tile_cnt = jnp.zeros((SUBLANES, LANES), F32)
    for g in groups:
        sel_t = jnp.where(allowed_t, picked[g][1], 0.0)
        if nblk < LANES:
            sel_t = jnp.concatenate([sel_t, jnp.zeros((LANES - nblk, Q_BLK), F32)], axis=0)
        sel = sel_t.T
        selb_ref[:, g * LANES:(g + 1) * LANES] = jnp.where(sel > 0.5, 0.0, -MASK_BIG).astype(BF16)
        per_block = _dot(ones, sel.astype(BF16))
        tile_cnt = tile_cnt + _dot(per_block.astype(BF16), gmap_ref[g])
    flags_ref[...] = (tile_cnt > 0.5).astype(jnp.int32)

    sig = jax.nn.sigmoid(ng_ref[...])
    per_head = []
    for h in range(B_HEADS):
        g, r = divmod(h, B_REP)
        per_head.append(sig[:, 3 * h:3 * h + 1] * o_cmp[g][r * Q_BLK:(r + 1) * Q_BLK])
    for hp, tile in enumerate(_place_heads(per_head, low)):
        oc_ref[:, hp * LANES:(hp + 1) * LANES] = tile.astype(BF16)


def _nsa_attend_body(lists_ref, cnts_ref, q_ref, ks0_ref, ks1_ref, kw0_ref, kw1_ref, vst_ref, vwt_ref, *rest,
                     nq, ntile, nstretch):
    selb_refs, oc_refs = rest[:nstretch], rest[nstretch:2 * nstretch]
    ng_ref, oh_ref, out_ref = rest[2 * nstretch:]
    step = pl.program_id(0) * nq + pl.program_id(1)
    q0 = pl.program_id(1) * Q_BLK
    stretch = q0 // SEL_SPAN

    def from_select(refs, cols):
        val = refs[0][:, cols]
        for k in range(1, nstretch):
            val = jnp.where(stretch == k, refs[k][:, cols], val)
        return val

    lane = _iota((1, LANES), 1)
    low = lane < HEAD_DIM
    t_row = q0 + _iota((1, Q_BLK), 1)
    win_keys = WIN + Q_BLK
    rows = B_REP * Q_BLK
    groups = range(B_KV_GROUPS)
    ks_refs = (ks0_ref, ks1_ref)
    kw_refs = (kw0_ref, kw1_ref)

    q_all = q_ref[...].astype(F32)
    qg = [_q_aug(q_all, g, lane, low) for g in groups]

    def heads_on_rows(o_t):
        o_t = jnp.concatenate([o_t, o_t], axis=0)
        return jnp.concatenate([o_t[:, r * Q_BLK:(r + 1) * Q_BLK].T for r in range(B_REP)], axis=0)

    def softmax_step(s, m_i):
        m_new = jnp.maximum(m_i, jnp.max(s, axis=0, keepdims=True))
        return jnp.exp2(s - m_new).astype(BF16), jnp.exp2(m_i - m_new), m_new

    q_slc = []
    for g in groups:
        sel_bias = from_select(selb_refs, slice(g * LANES, (g + 1) * LANES))
        q_slc.append(jnp.concatenate([qg[g], jnp.concatenate([sel_bias] * B_REP, axis=0)], axis=1))

    def qk(g, kt):
        k0 = pl.multiple_of(jnp.minimum(kt, ntile - 1) * SLC_TILE, SLC_TILE)
        o0 = pl.multiple_of(kt * SLC_TILE, SLC_TILE)
        k_aug = jnp.concatenate([ks_refs[g][pl.ds(k0, SLC_TILE), :], oh_ref[pl.ds(o0, SLC_TILE), :]], axis=1)
        return _dot_nt(k_aug, q_slc[g])

    def pv(g, p, kt):
        return _dot(vst_ref[jnp.minimum(kt, ntile - 1), g], p)

    def trip(k, carry):
        work = [(g, lists_ref[(step * B_KV_GROUPS + g) * LIST_W + k * SLC_TRIP + u])
                for u in range(SLC_TRIP) for g in groups]
        scores = [qk(g, kt) for g, kt in work]
        state = list(carry)
        for (g, kt), s in zip(work, scores):
            m_i, acc = state[g]
            p, a, m_i = softmax_step(s, m_i)
            state[g] = (m_i, a * acc + pv(g, p, kt))
        return tuple(state)

    n_trips = jnp.maximum(cnts_ref[step * B_KV_GROUPS], cnts_ref[step * B_KV_GROUPS + 1]) // SLC_TRIP
    init = tuple((jnp.full((1, rows), PICKED, F32), jnp.zeros((VT_ROWS, rows), F32)) for g in groups)
    state = lax.fori_loop(0, n_trips, trip, init)

    kt_diag = q0 // SLC_TILE
    key_pos = kt_diag * SLC_TILE + _iota((SLC_TILE, 1), 0)
    causal = jnp.where(key_pos <= t_row, 0.0, NEG)
    causal = jnp.concatenate([causal] * B_REP, axis=1)
    scores = [qk(g, kt_diag) + causal for g in groups]
    o_slc = []
    for g in groups:
        m_i, acc = state[g]
        p, a, m_i = softmax_step(scores[g], m_i)
        acc = a * acc + pv(g, p, kt_diag)
        o_slc.append(acc[:HEAD_DIM] / acc[HEAD_DIM:HEAD_DIM + 1])

    kstart = pl.multiple_of(jnp.maximum(q0 - WIN, 0), Q_BLK)
    d_w = t_row - (kstart + _iota((win_keys, 1), 0))
    bias_w = jnp.where((d_w >= 0) & (d_w < WIN), 0.0, NEG)
    bias_w = jnp.concatenate([bias_w] * B_REP, axis=1)
    scores = [_dot_nt(kw_refs[g][pl.ds(kstart, win_keys), :], qg[g]) + bias_w for g in groups]
    o_win = []
    for g in groups:
        s = scores[g]
        p = jnp.exp2(s - jnp.max(s, axis=0, keepdims=True)).astype(BF16)
        v_t = jnp.concatenate([vwt_ref[kstart // Q_BLK + u, g] for u in range(win_keys // Q_BLK)], axis=1)
        acc = _dot(v_t, p)
        o_win.append(acc[:HEAD_DIM] / acc[HEAD_DIM:HEAD_DIM + 1])

    sig_t = jax.nn.sigmoid(ng_ref[...]).T
    per_head = []
    for g in groups:
        gated = []
        for r in range(B_REP):
            h = g * B_REP + r
            qs = slice(r * Q_BLK, (r + 1) * Q_BLK)
            gated.append(sig_t[3 * h + 1:3 * h + 2] * o_slc[g][:, qs] + sig_t[3 * h + 2:3 * h + 3] * o_win[g][:, qs])
        rows_g = heads_on_rows(jnp.concatenate(gated, axis=1))
        per_head += [rows_g[r * Q_BLK:(r + 1) * Q_BLK] for r in range(B_REP)]
    for hp in range(B_HEADS // 2):
        cols = slice(hp * LANES, (hp + 1) * LANES)
        tile = jnp.where(low, per_head[2 * hp], per_head[2 * hp + 1])
        out_ref[:, cols] = (from_select(oc_refs, cols).astype(F32) + tile).astype(BF16)


def _tile_lists(flags, *, bsz, s_len):
    nq = s_len // Q_BLK
    ntile = s_len // SLC_TILE
    f = flags[:, 0, :B_KV_GROUPS * FLAG_W].reshape(bsz * nq, B_KV_GROUPS, FLAG_W)[:, :, :ntile] > 0
    kt = jnp.arange(ntile, dtype=jnp.int32)
    diag = jnp.tile((jnp.arange(nq, dtype=jnp.int32) * Q_BLK) // SLC_TILE, bsz)[:, None, None]
    touched = f & (kt < diag)
    rank = jnp.cumsum(touched, axis=-1, dtype=jnp.int32) - 1
    n_touched = rank[..., -1] + 1
    cnt = SLC_TRIP * ((n_touched + SLC_TRIP - 1) // SLC_TRIP)
    pos = jnp.arange(LIST_W, dtype=jnp.int32)
    hit = touched[..., None, :] & (rank[..., None, :] == pos[:, None])
    order = jnp.sum(jnp.where(hit, kt, 0), axis=-1)
    lists = jnp.where(pos < n_touched[..., None], order, ntile)
    return lists.reshape(-1), cnt.reshape(-1)


def _nsa(qb, kvb, vst, vwt, cmp, ng, onehot, ovt, gmap, *, bsz, s_len):
    nq = s_len // Q_BLK
    ntile = s_len // SLC_TILE
    ncp = cmp.shape[2]
    topk = min(SEL_TOPK, s_len // SEL_LEN)
    hw = B_HEADS * HEAD_DIM
    params = pltpu.CompilerParams(dimension_semantics=("arbitrary", "arbitrary"), vmem_limit_bytes=VMEM_LIMIT)
    span_q = min(SEL_SPAN, s_len) // Q_BLK
    nstretch = nq // span_q
    selbs, ocs, flag_parts = [], [], []
    for k in range(nstretch):
        src = lambda b, i, k=k: (b * nq + k * span_q + i, 0)
        dst = lambda b, i: (b * span_q + i, 0)
        selb, oc, flags = pl.pallas_call(
            functools.partial(_nsa_select_body, stretch=k, topk=topk),
            grid=(bsz, span_q),
            in_specs=[pl.BlockSpec((Q_BLK, hw), src),
                      pl.BlockSpec((None, 2 * B_KV_GROUPS, ncp, LANES), lambda b, i: (b, 0, 0, 0)),
                      pl.BlockSpec((Q_BLK, LANES), src),
                      pl.BlockSpec(ovt.shape, lambda b, i: (0, 0)),
                      pl.BlockSpec(gmap.shape, lambda b, i: (0, 0, 0))],
            out_specs=[pl.BlockSpec((Q_BLK, B_KV_GROUPS * LANES), dst),
                       pl.BlockSpec((Q_BLK, hw), dst),
                       pl.BlockSpec((None, None, SUBLANES, LANES), lambda b, i: (b, i, 0, 0))],
            out_shape=[jax.ShapeDtypeStruct((bsz * span_q * Q_BLK, B_KV_GROUPS * LANES), BF16),
                       jax.ShapeDtypeStruct((bsz * span_q * Q_BLK, hw), BF16),
                       jax.ShapeDtypeStruct((bsz, span_q, SUBLANES, LANES), jnp.int32)],
            compiler_params=params,
            name=f"nsa_select{k}",
        )(qb, cmp, ng, ovt, gmap)
        selbs.append(selb)
        ocs.append(oc)
        flag_parts.append(flags)
    flags = jnp.concatenate(flag_parts, axis=1).reshape(bsz * nq, SUBLANES, LANES)

    lists, cnts = _tile_lists(flags, bsz=bsz, s_len=s_len)
    rowblk2 = lambda b, i, lists, cnts: (b * nq + i, 0)
    res = lambda col: (lambda b, i, lists, cnts: (b, col))
    part = lambda k: (lambda b, i, lists, cnts: (b * span_q + jnp.clip(i - k * span_q, 0, span_q - 1), 0))
    return pl.pallas_call(
        functools.partial(_nsa_attend_body, nq=nq, ntile=ntile, nstretch=nstretch),
        grid_spec=pltpu.PrefetchScalarGridSpec(
            num_scalar_prefetch=2,
            grid=(bsz, nq),
            in_specs=[pl.BlockSpec((Q_BLK, hw), rowblk2)]
                     + [pl.BlockSpec((s_len, LANES), res(col)) for col in range(2 * B_KV_GROUPS)]
                     + [pl.BlockSpec((None,) + v.shape[1:], lambda b, i, lists, cnts: (b, 0, 0, 0, 0))
                        for v in (vst, vwt)]
                     + [pl.BlockSpec((Q_BLK, B_KV_GROUPS * LANES), part(k)) for k in range(nstretch)]
                     + [pl.BlockSpec((Q_BLK, hw), part(k)) for k in range(nstretch)]
                     + [pl.BlockSpec((Q_BLK, LANES), rowblk2),
                        pl.BlockSpec(onehot.shape, lambda b, i, lists, cnts: (0, 0))],
            out_specs=pl.BlockSpec((Q_BLK, hw), rowblk2)),
        out_shape=jax.ShapeDtypeStruct((bsz * s_len, hw), BF16),
        compiler_params=params,
        name="nsa_attend",
    )(lists, cnts, qb, kvb, kvb, kvb, kvb, vst, vwt, *selbs, *ocs, ng, onehot)


def _post_body(x_ref, oa_ref, ob_ref, gab_ref, p_ref, wua_ref, wub_ref, wout_ref, g2_ref, w1_ref, w2_ref,
               g3_ref, wpg_ref, wple_ref, gf_ref, out_ref, *, d, ff_chunk):
    ya = _dot(oa_ref[...], wua_ref[...])
    yb = _dot(ob_ref[...], wub_ref[...])
    mixed = (jax.nn.sigmoid(gab_ref[:, :d].astype(F32)) * ya
             + jax.nn.sigmoid(gab_ref[:, d:].astype(F32)) * yb)
    h = x_ref[...] + _dot(mixed.astype(BF16), wout_ref[...])
    n2 = _rms(h, g2_ref[...]).astype(BF16)
    acc = h
    for c in range(w1_ref.shape[1] // ff_chunk):
        cs = slice(c * ff_chunk, (c + 1) * ff_chunk)
        hid = jnp.square(jnp.maximum(_dot(n2, w1_ref[:, cs]), 0.0))
        acc = acc + _dot(hid.astype(BF16), w2_ref[cs, :])
    n3 = _rms(acc, g3_ref[...]).astype(BF16)
    gate = jax.nn.sigmoid(_dot(n3, wpg_ref[...]))
    h3 = acc + gate * _dot(p_ref[...].astype(BF16), wple_ref[...])
    out_ref[...] = _rms(h3, gf_ref[...])


def _post(x2, oa, ob, gab, p2, wua, wub, wout, g2, w1, w2, g3, wpg, wple, gf, *, tm=512, ff_chunk=1024):
    t_len, d = x2.shape
    row = lambda i: (i, 0)
    const = lambda i: (0, 0)
    resident = lambda a: pl.BlockSpec(a.shape, const, pipeline_mode=pl.Buffered(1))
    acts = (x2, oa, ob, gab, p2)
    params = (wua, wub, wout, g2, w1, w2, g3, wpg, wple, gf)
    return pl.pallas_call(
        functools.partial(_post_body, d=d, ff_chunk=ff_chunk),
        grid=(t_len // tm,),
        in_specs=[pl.BlockSpec((tm, a.shape[1]), row) for a in acts] + [resident(w) for w in params],
        out_specs=pl.BlockSpec((tm, d), row),
        out_shape=jax.ShapeDtypeStruct((t_len, d), F32),
        compiler_params=pltpu.CompilerParams(dimension_semantics=("arbitrary",),
                                             vmem_limit_bytes=VMEM_LIMIT),
        name="post",
    )(*acts, *params)


def _selection_overlap_t(ncp, s_len):
    ncmp = (s_len - CMP_LEN) // CMP_STRIDE + 1
    nsel = s_len // SEL_LEN
    ratio = SEL_LEN // CMP_STRIDE
    span = CMP_LEN // CMP_STRIDE
    i = np.arange(ncmp)[:, None]
    j = np.arange(nsel)[None, :]
    ov = np.maximum(np.minimum(i + span, ratio * (j + 1)) - np.maximum(i, ratio * j), 0)
    out = np.zeros((LANES, ncp), np.float32)
    out[:nsel, :ncmp] = ov.T
    return out


def _layer(h, p_i, norm_mix_g, w_in, pe_ck, w_ck1, w_ck2, pe_cv, w_cv1, w_cv2, w_up_a, w_up_b, w_out,
           norm_mlp_g, w_mlp1, w_mlp2, norm_ple_g, w_ple_gate, w_ple, final_g):
    bsz, s_len, d = h.shape
    t_len = bsz * s_len
    aw = A_HEADS * HEAD_DIM
    bw = B_HEADS * HEAD_DIM
    kvw = B_KV_GROUPS * HEAD_DIM
    assert s_len % A_TILE == 0 and s_len % SEL_SPAN == 0 and kvw == LANES
    assert s_len // SEL_LEN <= LANES and s_len // SLC_TILE <= FLAG_W

    o_qb = 3 * aw
    o_kv = o_qb + bw
    o_ng = o_kv + 6 * kvw
    o_ga = o_ng + 3 * B_HEADS
    kv = lambda i: w_in[:, o_kv + i * kvw:o_kv + (i + 1) * kvw]
    zeros_h = jnp.zeros((d, HEAD_DIM), w_in.dtype)
    grp = lambda w, g: jnp.concatenate([w[:, g * HEAD_DIM:(g + 1) * HEAD_DIM], zeros_h], axis=1)
    wa = jnp.concatenate([w_in[:, :aw] * (SCALE * LOG2E), w_in[:, aw:3 * aw]], axis=1)
    wc = jnp.concatenate([kv(0), kv(1)], axis=1)
    wq = w_in[:, o_qb:o_qb + bw] * (SCALE * LOG2E)
    wkv = jnp.concatenate([grp(kv(2), 0), grp(kv(2), 1), grp(kv(4), 0), grp(kv(4), 1)], axis=1)
    wvt = jnp.concatenate([kv(3), kv(5)], axis=1).T
    wng = jnp.concatenate([w_in[:, o_ng:o_ga], jnp.zeros((d, LANES - 3 * B_HEADS), w_in.dtype)], axis=1)
    wgab = w_in[:, o_ga:]
    x2 = h.reshape(t_len, d)
    a0, a1, a2, kvc, qb, kvb, ng, gab, vst, vwt = _proj(x2, norm_mix_g.reshape(1, d), *(w.astype(BF16) for w in
                                                        (wa, wc, wq, wkv, wng, wgab, wvt)), s_len=s_len)

    ncp = s_len // CMP_STRIDE
    pe2 = jnp.stack([jnp.tile(pe, (1, B_KV_GROUPS)) for pe in (pe_ck, pe_cv)])

    def per_row_block_diag(w1):
        w = w1.reshape(CMP_LEN, HEAD_DIM, CMP_HIDDEN)
        z = jnp.zeros_like(w)
        return jnp.concatenate([jnp.concatenate([w, z], axis=2), jnp.concatenate([z, w], axis=2)], axis=1)

    w1bd = jnp.stack([per_row_block_diag(w_ck1), per_row_block_diag(w_cv1)]).astype(BF16)
    zpad = jnp.zeros((CMP_HIDDEN, HEAD_DIM), w_ck2.dtype)
    w2s = jnp.stack([jnp.concatenate([w_ck2, zpad], axis=1), jnp.concatenate([w_ck2, zpad], axis=1),
                     jnp.concatenate([w_cv2, zpad], axis=1), jnp.concatenate([zpad, w_cv2], axis=1)]).astype(BF16)
    cmp = _compress(kvc, pe2, w1bd, w2s, bsz=bsz, s_len=s_len)

    oa = _dilated((a0, a1, a2), bsz=bsz, s_len=s_len)
    onehot = (np.arange(s_len + SLC_TILE)[:, None] // SEL_LEN == np.arange(LANES)[None, :])
    onehot[s_len:] = True
    blocks_per_tile = SLC_TILE // SEL_LEN
    gmap = np.zeros((B_KV_GROUPS, LANES, LANES), np.float32)
    for g in range(B_KV_GROUPS):
        gmap[g, np.arange(LANES), FLAG_W * g + np.arange(LANES) // blocks_per_tile] = 1.0
    ob = _nsa(qb, kvb, vst, vwt, cmp, ng, jnp.asarray(onehot, BF16),
               jnp.asarray(_selection_overlap_t(ncp, s_len), BF16), jnp.asarray(gmap, BF16), bsz=bsz, s_len=s_len)

    b16 = lambda w: w.astype(BF16)
    row = lambda v: v.reshape(1, d)
    return _post(x2, oa, ob, gab, p_i.reshape(t_len, -1), b16(w_up_a), b16(w_up_b), b16(w_out), row(norm_mlp_g),
                 b16(w_mlp1), b16(w_mlp2), row(norm_ple_g), b16(w_ple_gate), b16(w_ple), row(final_g)
                 ).reshape(bsz, s_len, d)


def kernel(x, p, norm_mix_g, w_in, pe_ck, w_ck1, w_ck2, pe_cv, w_cv1, w_cv2, w_up_a, w_up_b, w_out,
           norm_mlp_g, w_mlp1, w_mlp2, norm_ple_g, w_ple_gate, w_ple, norm_final_g):
    depth = w_in.shape[0]
    assert depth == 1, "the fused tail applies the final norm inside the single layer"
    return _layer(x, p[0], norm_mix_g[0], w_in[0], pe_ck[0], w_ck1[0], w_ck2[0], pe_cv[0], w_cv1[0], w_cv2[0],
                  w_up_a[0], w_up_b[0], w_out[0], norm_mlp_g[0], w_mlp1[0], w_mlp2[0], norm_ple_g[0],
                  w_ple_gate[0], w_ple[0], norm_final_g)
```

```python
import functools

import numpy as np
import jax
import jax.numpy as jnp
from jax import lax
from jax.experimental import pallas as pl
from jax.experimental.pallas import tpu as pltpu

HEAD_DIM = 64
A_HEADS = 8
A_CONFIGS = ((128, 1), (512, 4), (2048, 16))
B_HEADS = 8
B_KV_GROUPS = 2
B_REP = B_HEADS // B_KV_GROUPS
CMP_LEN = 32
CMP_STRIDE = 16
CMP_HIDDEN = 256
SEL_LEN = 64
SEL_TOPK = 16
WIN = 512
Q_BLK = 128
EPS = 1e-6
NEG = -1e30
FORCE = 1e9
MASK_BIG = 2.0 ** 100
SCALE = HEAD_DIM ** -0.5

PICKED = -3e38
CMP_IDX_RADIX = 16

LANES = 128
SUBLANES = 8
BF16_SUBLANES = 16
A_TILE = 2048
A_GROUP = 8
SLC_TILE = 256
VT_ROWS = HEAD_DIM + BF16_SUBLANES
FLAG_W = 32
SEL_SPAN = 2048
SLC_TRIP = 3
LIST_W = FLAG_W + SLC_TRIP
VMEM_LIMIT = 56 * 1024 * 1024

ALIBI_TERMS = 3
LANE_POS = 64
LANE_CMP = LANE_POS + 2 * ALIBI_TERMS
LOG2E = 1.4426950408889634

F32 = jnp.float32
BF16 = jnp.bfloat16


def _dot(a, b):
    return jnp.dot(a, b, preferred_element_type=F32)


def _dot_nt(a, b):
    return lax.dot_general(a, b, (((1,), (1,)), ((), ())), preferred_element_type=F32)


def _rms(x, g):
    inv = lax.rsqrt(jnp.mean(x * x, axis=-1, keepdims=True) + EPS)
    return (x * inv) * g


def _iota(shape, dim, dtype=jnp.int32):
    return lax.broadcasted_iota(dtype, shape, dim)


def _pair_columns(rel_lane, even_val, odd_val):
    inside = (rel_lane >= 0) & (rel_lane < 2 * ALIBI_TERMS)
    return jnp.where(inside, jnp.where(rel_lane % 2 == 0, even_val, odd_val), 0.0)


def _bf16_pieces(x):
    pieces, rest = [], np.float64(x)
    for _ in range(ALIBI_TERMS):
        piece = np.float64(np.asarray(rest, np.float32).astype(jnp.bfloat16).astype(np.float32))
        pieces.append(float(piece))
        rest = rest - piece
    return pieces


def _query_alibi_row(slope, lane):
    row = jnp.zeros(lane.shape, F32)
    coeffs = ((LANE_POS, SEL_LEN * slope), (LANE_POS + 1, slope),
              (LANE_CMP, CMP_IDX_RADIX * CMP_STRIDE * slope), (LANE_CMP + 1, CMP_STRIDE * slope))
    for lane0, coeff in coeffs:
        for t, piece in enumerate(_bf16_pieces(coeff * LOG2E)):
            row = jnp.where(lane == lane0 + 2 * t, piece, row)
    return row


def _proj_body(x_ref, g_ref, wa_ref, wc_ref, wq_ref, wkv_ref, wng_ref, wgab_ref, wvt_ref,
               a0_ref, a1_ref, a2_ref, kvc_ref, qb_ref, kvb_ref, ng_ref, gab_ref, vst_ref, vwt_ref, res_sc, mid_sc,
               *, tm, s_len):
    n = _rms(x_ref[...], g_ref[...]).astype(BF16)
    vt = _dot_nt(wvt_ref[...], n)
    for branch, (out_ref, width) in enumerate(((vst_ref, SLC_TILE), (vwt_ref, Q_BLK))):
        for u in range(tm // width):
            for g in range(B_KV_GROUPS):
                r0 = (branch * B_KV_GROUPS + g) * HEAD_DIM
                out_ref[u, g, :HEAD_DIM, :] = vt[r0:r0 + HEAD_DIM, u * width:(u + 1) * width].astype(BF16)
                out_ref[u, g, HEAD_DIM:, :] = jnp.ones((VT_ROWS - HEAD_DIM, width), BF16)
    res = _dot(n, wa_ref[...])
    (_, d0), (_, d1), (_, d2) = A_CONFIGS
    step = d2 // d1
    for s in range(res.shape[1] // LANES):
        cols = slice(s * LANES, (s + 1) * LANES)
        res_sc[s] = res[:, cols]
        a0_ref[0, :, cols] = res[:, cols].astype(BF16)
        for r in range(d1):
            part = res_sc[s, pl.ds(r, tm // d1, stride=d1), :]
            a1_ref[r, :, cols] = part.astype(BF16)
            mid_sc[r] = part
            for r2 in range(step):
                a2_ref[r + d1 * r2, :, cols] = mid_sc[r, pl.ds(r2, tm // d2, stride=step), :].astype(BF16)
    kvc_ref[...] = _dot(n, wc_ref[...])
    qb_ref[...] = _dot(n, wq_ref[...]).astype(BF16)
    ng_ref[...] = _dot(n, wng_ref[...])
    gab_ref[...] = _dot(n, wgab_ref[...]).astype(BF16)
    pos = (pl.program_id(0) * tm) % s_len + _iota((tm, LANES), 0)
    lane = _iota((tm, LANES), 1)
    posc = _pair_columns(lane - LANE_POS, (pos // SEL_LEN).astype(F32), (pos % SEL_LEN).astype(F32))
    kv = _dot(n, wkv_ref[...])
    for c in range(kv.shape[1] // LANES):
        kvb_ref[:, c * LANES:(c + 1) * LANES] = (kv[:, c * LANES:(c + 1) * LANES] + posc).astype(BF16)


def _proj(x2, g, wa, wc, wq, wkv, wng, wgab, wvt, *, s_len, tm=2 * SLC_TILE):
    t_len, d = x2.shape
    bsz = t_len // s_len
    nrt = s_len // tm
    const = lambda i: (0, 0)
    row = lambda i: (i, 0)
    ws = (wa, wc, wq, wkv, wng, wgab, wvt)
    flat = (wc, wq, wkv, wng, wgab)
    flat_dtypes = (F32, BF16, BF16, F32, BF16)
    aw = wa.shape[1]
    a_specs = [pl.BlockSpec((None, dil, tm // dil, aw), lambda i: (i // nrt, 0, i % nrt, 0))
               for _, dil in A_CONFIGS]
    a_shapes = [jax.ShapeDtypeStruct((bsz, dil, s_len // dil, aw), BF16) for _, dil in A_CONFIGS]
    return pl.pallas_call(
        functools.partial(_proj_body, tm=tm, s_len=s_len),
        grid=(t_len // tm,),
        in_specs=[pl.BlockSpec((tm, d), row), pl.BlockSpec((1, d), const)]
                 + [pl.BlockSpec(w.shape, const, pipeline_mode=pl.Buffered(1)) for w in ws],
        out_specs=a_specs + [pl.BlockSpec((tm, w.shape[1]), row) for w in flat]
                  + [pl.BlockSpec((None, tm // width, B_KV_GROUPS, VT_ROWS, width),
                                  lambda i: (i // nrt, i % nrt, 0, 0, 0)) for width in (SLC_TILE, Q_BLK)],
        out_shape=a_shapes + [jax.ShapeDtypeStruct((t_len, w.shape[1]), dt) for w, dt in zip(flat, flat_dtypes)]
                  + [jax.ShapeDtypeStruct((bsz, s_len // width, B_KV_GROUPS, VT_ROWS, width), BF16)
                     for width in (SLC_TILE, Q_BLK)],
        scratch_shapes=[pltpu.VMEM((aw // LANES, tm, LANES), F32),
                        pltpu.VMEM((A_CONFIGS[1][1], tm // A_CONFIGS[1][1], LANES), F32)],
        compiler_params=pltpu.CompilerParams(dimension_semantics=("arbitrary",),
                                             vmem_limit_bytes=VMEM_LIMIT),
        name="proj",
    )(x2, g, *ws)


def _gelu_tanh(x):
    return 0.5 * x * (1.0 + jnp.tanh(np.sqrt(2.0 / np.pi).astype(np.float32) * (x + 0.044715 * (x * x * x))))


def _compress_body(x_ref, pe_ref, w1_ref, w2_ref, out_ref, *, ncp):
    first = jnp.zeros((ncp, B_KV_GROUPS * CMP_HIDDEN), F32)
    second = jnp.zeros((ncp, B_KV_GROUPS * CMP_HIDDEN), F32)
    for l in range(CMP_STRIDE):
        x_l = x_ref[pl.ds(l, ncp, stride=CMP_STRIDE), :]
        first = first + _dot((x_l + pe_ref[l:l + 1, :]).astype(BF16), w1_ref[l])
        second = second + _dot((x_l + pe_ref[CMP_STRIDE + l:CMP_STRIDE + l + 1, :]).astype(BF16),
                               w1_ref[CMP_STRIDE + l])
    pre = first + jnp.concatenate([second[1:], second[:1]], axis=0)
    hid = _gelu_tanh(pre).astype(BF16)
    is_key = pl.program_id(1) == 0
    n_idx = _iota((ncp, LANES), 0)
    lane = _iota((ncp, LANES), 1)
    nc = _pair_columns(lane - LANE_CMP, (n_idx // CMP_IDX_RADIX).astype(F32), (n_idx % CMP_IDX_RADIX).astype(F32))
    nc = jnp.where(is_key, nc, 0.0)
    for g in range(B_KV_GROUPS):
        out = _dot(hid[:, g * CMP_HIDDEN:(g + 1) * CMP_HIDDEN], w2_ref[g])
        out_ref[g] = (out + nc).astype(BF16)


def _compress(kvc, pe2, w1bd, w2s, *, bsz, s_len):
    ncp = s_len // CMP_STRIDE
    return pl.pallas_call(
        functools.partial(_compress_body, ncp=ncp),
        grid=(bsz, 2),
        in_specs=[pl.BlockSpec((s_len, LANES), lambda b, kv: (b, kv)),
                  pl.BlockSpec((None,) + pe2.shape[1:], lambda b, kv: (kv, 0, 0)),
                  pl.BlockSpec((None,) + w1bd.shape[1:], lambda b, kv: (kv, 0, 0, 0)),
                  pl.BlockSpec((B_KV_GROUPS, CMP_HIDDEN, LANES), lambda b, kv: (kv, 0, 0))],
        out_specs=pl.BlockSpec((None, B_KV_GROUPS, ncp, LANES), lambda b, kv: (b, kv, 0, 0)),
        out_shape=jax.ShapeDtypeStruct((bsz, 2 * B_KV_GROUPS, ncp, LANES), BF16),
        compiler_params=pltpu.CompilerParams(dimension_semantics=("arbitrary", "arbitrary"),
                                             vmem_limit_bytes=VMEM_LIMIT),
        name="compress",
    )(kvc, pe2, w1bd, w2s)


def _head_slope(h, n_heads):
    out = jnp.float32(2.0 ** (-8.0 * n_heads / n_heads))
    for k in range(n_heads - 1):
        out = jnp.where(h == k, jnp.float32(2.0 ** (-8.0 * (k + 1) / n_heads)), out)
    return out


def _dilated_body(*refs):
    ncfg = len(A_CONFIGS)
    in_refs = refs[:5 * ncfg]
    out_ref = refs[5 * ncfg]
    o_sc, l_sc, m_sc, bias_sc = refs[5 * ncfg + 1:5 * ncfg + 5]
    kbufs = refs[5 * ncfg + 5:5 * ncfg + 5 + ncfg]
    vbufs = refs[5 * ncfg + 5 + ncfg:]
    hp = pl.program_id(1)
    first_tile = pl.program_id(2) == 0
    low = _iota((1, LANES), 1) < HEAD_DIM

    @pl.when(first_tile)
    def _():
        qi = _iota((Q_BLK, 2 * Q_BLK), 0)
        kj = _iota((Q_BLK, 2 * Q_BLK), 1)
        dist = qi - kj + Q_BLK
        for c, (window, dil) in enumerate(A_CONFIGS):
            valid = (dist >= 0) & (dist <= window // dil)
            for hh in range(2):
                slope = _head_slope(2 * hp + hh, A_HEADS)
                bias = jnp.where(valid, -(slope * dil * LOG2E) * dist.astype(F32), NEG)
                bias_sc[(c * 2 + hh) * 2] = bias
                bias_sc[(c * 2 + hh) * 2 + 1] = jnp.where(kj < Q_BLK, NEG, bias)

    groups_by_cfg = []
    for c, (window, dil) in enumerate(A_CONFIGS):
        q_ref, k_ref, v_ref, kh_ref, vh_ref = in_refs[5 * c:5 * c + 5]
        kbuf, vbuf = kbufs[c], vbufs[c]
        rows = A_TILE // dil
        nsub = rows // Q_BLK
        kbuf[:, :Q_BLK, :] = kh_ref[...]
        kbuf[:, Q_BLK:, :] = k_ref[...]
        vbuf[:, :Q_BLK, :] = vh_ref[...]
        vbuf[:, Q_BLK:, :] = v_ref[...]

        def group(gidx, carry, c=c, dil=dil, nsub=nsub, q_ref=q_ref, kbuf=kbuf, vbuf=vbuf):
            subs = []
            for u in range(A_GROUP):
                idx = gidx * A_GROUP + u
                r = idx // nsub
                j = idx % nsub
                j0 = pl.multiple_of(j * Q_BLK, Q_BLK)
                seq_start = ((j == 0) & first_tile).astype(jnp.int32)
                subs.append((r, j, j0, seq_start))
            scores = []
            for r, j, j0, seq_start in subs:
                q = q_ref[r, pl.ds(j0, Q_BLK), :]
                k2 = kbuf[r, pl.ds(j0, 2 * Q_BLK), :]
                for hh in range(2):
                    qm = jnp.where(low == (hh == 0), q, jnp.zeros_like(q))
                    scores.append(_dot_nt(qm, k2) + bias_sc[(c * 2 + hh) * 2 + seq_start])
            probs = []
            for s in scores:
                m = jnp.max(s, axis=-1, keepdims=True)
                e = jnp.exp2(s - m)
                probs.append((e.astype(BF16), m, jnp.sum(e, axis=-1, keepdims=True)))
            for u, (r, j, j0, seq_start) in enumerate(subs):
                v2 = vbuf[r, pl.ds(j0, 2 * Q_BLK), :]
                outs = [_dot(probs[2 * u + hh][0], v2) for hh in range(2)]
                stats = [[jnp.broadcast_to(probs[2 * u + hh][i], (Q_BLK, LANES)) for hh in range(2)] for i in (1, 2)]
                row0 = j * (Q_BLK * dil) + r
                dst = pl.ds(row0, Q_BLK, stride=dil) if dil > 1 else pl.ds(row0, Q_BLK)
                o_sc[c, dst, :] = jnp.where(low, outs[0], outs[1])
                m_sc[c, dst, :] = jnp.where(low, stats[0][0], stats[0][1])
                l_sc[c, dst, :] = jnp.where(low, stats[1][0], stats[1][1])
            return carry

        groups_by_cfg.append(group)

    def trip(gidx, carry):
        for group in groups_by_cfg:
            carry = group(gidx, carry)
        return carry

    lax.fori_loop(0, A_TILE // Q_BLK // A_GROUP, trip, 0, unroll=True)

    m = jnp.maximum(jnp.maximum(m_sc[0], m_sc[1]), m_sc[2])
    num = jnp.zeros((A_TILE, LANES), F32)
    den = jnp.zeros((A_TILE, LANES), F32)
    for c in range(ncfg):
        e = jnp.exp2(m_sc[c] - m)
        num = num + e * o_sc[c]
        den = den + e * l_sc[c]
    out_ref[...] = (num / den).astype(BF16)


def _dilated(qkv_by_cfg, *, bsz, s_len):
    nt = s_len // A_TILE
    npair = A_HEADS // 2
    ncfg = len(A_CONFIGS)
    in_specs, operands, kv_scratch = [], [], []
    for (window, dil), arr in zip(A_CONFIGS, qkv_by_cfg):
        rows = A_TILE // dil
        nsub = rows // Q_BLK
        cur = lambda off: (lambda b, h, t: (b, 0, t, off + h))
        halo = lambda off, nsub=nsub: (lambda b, h, t: (b, 0, jnp.maximum(t * nsub - 1, 0), off + h))
        in_specs += [pl.BlockSpec((None, dil, rows, LANES), cur(0)),
                     pl.BlockSpec((None, dil, rows, LANES), cur(npair)),
                     pl.BlockSpec((None, dil, rows, LANES), cur(2 * npair)),
                     pl.BlockSpec((None, dil, Q_BLK, LANES), halo(npair)),
                     pl.BlockSpec((None, dil, Q_BLK, LANES), halo(2 * npair))]
        operands += [arr] * 5
        kv_scratch.append(pltpu.VMEM((dil, Q_BLK + rows, LANES), BF16))
    return pl.pallas_call(
        _dilated_body,
        grid=(bsz, npair, nt),
        in_specs=in_specs,
        out_specs=pl.BlockSpec((A_TILE, LANES), lambda b, h, t: (b * nt + t, h)),
        out_shape=jax.ShapeDtypeStruct((bsz * s_len, A_HEADS * HEAD_DIM), BF16),
        scratch_shapes=[pltpu.VMEM((ncfg, A_TILE, LANES), F32)] * 3
                       + [pltpu.VMEM((ncfg * 4, Q_BLK, 2 * Q_BLK), F32)] + kv_scratch + kv_scratch,
        compiler_params=pltpu.CompilerParams(dimension_semantics=("arbitrary",) * 3,
                                             vmem_limit_bytes=VMEM_LIMIT),
        name="dilated",
    )(*operands)


def _q_aug(q_all, g, lane, low):
    rows = []
    for r in range(B_REP):
        h = g * B_REP + r
        blk = q_all[:, (h // 2) * LANES:(h // 2 + 1) * LANES]
        if h % 2 == 1:
            blk = pltpu.roll(blk, HEAD_DIM, axis=1)
        rows.append(jnp.where(low, blk, _query_alibi_row(2.0 ** (-8.0 * (h + 1) / B_HEADS), lane)))
    return jnp.concatenate(rows, axis=0).astype(BF16)


def _place_heads(per_head, low):
    placed = []
    for h, o in enumerate(per_head):
        placed.append(pltpu.roll(o, HEAD_DIM, axis=1) if h % 2 != h // B_REP else o)
    return [jnp.where(low, placed[2 * hp], placed[2 * hp + 1]) for hp in range(B_HEADS // 2)]


def _nsa_select_body(q_ref, cmp_ref, ng_ref, ovt_ref, gmap_ref, selb_ref, oc_ref, flags_ref, *, stretch, topk):
    ncw = min(cmp_ref.shape[1], (stretch + 1) * SEL_SPAN // CMP_STRIDE)
    nblk = min(LANES, (stretch + 1) * SEL_SPAN // SEL_LEN)
    q0 = stretch * SEL_SPAN + pl.program_id(1) * Q_BLK
    lane = _iota((1, LANES), 1)
    low = lane < HEAD_DIM
    t_col = q0 + _iota((Q_BLK, 1), 0)
    t_row = q0 + _iota((1, Q_BLK), 1)
    groups = range(B_KV_GROUPS)

    q_all = q_ref[...].astype(F32)
    qg = [_q_aug(q_all, g, lane, low) for g in groups]
    has_key = (t_col >= CMP_LEN - 1)[None]

    cmp_end = CMP_STRIDE * _iota((1, ncw), 1) + (CMP_LEN - 1)
    bias_c = jnp.where(cmp_end <= t_col, 0.0, NEG)
    blk_t = _iota((nblk, 1), 0)
    allowed_t = blk_t * SEL_LEN <= t_row
    cur_t = t_row // SEL_LEN
    forced_t = (blk_t == 0) | (blk_t == cur_t) | (blk_t == cur_t - 1)
    blk_f = blk_t.astype(F32)

    scores = [_dot_nt(qg[g], cmp_ref[g, :ncw, :]).reshape(B_REP, Q_BLK, ncw) + bias_c[None] for g in groups]
    probs = []
    for s in scores:
        e = jnp.exp2(s - jnp.max(s, axis=-1, keepdims=True))
        den = jnp.sum(e, axis=-1, keepdims=True)
        probs.append(e * jnp.where(has_key, 1.0 / den, 0.0))
    o_cmp = [_dot(probs[g].reshape(B_REP * Q_BLK, ncw).astype(BF16), cmp_ref[B_KV_GROUPS + g, :ncw, :])
             for g in groups]
    ranks = []
    for p in probs:
        psum = p[0] + p[1] + p[2] + p[3]
        p_hi = psum.astype(BF16)
        p_lo = (psum - p_hi.astype(F32)).astype(BF16)
        ov = ovt_ref[:nblk, :ncw]
        imp_t = _dot_nt(ov, p_hi) + _dot_nt(ov, p_lo)
        ranks.append(jnp.where(allowed_t, imp_t + jnp.where(forced_t, FORCE, 0.0), NEG))

    def pick_one(_, carry):
        out = []
        for rank, sel in carry:
            best = jnp.max(rank, axis=0, keepdims=True)
            cand = jnp.where(rank == best, blk_f, float(LANES))
            idx = jnp.min(cand, axis=0, keepdims=True)
            pick = blk_f == idx
            out.append((jnp.where(pick, PICKED, rank), jnp.where(pick, 1.0, sel)))
        return tuple(out)

    picked = lax.fori_loop(0, topk, pick_one, tuple((rank, jnp.zeros((nblk, Q_BLK), F32)) for rank in ranks))

    ones = jnp.ones((SUBLANES, Q_BLK), BF16)
    tile_cnt = jnp.zeros((SUBLANES, LANES), F32)
    for g in groups:
        sel_t = jnp.where(allowed_t, picked[g][1], 0.0)
        if nblk < LANES:
            sel_t = jnp.concatenate([sel_t, jnp.zeros((LANES - nblk, Q_BLK), F32)], axis=0)
        sel = sel_t.T
        selb_ref[:, g * LANES:(g + 1) * LANES] = jnp.where(sel > 0.5, 0.0, -MASK_BIG).astype(BF16)
        per_block = _dot(ones, sel.astype(BF16))
        tile_cnt = tile_cnt + _dot(per_block.astype(BF16), gmap_ref[g])
    flags_ref[...] = (tile_cnt > 0.5).astype(jnp.int32)

    sig = jax.nn.sigmoid(ng_ref[...])
    per_head = []
    for h in range(B_HEADS):
        g, r = divmod(h, B_REP)
        per_head.append(sig[:, 3 * h:3 * h + 1] * o_cmp[g][r * Q_BLK:(r + 1) * Q_BLK])
    for hp, tile in enumerate(_place_heads(per_head, low)):
        oc_ref[:, hp * LANES:(hp + 1) * LANES] = tile.astype(BF16)


def _nsa_attend_body(lists_ref, cnts_ref, q_ref, ks0_ref, ks1_ref, kw0_ref, kw1_ref, vst_ref, vwt_ref, *rest,
                     nq, ntile, nstretch):
    selb_refs, oc_refs = rest[:nstretch], rest[nstretch:2 * nstretch]
    ng_ref, oh_ref, out_ref = rest[2 * nstretch:]
    step = pl.program_id(0) * nq + pl.program_id(1)
    q0 = pl.program_id(1) * Q_BLK
    stretch = q0 // SEL_SPAN

    def from_select(refs, cols):
        val = refs[0][:, cols]
        for k in range(1, nstretch):
            val = jnp.where(stretch == k, refs[k][:, cols], val)
        return val

    lane = _iota((1, LANES), 1)
    low = lane < HEAD_DIM
    t_row = q0 + _iota((1, Q_BLK), 1)
    win_keys = WIN + Q_BLK
    rows = B_REP * Q_BLK
    groups = range(B_KV_GROUPS)
    ks_refs = (ks0_ref, ks1_ref)
    kw_refs = (kw0_ref, kw1_ref)

    q_all = q_ref[...].astype(F32)
    qg = [_q_aug(q_all, g, lane, low) for g in groups]

    def heads_on_rows(o_t):
        o_t = jnp.concatenate([o_t, o_t], axis=0)
        return jnp.concatenate([o_t[:, r * Q_BLK:(r + 1) * Q_BLK].T for r in range(B_REP)], axis=0)

    def softmax_step(s, m_i):
        m_new = jnp.maximum(m_i, jnp.max(s, axis=0, keepdims=True))
        return jnp.exp2(s - m_new).astype(BF16), jnp.exp2(m_i - m_new), m_new

    q_slc = []
    for g in groups:
        sel_bias = from_select(selb_refs, slice(g * LANES, (g + 1) * LANES))
        q_slc.append(jnp.concatenate([qg[g], jnp.concatenate([sel_bias] * B_REP, axis=0)], axis=1))

    def qk(g, kt):
        k0 = pl.multiple_of(jnp.minimum(kt, ntile - 1) * SLC_TILE, SLC_TILE)
        o0 = pl.multiple_of(kt * SLC_TILE, SLC_TILE)
        k_aug = jnp.concatenate([ks_refs[g][pl.ds(k0, SLC_TILE), :], oh_ref[pl.ds(o0, SLC_TILE), :]], axis=1)
        return _dot_nt(k_aug, q_slc[g])

    def pv(g, p, kt):
        return _dot(vst_ref[jnp.minimum(kt, ntile - 1), g], p)

    def trip(k, carry):
        work = [(g, lists_ref[(step * B_KV_GROUPS + g) * LIST_W + k * SLC_TRIP + u])
                for u in range(SLC_TRIP) for g in groups]
        scores = [qk(g, kt) for g, kt in work]
        state = list(carry)
        for (g, kt), s in zip(work, scores):
            m_i, acc = state[g]
            p, a, m_i = softmax_step(s, m_i)
            state[g] = (m_i, a * acc + pv(g, p, kt))
        return tuple(state)

    n_trips = jnp.maximum(cnts_ref[step * B_KV_GROUPS], cnts_ref[step * B_KV_GROUPS + 1]) // SLC_TRIP
    init = tuple((jnp.full((1, rows), PICKED, F32), jnp.zeros((VT_ROWS, rows), F32)) for g in groups)
    state = lax.fori_loop(0, n_trips, trip, init)

    kt_diag = q0 // SLC_TILE
    key_pos = kt_diag * SLC_TILE + _iota((SLC_TILE, 1), 0)
    causal = jnp.where(key_pos <= t_row, 0.0, NEG)
    causal = jnp.concatenate([causal] * B_REP, axis=1)
    scores = [qk(g, kt_diag) + causal for g in groups]
    o_slc = []
    for g in groups:
        m_i, acc = state[g]
        p, a, m_i = softmax_step(scores[g], m_i)
        acc = a * acc + pv(g, p, kt_diag)
        o_slc.append(acc[:HEAD_DIM] / acc[HEAD_DIM:HEAD_DIM + 1])

    kstart = pl.multiple_of(jnp.maximum(q0 - WIN, 0), Q_BLK)
    d_w = t_row - (kstart + _iota((win_keys, 1), 0))
    bias_w = jnp.where((d_w >= 0) & (d_w < WIN), 0.0, NEG)
    bias_w = jnp.concatenate([bias_w] * B_REP, axis=1)
    scores = [_dot_nt(kw_refs[g][pl.ds(kstart, win_keys), :], qg[g]) + bias_w for g in groups]
    o_win = []
    for g in groups:
        s = scores[g]
        p = jnp.exp2(s - jnp.max(s, axis=0, keepdims=True)).astype(BF16)
        v_t = jnp.concatenate([vwt_ref[kstart // Q_BLK + u, g] for u in range(win_keys // Q_BLK)], axis=1)
        acc = _dot(v_t, p)
        o_win.append(acc[:HEAD_DIM] / acc[HEAD_DIM:HEAD_DIM + 1])

    sig_t = jax.nn.sigmoid(ng_ref[...]).T
    per_head = []
    for g in groups:
        gated = []
        for r in range(B_REP):
            h = g * B_REP + r
            qs = slice(r * Q_BLK, (r + 1) * Q_BLK)
            gated.append(sig_t[3 * h + 1:3 * h + 2] * o_slc[g][:, qs] + sig_t[3 * h + 2:3 * h + 3] * o_win[g][:, qs])
        rows_g = heads_on_rows(jnp.concatenate(gated, axis=1))
        per_head += [rows_g[r * Q_BLK:(r + 1) * Q_BLK] for r in range(B_REP)]
    for hp in range(B_HEADS // 2):
        cols = slice(hp * LANES, (hp + 1) * LANES)
        tile = jnp.where(low, per_head[2 * hp], per_head[2 * hp + 1])
        out_ref[:, cols] = (from_select(oc_refs, cols).astype(F32) + tile).astype(BF16)


def _tile_lists(flags, *, bsz, s_len):
    nq = s_len // Q_BLK
    ntile = s_len // SLC_TILE
    f = flags[:, 0, :B_KV_GROUPS * FLAG_W].reshape(bsz * nq, B_KV_GROUPS, FLAG_W)[:, :, :ntile] > 0
    kt = jnp.arange(ntile, dtype=jnp.int32)
    diag = jnp.tile((jnp.arange(nq, dtype=jnp.int32) * Q_BLK) // SLC_TILE, bsz)[:, None, None]
    touched = f & (kt < diag)
    rank = jnp.cumsum(touched, axis=-1, dtype=jnp.int32) - 1
    n_touched = rank[..., -1] + 1
    cnt = SLC_TRIP * ((n_touched + SLC_TRIP - 1) // SLC_TRIP)
    pos = jnp.arange(LIST_W, dtype=jnp.int32)
    hit = touched[..., None, :] & (rank[..., None, :] == pos[:, None])
    order = jnp.sum(jnp.where(hit, kt, 0), axis=-1)
    lists = jnp.where(pos < n_touched[..., None], order, ntile)
    return lists.reshape(-1), cnt.reshape(-1)


def _nsa(qb, kvb, vst, vwt, cmp, ng, onehot, ovt, gmap, *, bsz, s_len):
    nq = s_len // Q_BLK
    ntile = s_len // SLC_TILE
    ncp = cmp.shape[2]
    topk = min(SEL_TOPK, s_len // SEL_LEN)
    hw = B_HEADS * HEAD_DIM
    params = pltpu.CompilerParams(dimension_semantics=("arbitrary", "arbitrary"), vmem_limit_bytes=VMEM_LIMIT)
    span_q = min(SEL_SPAN, s_len) // Q_BLK
    nstretch = nq // span_q
    selbs, ocs, flag_parts = [], [], []
    for k in range(nstretch):
        src = lambda b, i, k=k: (b * nq + k * span_q + i, 0)
        dst = lambda b, i: (b * span_q + i, 0)
        selb, oc, flags = pl.pallas_call(
            functools.partial(_nsa_select_body, stretch=k, topk=topk),
            grid=(bsz, span_q),
            in_specs=[pl.BlockSpec((Q_BLK, hw), src),
                      pl.BlockSpec((None, 2 * B_KV_GROUPS, ncp, LANES), lambda b, i: (b, 0, 0, 0)),
                      pl.BlockSpec((Q_BLK, LANES), src),
                      pl.BlockSpec(ovt.shape, lambda b, i: (0, 0)),
                      pl.BlockSpec(gmap.shape, lambda b, i: (0, 0, 0))],
            out_specs=[pl.BlockSpec((Q_BLK, B_KV_GROUPS * LANES), dst),
                       pl.BlockSpec((Q_BLK, hw), dst),
                       pl.BlockSpec((None, None, SUBLANES, LANES), lambda b, i: (b, i, 0, 0))],
            out_shape=[jax.ShapeDtypeStruct((bsz * span_q * Q_BLK, B_KV_GROUPS * LANES), BF16),
                       jax.ShapeDtypeStruct((bsz * span_q * Q_BLK, hw), BF16),
                       jax.ShapeDtypeStruct((bsz, span_q, SUBLANES, LANES), jnp.int32)],
            compiler_params=params,
            name=f"nsa_select{k}",
        )(qb, cmp, ng, ovt, gmap)
        selbs.append(selb)
        ocs.append(oc)
        flag_parts.append(flags)
    flags = jnp.concatenate(flag_parts, axis=1).reshape(bsz * nq, SUBLANES, LANES)

    lists, cnts = _tile_lists(flags, bsz=bsz, s_len=s_len)
    rowblk2 = lambda b, i, lists, cnts: (b * nq + i, 0)
    res = lambda col: (lambda b, i, lists, cnts: (b, col))
    part = lambda k: (lambda b, i, lists, cnts: (b * span_q + jnp.clip(i - k * span_q, 0, span_q - 1), 0))
    return pl.pallas_call(
        functools.partial(_nsa_attend_body, nq=nq, ntile=ntile, nstretch=nstretch),
        grid_spec=pltpu.PrefetchScalarGridSpec(
            num_scalar_prefetch=2,
            grid=(bsz, nq),
            in_specs=[pl.BlockSpec((Q_BLK, hw), rowblk2)]
                     + [pl.BlockSpec((s_len, LANES), res(col)) for col in range(2 * B_KV_GROUPS)]
                     + [pl.BlockSpec((None,) + v.shape[1:], lambda b, i, lists, cnts: (b, 0, 0, 0, 0))
                        for v in (vst, vwt)]
                     + [pl.BlockSpec((Q_BLK, B_KV_GROUPS * LANES), part(k)) for k in range(nstretch)]
                     + [pl.BlockSpec((Q_BLK, hw), part(k)) for k in range(nstretch)]
                     + [pl.BlockSpec((Q_BLK, LANES), rowblk2),
                        pl.BlockSpec(onehot.shape, lambda b, i, lists, cnts: (0, 0))],
            out_specs=pl.BlockSpec((Q_BLK, hw), rowblk2)),
        out_shape=jax.ShapeDtypeStruct((bsz * s_len, hw), BF16),
        compiler_params=params,
        name="nsa_attend",
    )(lists, cnts, qb, kvb, kvb, kvb, kvb, vst, vwt, *selbs, *ocs, ng, onehot)


def _post_body(x_ref, oa_ref, ob_ref, gab_ref, p_ref, wua_ref, wub_ref, wout_ref, g2_ref, w1_ref, w2_ref,
               g3_ref, wpg_ref, wple_ref, gf_ref, out_ref, *, d, ff_chunk):
    ya = _dot(oa_ref[...], wua_ref[...])
    yb = _dot(ob_ref[...], wub_ref[...])
    mixed = (jax.nn.sigmoid(gab_ref[:, :d].astype(F32)) * ya
             + jax.nn.sigmoid(gab_ref[:, d:].astype(F32)) * yb)
    h = x_ref[...] + _dot(mixed.astype(BF16), wout_ref[...])
    n2 = _rms(h, g2_ref[...]).astype(BF16)
    acc = h
    for c in range(w1_ref.shape[1] // ff_chunk):
        cs = slice(c * ff_chunk, (c + 1) * ff_chunk)
        hid = jnp.square(jnp.maximum(_dot(n2, w1_ref[:, cs]), 0.0))
        acc = acc + _dot(hid.astype(BF16), w2_ref[cs, :])
    n3 = _rms(acc, g3_ref[...]).astype(BF16)
    gate = jax.nn.sigmoid(_dot(n3, wpg_ref[...]))
    h3 = acc + gate * _dot(p_ref[...].astype(BF16), wple_ref[...])
    out_ref[...] = _rms(h3, gf_ref[...])


def _post(x2, oa, ob, gab, p2, wua, wub, wout, g2, w1, w2, g3, wpg, wple, gf, *, tm=512, ff_chunk=1024):
    t_len, d = x2.shape
    row = lambda i: (i, 0)
    const = lambda i: (0, 0)
    resident = lambda a: pl.BlockSpec(a.shape, const, pipeline_mode=pl.Buffered(1))
    acts = (x2, oa, ob, gab, p2)
    params = (wua, wub, wout, g2, w1, w2, g3, wpg, wple, gf)
    return pl.pallas_call(
        functools.partial(_post_body, d=d, ff_chunk=ff_chunk),
        grid=(t_len // tm,),
        in_specs=[pl.BlockSpec((tm, a.shape[1]), row) for a in acts] + [resident(w) for w in params],
        out_specs=pl.BlockSpec((tm, d), row),
        out_shape=jax.ShapeDtypeStruct((t_len, d), F32),
        compiler_params=pltpu.CompilerParams(dimension_semantics=("arbitrary",),
                                             vmem_limit_bytes=VMEM_LIMIT),
        name="post",
    )(*acts, *params)


def _selection_overlap_t(ncp, s_len):
    ncmp = (s_len - CMP_LEN) // CMP_STRIDE + 1
    nsel = s_len // SEL_LEN
    ratio = SEL_LEN // CMP_STRIDE
    span = CMP_LEN // CMP_STRIDE
    i = np.arange(ncmp)[:, None]
    j = np.arange(nsel)[None, :]
    ov = np.maximum(np.minimum(i + span, ratio * (j + 1)) - np.maximum(i, ratio * j), 0)
    out = np.zeros((LANES, ncp), np.float32)
    out[:nsel, :ncmp] = ov.T
    return out


def _layer(h, p_i, norm_mix_g, w_in, pe_ck, w_ck1, w_ck2, pe_cv, w_cv1, w_cv2, w_up_a, w_up_b, w_out,
           norm_mlp_g, w_mlp1, w_mlp2, norm_ple_g, w_ple_gate, w_ple, final_g):
    bsz, s_len, d = h.shape
    t_len = bsz * s_len
    aw = A_HEADS * HEAD_DIM
    bw = B_HEADS * HEAD_DIM
    kvw = B_KV_GROUPS * HEAD_DIM
    assert s_len % A_TILE == 0 and s_len % SEL_SPAN == 0 and kvw == LANES
    assert s_len // SEL_LEN <= LANES and s_len // SLC_TILE <= FLAG_W

    o_qb = 3 * aw
    o_kv = o_qb + bw
    o_ng = o_kv + 6 * kvw
    o_ga = o_ng + 3 * B_HEADS
    kv = lambda i: w_in[:, o_kv + i * kvw:o_kv + (i + 1) * kvw]
    zeros_h = jnp.zeros((d, HEAD_DIM), w_in.dtype)
    grp = lambda w, g: jnp.concatenate([w[:, g * HEAD_DIM:(g + 1) * HEAD_DIM], zeros_h], axis=1)
    wa = jnp.concatenate([w_in[:, :aw] * (SCALE * LOG2E), w_in[:, aw:3 * aw]], axis=1)
    wc = jnp.concatenate([kv(0), kv(1)], axis=1)
    wq = w_in[:, o_qb:o_qb + bw] * (SCALE * LOG2E)
    wkv = jnp.concatenate([grp(kv(2), 0), grp(kv(2), 1), grp(kv(4), 0), grp(kv(4), 1)], axis=1)
    wvt = jnp.concatenate([kv(3), kv(5)], axis=1).T
    wng = jnp.concatenate([w_in[:, o_ng:o_ga], jnp.zeros((d, LANES - 3 * B_HEADS), w_in.dtype)], axis=1)
    wgab = w_in[:, o_ga:]
    x2 = h.reshape(t_len, d)
    a0, a1, a2, kvc, qb, kvb, ng, gab, vst, vwt = _proj(x2, norm_mix_g.reshape(1, d), *(w.astype(BF16) for w in
                                                        (wa, wc, wq, wkv, wng, wgab, wvt)), s_len=s_len)

    ncp = s_len // CMP_STRIDE
    pe2 = jnp.stack([jnp.tile(pe, (1, B_KV_GROUPS)) for pe in (pe_ck, pe_cv)])

    def per_row_block_diag(w1):
        w = w1.reshape(CMP_LEN, HEAD_DIM, CMP_HIDDEN)
        z = jnp.zeros_like(w)
        return jnp.concatenate([jnp.concatenate([w, z], axis=2), jnp.concatenate([z, w], axis=2)], axis=1)

    w1bd = jnp.stack([per_row_block_diag(w_ck1), per_row_block_diag(w_cv1)]).astype(BF16)
    zpad = jnp.zeros((CMP_HIDDEN, HEAD_DIM), w_ck2.dtype)
    w2s = jnp.stack([jnp.concatenate([w_ck2, zpad], axis=1), jnp.concatenate([w_ck2, zpad], axis=1),
                     jnp.concatenate([w_cv2, zpad], axis=1), jnp.concatenate([zpad, w_cv2], axis=1)]).astype(BF16)
    cmp = _compress(kvc, pe2, w1bd, w2s, bsz=bsz, s_len=s_len)

    oa = _dilated((a0, a1, a2), bsz=bsz, s_len=s_len)
    onehot = (np.arange(s_len + SLC_TILE)[:, None] // SEL_LEN == np.arange(LANES)[None, :])
    onehot[s_len:] = True
    blocks_per_tile = SLC_TILE // SEL_LEN
    gmap = np.zeros((B_KV_GROUPS, LANES, LANES), np.float32)
    for g in range(B_KV_GROUPS):
        gmap[g, np.arange(LANES), FLAG_W * g + np.arange(LANES) // blocks_per_tile] = 1.0
    ob = _nsa(qb, kvb, vst, vwt, cmp, ng, jnp.asarray(onehot, BF16),
               jnp.asarray(_selection_overlap_t(ncp, s_len), BF16), jnp.asarray(gmap, BF16), bsz=bsz, s_len=s_len)

    b16 = lambda w: w.astype(BF16)
    row = lambda v: v.reshape(1, d)
    return _post(x2, oa, ob, gab, p_i.reshape(t_len, -1), b16(w_up_a), b16(w_up_b), b16(w_out), row(norm_mlp_g),
                 b16(w_mlp1), b16(w_mlp2), row(norm_ple_g), b16(w_ple_gate), b16(w_ple), row(final_g)
                 ).reshape(bsz, s_len, d)


def kernel(x, p, norm_mix_g, w_in, pe_ck, w_ck1, w_ck2, pe_cv, w_cv1, w_cv2, w_up_a, w_up_b, w_out,
           norm_mlp_g, w_mlp1, w_mlp2, norm_ple_g, w_ple_gate, w_ple, norm_final_g):
    depth = w_in.shape[0]
    assert depth == 1, "the fused tail applies the final norm inside the single layer"
    return _layer(x, p[0], norm_mix_g[0], w_in[0], pe_ck[0], w_ck1[0], w_ck2[0], pe_cv[0], w_cv1[0], w_cv2[0],
                  w_up_a[0], w_up_b[0], w_out[0], norm_mlp_g[0], w_mlp1[0], w_mlp2[0], norm_ple_g[0],
                  w_ple_gate[0], w_ple[0], norm_final_g)
```
